```python
import math
import jax, jax.numpy as jnp
from jax import lax
import numpy as np

D_MODEL = 4096
BATCH = 2
SEQ = 4096
DEPTH = 2
DEC_BATCH = 32
DEC_SEQ = 32
PAST_LEN = 1024

CHUNK = 64
N_A_LAYERS = DEPTH // 2
N_B_LAYERS = DEPTH - N_A_LAYERS

SSM_EXPAND = 2
SSM_INNER = SSM_EXPAND * D_MODEL
SSM_HEAD_DIM = 64
SSM_HEADS = SSM_INNER // SSM_HEAD_DIM
SSM_GROUPS = 8
SSM_STATE = 128
SSM_CONV = 4
SSM_CONV_DIM = SSM_INNER + 2 * SSM_GROUPS * SSM_STATE
SSM_IN_DIM = SSM_INNER + SSM_CONV_DIM + SSM_HEADS
SSM_BLOCK = CHUNK

SB_HEAD_DIM = 128
SB_HEADS = D_MODEL // SB_HEAD_DIM
SB_Q_BLOCK = 128

N_EXPERTS = 32
N_EXPERT_GROUPS = 4
EXPERTS_PER_GROUP = N_EXPERTS // N_EXPERT_GROUPS
TOP_K = 2
D_EXPERT = D_MODEL // 4
MOE_BLOCK = 128

DN_ALPHA = (2 * DEPTH) ** 0.25
DN_BETA = (8 * DEPTH) ** -0.25
LN_EPS = 1e-5
RMS_EPS = 1e-5

kernel_name = 'yoco_ssd_stickbreaking_moe_stream_step'


def layer_norm(x, g, b):
    xf = x.astype(jnp.float32)
    mu = jnp.mean(xf, -1, keepdims=True)
    var = jnp.mean(jnp.square(xf - mu), -1, keepdims=True)
    return ((xf - mu) * lax.rsqrt(var + LN_EPS) * g.astype(jnp.float32) + b.astype(jnp.float32)).astype(x.dtype)


def ada_params(c, w_ada, b_ada):
    m = jnp.einsum('bd,de->be', jax.nn.silu(c), w_ada) + b_ada
    return jnp.split(m[:, None, :], 6, axis=-1)


def causal_dwconv(x, hist, w, b):
    xp = jnp.concatenate([hist.astype(x.dtype), x], axis=1)
    y = lax.conv_general_dilated(xp, w[:, None, :].astype(x.dtype), window_strides=(1,), padding='VALID',
                                 dimension_numbers=('NWC', 'WIO', 'NWC'), feature_group_count=x.shape[-1])
    return y + b, xp[:, xp.shape[1] - (SSM_CONV - 1):]


def ssd_scan(xh, dt, a, bm, cm, h0):
    f32 = jnp.float32
    nb_, L = xh.shape[:2]
    blk = min(SSM_BLOCK, L)
    nc = L // blk
    hpg = SSM_HEADS // SSM_GROUPS
    x = (xh.astype(f32) * dt[..., None]).reshape(nb_, nc, blk, SSM_GROUPS, hpg, SSM_HEAD_DIM)
    da_cs = jnp.cumsum((dt * a).reshape(nb_, nc, blk, SSM_GROUPS, hpg), axis=2)
    B = bm.astype(f32).reshape(nb_, nc, blk, SSM_GROUPS, SSM_STATE)
    C = cm.astype(f32).reshape(nb_, nc, blk, SSM_GROUPS, SSM_STATE)
    tril = jnp.tril(jnp.ones((blk, blk), bool))[:, :, None, None]
    seg = da_cs[:, :, :, None] - da_cs[:, :, None, :]
    decay = jnp.exp(jnp.where(tril, seg, -jnp.inf))
    cb = jnp.einsum('bclgn,bcsgn->bclsg', C, B)
    y_diag = jnp.einsum('bclsg,bclsgj,bcsgjp->bclgjp', cb, decay, x)
    decay_states = jnp.exp(da_cs[:, :, -1:] - da_cs)
    states = jnp.einsum('bclgn,bclgj,bclgjp->bcgjpn', B, decay_states, x)
    block_decay = jnp.exp(da_cs[:, :, -1])

    def step(h, inp):
        st, dec = inp
        return dec[..., None, None] * h + st, h

    h_init = h0.astype(f32).reshape(nb_, SSM_GROUPS, hpg, SSM_HEAD_DIM, SSM_STATE)
    h_fin, h_prev = lax.scan(step, h_init, (jnp.moveaxis(states, 1, 0), jnp.moveaxis(block_decay, 1, 0)))
    h_prev = jnp.moveaxis(h_prev, 0, 1)
    y_off = jnp.einsum('bclgn,bcgjpn,bclgj->bclgjp', C, h_prev, jnp.exp(da_cs))
    y = (y_diag + y_off).reshape(nb_, L, SSM_HEADS, SSM_HEAD_DIM)
    return y, h_fin.reshape(nb_, SSM_HEADS, SSM_HEAD_DIM, SSM_STATE)


def mamba2_mixer(h, conv_hist, ssm_h0, w_in, conv_w, conv_b, dt_bias, a_log, d_skip, norm_g, w_out):
    f32 = jnp.float32
    nb_, L, _ = h.shape
    zxbcdt = h @ w_in
    z, xbc, dt = jnp.split(zxbcdt, [SSM_INNER, SSM_INNER + SSM_CONV_DIM], axis=-1)
    xbc, new_hist = causal_dwconv(xbc, conv_hist, conv_w, conv_b)
    xbc = jax.nn.silu(xbc)
    xs, bm, cm = jnp.split(xbc, [SSM_INNER, SSM_INNER + SSM_GROUPS * SSM_STATE], axis=-1)
    xs = xs.reshape(nb_, L, SSM_HEADS, SSM_HEAD_DIM)
    bm = bm.reshape(nb_, L, SSM_GROUPS, SSM_STATE)
    cm = cm.reshape(nb_, L, SSM_GROUPS, SSM_STATE)
    dt = jax.nn.softplus(dt.astype(f32) + dt_bias.astype(f32))
    a = -jnp.exp(a_log.astype(f32))
    y, h_new = ssd_scan(xs, dt, a, bm, cm, ssm_h0)
    y = y + d_skip.astype(f32)[:, None] * xs.astype(f32)
    y = y.reshape(nb_, L, SSM_INNER) * jax.nn.silu(z.astype(f32))
    yg = y.reshape(nb_, L, SSM_GROUPS, SSM_INNER // SSM_GROUPS)
    yg = yg * lax.rsqrt(jnp.mean(yg * yg, -1, keepdims=True) + RMS_EPS)
    y = yg.reshape(nb_, L, SSM_INNER) * norm_g.astype(f32)
    return y.astype(h.dtype) @ w_out, new_hist, h_new.astype(ssm_h0.dtype)


def stick_breaking(q, k, v, q_offset):
    nb_, Lq = q.shape[:2]
    Lk = k.shape[1]
    qb = min(SB_Q_BLOCK, Lq)
    nblk = Lq // qb
    scale = SB_HEAD_DIM ** -0.5
    k_pos = jnp.arange(Lk)
    qs = jnp.moveaxis(q.reshape(nb_, nblk, qb, SB_HEADS, SB_HEAD_DIM), 1, 0)

    def block(args):
        qblk, i = args
        t = q_offset + i * qb + jnp.arange(qb)
        z = jnp.einsum('bqhd,bkhd->bhqk', qblk, k, preferred_element_type=jnp.float32) * scale
        mask = k_pos[None, :] < t[:, None]
        ls = jax.nn.log_sigmoid(z)
        log1m = jnp.where(mask, ls - z, 0.0)
        after = lax.cumsum(log1m, axis=3, reverse=True) - log1m
        w = jnp.where(mask, jnp.exp(ls + after), 0.0)
        return jnp.einsum('bhqk,bkhd->bqhd', w.astype(v.dtype), v)

    out = lax.map(block, (qs, jnp.arange(nblk)))
    return jnp.moveaxis(out, 0, 1).reshape(nb_, Lq, SB_HEADS * SB_HEAD_DIM)


def route(x2d, router_w, router_b):
    s = jax.nn.sigmoid(jnp.dot(x2d, router_w, preferred_element_type=jnp.float32))
    sel = s + router_b.astype(jnp.float32)
    grp_score = lax.top_k(sel.reshape(-1, N_EXPERT_GROUPS, EXPERTS_PER_GROUP), 2)[0].sum(-1)
    g_idx = jnp.argmax(grp_score, axis=-1)
    in_grp = (jnp.arange(N_EXPERTS) // EXPERTS_PER_GROUP)[None, :] == g_idx[:, None]
    _, idx = lax.top_k(jnp.where(in_grp, sel, -jnp.inf), TOP_K)
    gs = jnp.take_along_axis(s, idx, axis=1)
    return idx, gs / jnp.sum(gs, -1, keepdims=True)


def moe_ffn(h, router_w, router_b, w_gate, w_up, w_down):
    nb_, L, D = h.shape
    x = h.reshape(-1, D)
    T = x.shape[0]
    idx, gates = route(x, router_w, router_b)
    A = T * TOP_K
    flat_e = idx.reshape(-1)
    order = jnp.argsort(flat_e)
    sorted_e = flat_e[order]
    counts = jnp.bincount(flat_e, length=N_EXPERTS)
    starts = jnp.cumsum(counts) - counts
    padded = (counts + MOE_BLOCK - 1) // MOE_BLOCK * MOE_BLOCK
    pad_ends = jnp.cumsum(padded)
    pad_starts = pad_ends - padded
    slot = pad_starts[sorted_e] + jnp.arange(A) - starts[sorted_e]
    n_blocks = -(-A // MOE_BLOCK) + N_EXPERTS
    P = n_blocks * MOE_BLOCK
    tok = (order // TOP_K).astype(jnp.int32)
    tok_of_slot = jnp.full((P,), T, jnp.int32).at[slot].set(tok)
    blk_expert = jnp.minimum(jnp.searchsorted(pad_ends, jnp.arange(n_blocks) * MOE_BLOCK, side='right'), N_EXPERTS - 1)
    xpad = jnp.concatenate([x, jnp.zeros((1, D), x.dtype)], axis=0)[tok_of_slot].reshape(n_blocks, MOE_BLOCK, D)

    def expert_block(args):
        xb, e = args
        hid = jax.nn.silu(xb @ w_gate[e]) * (xb @ w_up[e])
        return hid @ w_down[e]

    yb = lax.map(expert_block, (xpad, blk_expert)).reshape(P, D)
    y_assign = yb[slot] * gates.reshape(-1)[order][:, None].astype(yb.dtype)
    out = jnp.zeros((T, D), yb.dtype).at[tok].add(y_assign)
    return out.reshape(nb_, L, D)


def run_group(x, c, conv_hist, ssm_h0, past_k, past_v, weights):
    (w_ada, b_ada, ln_g, ln_b, ssm_w_in, ssm_conv_w, ssm_conv_b, ssm_dt_bias, ssm_a_log, ssm_d,
     ssm_norm_g, ssm_w_out, sb_w_k, sb_w_v, sb_w_q, sb_w_o, router_w, router_b,
     moe_w_gate, moe_w_up, moe_w_down) = weights
    q_offset = 0 if past_k is None else past_k.shape[1]
    new_conv, new_ssm = [], []
    k_rows = v_rows = kv_k = kv_v = None
    for layer in range(DEPTH):
        shift1, scale1, gate1, shift2, scale2, gate2 = ada_params(c, w_ada[layer], b_ada[layer])
        h = x * (1 + scale1) + shift1
        if layer < N_A_LAYERS:
            m, hc, hs = mamba2_mixer(h, conv_hist[layer], ssm_h0[layer], ssm_w_in[layer], ssm_conv_w[layer],
                                     ssm_conv_b[layer], ssm_dt_bias[layer], ssm_a_log[layer], ssm_d[layer],
                                     ssm_norm_g[layer], ssm_w_out[layer])
            new_conv.append(hc)
            new_ssm.append(hs)
        else:
            j = layer - N_A_LAYERS
            nb_, L, _ = h.shape
            q = (h @ sb_w_q[j]).reshape(nb_, L, SB_HEADS, SB_HEAD_DIM)
            m = stick_breaking(q, kv_k, kv_v, q_offset) @ sb_w_o[j]
        x = layer_norm(DN_ALPHA * x + gate1 * m, ln_g[layer, 0], ln_b[layer, 0])
        h = x * (1 + scale2) + shift2
        f = moe_ffn(h, router_w, router_b, moe_w_gate[layer], moe_w_up[layer], moe_w_down[layer])
        x = layer_norm(DN_ALPHA * x + gate2 * f, ln_g[layer, 1], ln_b[layer, 1])
        if layer == N_A_LAYERS - 1:
            nb_, L, _ = x.shape
            k_rows = (x @ sb_w_k).reshape(nb_, L, SB_HEADS, SB_HEAD_DIM)
            v_rows = (x @ sb_w_v).reshape(nb_, L, SB_HEADS, SB_HEAD_DIM)
            kv_k = k_rows if past_k is None else jnp.concatenate([past_k.astype(k_rows.dtype), k_rows], axis=1)
            kv_v = v_rows if past_v is None else jnp.concatenate([past_v.astype(v_rows.dtype), v_rows], axis=1)
    return x, jnp.stack(new_conv), jnp.stack(new_ssm), k_rows, v_rows


def setup_inputs(seed: int = 0) -> dict:
    key = jax.random.key(seed)
    ks = jax.random.split(key, 32)
    f32 = jnp.float32
    D = D_MODEL

    def nrm(k, shape, s):
        return jax.random.normal(k, shape, f32) * s

    dt = jnp.exp(jax.random.uniform(ks[10], (N_A_LAYERS, SSM_HEADS), f32, math.log(1e-3), math.log(1e-1)))
    dt_bias = dt + jnp.log(-jnp.expm1(-dt))
    return {
        'x_prompt': nrm(ks[0], (BATCH, SEQ, D), 1.0),
        'x_sample': nrm(ks[1], (DEC_BATCH, DEC_SEQ, D), 1.0),
        'c_prompt': nrm(ks[2], (BATCH, D), 1.0),
        'c_sample': nrm(ks[3], (DEC_BATCH, D), 1.0),
        'state_conv': nrm(ks[4], (N_A_LAYERS, DEC_BATCH, SSM_CONV - 1, SSM_CONV_DIM), 1.0),
        'state_ssm': nrm(ks[5], (N_A_LAYERS, DEC_BATCH, SSM_HEADS, SSM_HEAD_DIM, SSM_STATE), 0.1),
        'cache_k': nrm(ks[6], (DEC_BATCH, PAST_LEN, SB_HEADS, SB_HEAD_DIM), 1.0),
        'cache_v': nrm(ks[7], (DEC_BATCH, PAST_LEN, SB_HEADS, SB_HEAD_DIM), 1.0),
        'w_ada': nrm(ks[8], (DEPTH, D, 6 * D), D ** -0.5),
        'b_ada': nrm(ks[9], (DEPTH, 6 * D), 0.02),
        'ln_g': 1.0 + nrm(ks[11], (DEPTH, 2, D), 0.02),
        'ln_b': nrm(ks[12], (DEPTH, 2, D), 0.02),
        'ssm_w_in': nrm(ks[13], (N_A_LAYERS, D, SSM_IN_DIM), D ** -0.5),
        'ssm_conv_w': nrm(ks[14], (N_A_LAYERS, SSM_CONV, SSM_CONV_DIM), SSM_CONV ** -0.5),
        'ssm_conv_b': nrm(ks[15], (N_A_LAYERS, SSM_CONV_DIM), 0.02),
        'ssm_dt_bias': dt_bias,
        'ssm_a_log': jnp.log(jax.random.uniform(ks[16], (N_A_LAYERS, SSM_HEADS), f32, 1.0, 16.0)),
        'ssm_d': 1.0 + nrm(ks[17], (N_A_LAYERS, SSM_HEADS), 0.02),
        'ssm_norm_g': 1.0 + nrm(ks[18], (N_A_LAYERS, SSM_INNER), 0.02),
        'ssm_w_out': nrm(ks[19], (N_A_LAYERS, SSM_INNER, D), DN_BETA * SSM_INNER ** -0.5),
        'sb_w_k': nrm(ks[20], (D, D), D ** -0.5),
        'sb_w_v': nrm(ks[21], (D, D), D ** -0.5),
        'sb_w_q': nrm(ks[22], (N_B_LAYERS, D, D), D ** -0.5),
        'sb_w_o': nrm(ks[23], (N_B_LAYERS, D, D), DN_BETA * D ** -0.5),
        'router_w': nrm(ks[24], (D, N_EXPERTS), D ** -0.5),
        'router_b': nrm(ks[25], (N_EXPERTS,), 0.01),
        'moe_w_gate': nrm(ks[26], (DEPTH, N_EXPERTS, D, D_EXPERT), D ** -0.5),
        'moe_w_up': nrm(ks[27], (DEPTH, N_EXPERTS, D, D_EXPERT), D ** -0.5),
        'moe_w_down': nrm(ks[28], (DEPTH, N_EXPERTS, D_EXPERT, D), DN_BETA * D_EXPERT ** -0.5),
    }


def reference(x_prompt, x_sample, c_prompt, c_sample, state_conv, state_ssm, cache_k, cache_v,
              w_ada, b_ada, ln_g, ln_b, ssm_w_in, ssm_conv_w, ssm_conv_b, ssm_dt_bias, ssm_a_log, ssm_d,
              ssm_norm_g, ssm_w_out, sb_w_k, sb_w_v, sb_w_q, sb_w_o, router_w, router_b,
              moe_w_gate, moe_w_up, moe_w_down):
    weights = (w_ada, b_ada, ln_g, ln_b, ssm_w_in, ssm_conv_w, ssm_conv_b, ssm_dt_bias, ssm_a_log, ssm_d,
               ssm_norm_g, ssm_w_out, sb_w_k, sb_w_v, sb_w_q, sb_w_o, router_w, router_b,
               moe_w_gate, moe_w_up, moe_w_down)
    nbp = x_prompt.shape[0]
    zero_conv = jnp.zeros((N_A_LAYERS, nbp, SSM_CONV - 1, SSM_CONV_DIM), x_prompt.dtype)
    zero_ssm = jnp.zeros((N_A_LAYERS, nbp, SSM_HEADS, SSM_HEAD_DIM, SSM_STATE), state_ssm.dtype)
    y_prompt, conv_p, ssm_p, k_p, v_p = run_group(x_prompt, c_prompt, zero_conv, zero_ssm, None, None, weights)
    y_sample, conv_s, ssm_s, k_s, v_s = run_group(x_sample, c_sample, state_conv, state_ssm, cache_k, cache_v, weights)
    return (y_prompt, y_sample, conv_p, ssm_p, k_p, v_p, conv_s, ssm_s, k_s, v_s)
```

```python
import functools

import jax
import jax.numpy as jnp
from jax import lax
from jax.experimental import pallas as pl
from jax.experimental.pallas import tpu as pltpu

F32 = jnp.float32
BF16 = jnp.bfloat16
HIGHEST = lax.Precision.HIGHEST

DEPTH = 2
N_EXPERT_GROUPS = 4
TOP_K = 2
DN_ALPHA = (2 * DEPTH) ** 0.25
LN_EPS = 1e-5
RMS_EPS = 1e-5
SSD_CHUNK = 64

ROWS = 32
VMEM_LIMIT = 52 * 1024 * 1024
EXP_ZERO_BELOW = -105.0


def _params(*sem):
    return pltpu.CompilerParams(dimension_semantics=sem, vmem_limit_bytes=VMEM_LIMIT)


def _pick_tile(n, cap, mult=128):
    best = None
    for c in range(mult, min(n, cap) + 1, mult):
        if n % c == 0:
            best = c
    assert best is not None, (n, cap)
    return best


def _ada_kernel(c_ref, w_ref, b_ref, o_ref):
    c = c_ref[...]
    s = (c * jax.nn.sigmoid(c)).astype(BF16)
    o_ref[...] = jnp.dot(s, w_ref[...].astype(BF16), preferred_element_type=F32) + b_ref[...]


def _ada(c_all, w_ada, b_ada):
    depth, d, n = w_ada.shape
    bc = c_all.shape[0]
    tn = 512
    return pl.pallas_call(
        _ada_kernel,
        grid=(depth, n // tn),
        in_specs=[pl.BlockSpec((bc, d), lambda l, j: (0, 0)),
                  pl.BlockSpec((None, d, tn), lambda l, j: (l, 0, j)),
                  pl.BlockSpec((None, 1, tn), lambda l, j: (l, 0, j))],
        out_specs=pl.BlockSpec((None, bc, tn), lambda l, j: (l, 0, j)),
        out_shape=jax.ShapeDtypeStruct((depth, bc, n), F32),
        compiler_params=_params("arbitrary", "arbitrary"),
        name="ada",
    )(c_all, w_ada, b_ada.reshape(depth, 1, n))


def _mod_kernel(x_ref, sc_ref, sh_ref, h_ref):
    gb = sc_ref.shape[0]
    rows, d = x_ref.shape
    x = x_ref[...].reshape(gb, rows // gb, d)
    h = x * (1.0 + sc_ref[...]) + sh_ref[...]
    h_ref[...] = h.reshape(rows, d).astype(h_ref.dtype)


def _modulate(x, modg, scale_blk, shift_blk, gb=8):
    t, d = x.shape
    g = modg.shape[0]
    return pl.pallas_call(
        _mod_kernel,
        grid=(g // gb,),
        in_specs=[pl.BlockSpec((gb * ROWS, d), lambda i: (i, 0)),
                  pl.BlockSpec((gb, 1, d), lambda i: (i, 0, scale_blk)),
                  pl.BlockSpec((gb, 1, d), lambda i: (i, 0, shift_blk))],
        out_specs=pl.BlockSpec((gb * ROWS, d), lambda i: (i, 0)),
        out_shape=jax.ShapeDtypeStruct((t, d), BF16),
        compiler_params=_params("arbitrary"),
        name="modulate",
    )(x, modg, modg)


def _ln_kernel(*refs, n_m, gated_m, with_h, with_router, with_xb):
    it = iter(refs)
    x_ref = next(it)
    m_refs = [next(it) for _ in range(n_m)]
    mg_refs = [next(it) for _ in range(n_m)] if gated_m else []
    gate_ref, g_ref, b_ref = next(it), next(it), next(it)
    sc_ref = sh_ref = rw_ref = None
    if with_h:
        sc_ref, sh_ref = next(it), next(it)
    if with_router:
        rw_ref = next(it)
    xo_ref = next(it)
    h_ref = next(it) if with_h else None
    s_ref = next(it) if with_router else None
    xb_ref = next(it) if with_xb else None

    gb = gate_ref.shape[0]
    rows, d = x_ref.shape
    if gated_m:
        m = m_refs[0][...] * mg_refs[0][...]
        for mr, gr in zip(m_refs[1:], mg_refs[1:]):
            m = m + mr[...] * gr[...]
    else:
        m = m_refs[0][...]
        for mr in m_refs[1:]:
            m = m + mr[...]
    v = DN_ALPHA * x_ref[...] + (gate_ref[...] * m.reshape(gb, rows // gb, d)).reshape(rows, d)
    mu = jnp.mean(v, axis=-1, keepdims=True)
    vc = v - mu
    var = jnp.mean(vc * vc, axis=-1, keepdims=True)
    xn = vc * lax.rsqrt(var + LN_EPS) * g_ref[...] + b_ref[...]
    xo_ref[...] = xn
    if with_xb:
        xb_ref[...] = xn.astype(BF16)
    if with_h:
        h = xn.reshape(gb, rows // gb, d) * (1.0 + sc_ref[...]) + sh_ref[...]
        h = h.reshape(rows, d)
        h_ref[...] = h.astype(BF16)
        if with_router:
            logits = jnp.dot(h, rw_ref[...], preferred_element_type=F32, precision=HIGHEST)
            s_ref[...] = jax.nn.sigmoid(logits)


def _ln(x, ms, gate_src, ln_g, ln_b, *, m_gates=None, next_mod=None, router_w=None, with_xb=False, gb=8):
    t, d = x.shape
    modg, gate_blk = gate_src
    g = modg.shape[0]
    tr = gb * ROWS
    row = pl.BlockSpec((tr, d), lambda i: (i, 0))
    vec = pl.BlockSpec((1, d), lambda i: (0, 0))

    def modspec(blk):
        return pl.BlockSpec((gb, 1, d), lambda i: (i, 0, blk))

    args = [x] + list(ms)
    specs = [row] + [row] * len(ms)
    if m_gates is not None:
        args += list(m_gates)
        specs += [pl.BlockSpec((tr, 1), lambda i: (i, 0))] * len(ms)
    args += [modg, ln_g.reshape(1, d), ln_b.reshape(1, d)]
    specs += [modspec(gate_blk), vec, vec]
    outs = [jax.ShapeDtypeStruct((t, d), F32)]
    out_specs = [row]
    if next_mod is not None:
        nm, sc_blk, sh_blk = next_mod
        args += [nm, nm]
        specs += [modspec(sc_blk), modspec(sh_blk)]
        outs.append(jax.ShapeDtypeStruct((t, d), BF16))
        out_specs.append(row)
    if router_w is not None:
        e = router_w.shape[1]
        args.append(router_w)
        specs.append(pl.BlockSpec((d, e), lambda i: (0, 0)))
        outs.append(jax.ShapeDtypeStruct((t, e), F32))
        out_specs.append(pl.BlockSpec((tr, e), lambda i: (i, 0)))
    if with_xb:
        outs.append(jax.ShapeDtypeStruct((t, d), BF16))
        out_specs.append(row)
    kern = functools.partial(_ln_kernel, n_m=len(ms), gated_m=m_gates is not None,
                             with_h=next_mod is not None, with_router=router_w is not None, with_xb=with_xb)
    return pl.pallas_call(
        kern, grid=(g // gb,), in_specs=specs, out_specs=out_specs, out_shape=outs,
        compiler_params=_params("arbitrary"), name="ln",
    )(*args)


def _mm_kernel(*refs, with_acc):
    if with_acc:
        x_ref, w_ref, a_ref, o_ref, wb_ref = refs
    else:
        x_ref, w_ref, o_ref, wb_ref = refs
        a_ref = None

    @pl.when(pl.program_id(1) == 0)
    def _():
        wb_ref[...] = w_ref[...].astype(BF16)

    r = jnp.dot(x_ref[...], wb_ref[...], preferred_element_type=F32)
    if with_acc:
        r = r + a_ref[...]
    o_ref[...] = r.astype(o_ref.dtype)


def _matmul(x, w, *, lead=(), kblk=None, x_koff=0, w_koff=0, n_off=0, n_out=None, tm=1024, tn=512,
            out_dtype=F32, acc=None, name="matmul"):
    t = x.shape[0]
    kfull, nfull = w.shape[-2:]
    kblk = kfull if kblk is None else kblk
    n_out = nfull if n_out is None else n_out
    tm = _pick_tile(t, tm)
    assert n_out % tn == 0
    nl = len(lead)
    in_specs = [pl.BlockSpec((tm, kblk), lambda j, i: (i, x_koff)),
                pl.BlockSpec((None,) * nl + (kblk, tn), lambda j, i: tuple(lead) + (w_koff, j + n_off))]
    args = [x, w]
    if acc is not None:
        in_specs.append(pl.BlockSpec((tm, tn), lambda j, i: (i, j)))
        args.append(acc)
    return pl.pallas_call(
        functools.partial(_mm_kernel, with_acc=acc is not None),
        grid=(n_out // tn, t // tm),
        in_specs=in_specs,
        out_specs=pl.BlockSpec((tm, tn), lambda j, i: (i, j)),
        out_shape=jax.ShapeDtypeStruct((t, n_out), out_dtype),
        scratch_shapes=[pltpu.VMEM((kblk, tn), BF16)],
        compiler_params=_params("arbitrary", "arbitrary"),
        name=name,
    )(*args)


def _conv_kernel(*refs, with_hist, kw):
    if with_hist:
        x_ref, hist_ref, w_ref, b_ref, o_ref, xp_ref = refs
    else:
        x_ref, w_ref, b_ref, o_ref, xp_ref = refs
    tr = x_ref.shape[0]
    if with_hist:
        xp_ref[0:8, :] = hist_ref[...]
    else:
        @pl.when(pl.program_id(2) == 0)
        def _():
            xp_ref[0:8, :] = jnp.zeros((8, xp_ref.shape[1]), F32)
    xp_ref[8:8 + tr, :] = x_ref[...]
    w = w_ref[...]
    y = b_ref[...] + w[kw - 1:kw, :] * x_ref[...]
    for k in range(1, kw):
        y = y + w[kw - 1 - k:kw - k, :] * xp_ref[8 - k:8 - k + tr, :]
    o_ref[...] = (y * jax.nn.sigmoid(y)).astype(o_ref.dtype)
    if not with_hist:
        xp_ref[0:8, :] = xp_ref[tr:tr + 8, :]


def _conv_silu(zx, col0, conv_w, conv_b, *, row0, n_seq, seq_len, hist8, tc=1024):
    kw, c = conv_w.shape
    tr = min(seq_len, 512)
    nrt = seq_len // tr
    assert seq_len % tr == 0 and c % tc == 0 and col0 % tc == 0 and row0 % tr == 0
    with_hist = hist8 is not None
    assert not with_hist or nrt == 1
    rb0, cb0 = row0 // tr, col0 // tc
    in_specs = [pl.BlockSpec((tr, tc), lambda s, j, r: (rb0 + s * nrt + r, cb0 + j))]
    args = [zx]
    if with_hist:
        in_specs.append(pl.BlockSpec((None, 8, tc), lambda s, j, r: (s, 0, j)))
        args.append(hist8)
    in_specs += [pl.BlockSpec((kw, tc), lambda s, j, r: (0, j)),
                 pl.BlockSpec((1, tc), lambda s, j, r: (0, j))]
    args += [conv_w, conv_b.reshape(1, c)]
    return pl.pallas_call(
        functools.partial(_conv_kernel, with_hist=with_hist, kw=kw),
        grid=(n_seq, c // tc, nrt),
        in_specs=in_specs,
        out_specs=pl.BlockSpec((tr, tc), lambda s, j, r: (s * nrt + r, j)),
        out_shape=jax.ShapeDtypeStruct((n_seq * seq_len, c), BF16),
        scratch_shapes=[pltpu.VMEM((tr + 8, tc), F32)],
        compiler_params=_params("arbitrary", "arbitrary", "arbitrary"),
        name="conv_silu",
    )(*args)


def _ssd_kernel(*refs, with_h0, hpg, p):
    if with_h0:
        (x_ref, b_ref, c_ref, z_ref, dt_ref, dtb_ref, alog_ref, dsk_ref, ng_ref, h0_ref,
         y_ref, hf_ref, ht_ref, yd_ref) = refs
    else:
        (x_ref, b_ref, c_ref, z_ref, dt_ref, dtb_ref, alog_ref, dsk_ref, ng_ref,
         y_ref, hf_ref, ht_ref, yd_ref) = refs
        h0_ref = None
    g = pl.program_id(1)
    ci = pl.program_id(2)
    lc, gw = x_ref.shape
    nh = dt_ref.shape[1]

    @pl.when(ci == 0)
    def _():
        if with_h0:
            ht_ref[...] = h0_ref[...].T
        else:
            ht_ref[...] = jnp.zeros(ht_ref.shape, F32)

    dt = jax.nn.softplus(dt_ref[...] + dtb_ref[...])
    da = dt * (-jnp.exp(alog_ref[...]))
    head_of_col = g * hpg + lax.broadcasted_iota(jnp.int32, (nh, gw), 1) // p
    expand = (lax.broadcasted_iota(jnp.int32, (nh, gw), 0) == head_of_col).astype(F32)
    dt_e = jnp.dot(dt, expand, preferred_element_type=F32, precision=HIGHEST)
    da_e = jnp.dot(da, expand, preferred_element_type=F32, precision=HIGHEST)
    r_i = lax.broadcasted_iota(jnp.int32, (lc, lc), 0)
    c_i = lax.broadcasted_iota(jnp.int32, (lc, lc), 1)
    tril = r_i >= c_i
    cs = jnp.dot(tril.astype(F32), da_e, preferred_element_type=F32, precision=HIGHEST)
    cs_last = cs[lc - 1:lc, :]
    eye = r_i == c_i

    xs = x_ref[...].astype(F32)
    xdt = xs * dt_e
    xdt_b = xdt.astype(BF16)
    bm = b_ref[...]
    cm = c_ref[...]
    cb = lax.dot_general(cm, bm, (((1,), (1,)), ((), ())), preferred_element_type=F32)

    for j in range(hpg):
        col = jnp.broadcast_to(cs[:, j * p:j * p + 1], (lc, lc))
        row = jnp.sum(jnp.where(eye, col, 0.0), axis=0, keepdims=True)
        dec = jnp.exp(jnp.where(tril, col - row, -jnp.inf))
        mj = (cb * dec).astype(BF16)
        yd_ref[:, j * p:(j + 1) * p] = jnp.dot(mj, xdt_b[:, j * p:(j + 1) * p], preferred_element_type=F32)

    ht = ht_ref[...]
    y_off = jnp.dot(cm, ht.astype(BF16), preferred_element_type=F32) * jnp.exp(cs)
    xw = (xdt * jnp.exp(cs_last - cs)).astype(BF16)
    st_t = lax.dot_general(bm, xw, (((0,), (0,)), ((), ())), preferred_element_type=F32)
    ht_new = jnp.exp(cs_last) * ht + st_t
    ht_ref[...] = ht_new

    @pl.when(ci == pl.num_programs(2) - 1)
    def _():
        hf_ref[...] = ht_new.T

    y = yd_ref[...] + y_off + dsk_ref[...] * xs
    z = z_ref[...]
    y = y * (z * jax.nn.sigmoid(z))
    ms = jnp.mean(y * y, axis=-1, keepdims=True)
    y_ref[...] = (y * lax.rsqrt(ms + RMS_EPS) * ng_ref[...]).astype(y_ref.dtype)


def _ssd(act, zx, dt_raw, dt_bias, a_log, d_skip, norm_g, h0, *, row0, act_row0, n_seq, seq_len, lc,
         n_heads, p, n_state, n_groups):
    inner = n_heads * p
    hpg = n_heads // n_groups
    gw = hpg * p
    nc = seq_len // lc
    assert seq_len % lc == 0 and row0 % lc == 0 and act_row0 % lc == 0 and gw % 128 == 0 and n_state % 128 == 0
    rb0, ab0 = row0 // lc, act_row0 // lc
    bcol = inner // n_state
    with_h0 = h0 is not None
    d_e = jnp.repeat(d_skip.astype(F32), p).reshape(1, inner)
    in_specs = [
        pl.BlockSpec((lc, gw), lambda s, g, c: (ab0 + s * nc + c, g)),
        pl.BlockSpec((lc, n_state), lambda s, g, c: (ab0 + s * nc + c, bcol + g)),
        pl.BlockSpec((lc, n_state), lambda s, g, c: (ab0 + s * nc + c, bcol + n_groups + g)),
        pl.BlockSpec((lc, gw), lambda s, g, c: (rb0 + s * nc + c, g)),
        pl.BlockSpec((lc, n_heads), lambda s, g, c: (rb0 + s * nc + c, 0)),
        pl.BlockSpec((1, n_heads), lambda s, g, c: (0, 0)),
        pl.BlockSpec((1, n_heads), lambda s, g, c: (0, 0)),
        pl.BlockSpec((1, gw), lambda s, g, c: (0, g)),
        pl.BlockSpec((1, gw), lambda s, g, c: (0, g)),
    ]
    args = [act, act, act, zx, dt_raw, dt_bias.reshape(1, n_heads).astype(F32),
            a_log.reshape(1, n_heads).astype(F32), d_e, norm_g.reshape(1, inner).astype(F32)]
    if with_h0:
        in_specs.append(pl.BlockSpec((None, gw, n_state), lambda s, g, c: (s, g, 0)))
        args.append(h0)
    return pl.pallas_call(
        functools.partial(_ssd_kernel, with_h0=with_h0, hpg=hpg, p=p),
        grid=(n_seq, n_groups, nc),
        in_specs=in_specs,
        out_specs=[pl.BlockSpec((lc, gw), lambda s, g, c: (s * nc + c, g)),
                   pl.BlockSpec((None, gw, n_state), lambda s, g, c: (s, g, 0))],
        out_shape=[jax.ShapeDtypeStruct((n_seq * seq_len, inner), BF16),
                   jax.ShapeDtypeStruct((n_seq, inner, n_state), F32)],
        scratch_shapes=[pltpu.VMEM((n_state, gw), F32), pltpu.VMEM((lc, gw), F32)],
        compiler_params=_params("arbitrary", "arbitrary", "arbitrary"),
        name="ssd",
    )(*args)


def _sb_block(q, k, v, r_run, acc, scale, diag):
    tq, tk = q.shape[0], k.shape[0]
    z = lax.dot_general(q, k, (((1,), (1,)), ((), ())), preferred_element_type=F32) * scale
    ls = jnp.minimum(z, 0.0) - jnp.log(1.0 + jnp.exp(-jnp.abs(z)))
    l1 = ls - z
    if diag:
        mask = lax.broadcasted_iota(jnp.int32, (tq, tk), 1) < lax.broadcasted_iota(jnp.int32, (tq, tk), 0)
        l1 = jnp.where(mask, l1, 0.0)
    later = (lax.broadcasted_iota(jnp.int32, (tk, tk), 0) > lax.broadcasted_iota(jnp.int32, (tk, tk), 1))
    after = jnp.dot(l1, later.astype(F32), preferred_element_type=F32, precision=HIGHEST)
    w = jnp.exp(ls + after + r_run)
    if diag:
        w = jnp.where(mask, w, 0.0)
    acc = acc + jnp.dot(w.astype(BF16), v, preferred_element_type=F32)
    r_run = r_run + jnp.sum(l1, axis=1, keepdims=True)
    return r_run, acc


def _sb_kernel(*refs, hb, dh, tk, scale, with_past):
    if with_past:
        q_ref, k_ref, v_ref, pk_ref, pv_ref, o_ref = refs
    else:
        q_ref, k_ref, v_ref, o_ref = refs
    tq = q_ref.shape[0]

    def head(ref, rows, h):
        return ref[rows, h * dh:(h + 1) * dh]

    if with_past:
        own = slice(None)
        n_prev = pk_ref.shape[0] // tk
    else:
        qi = pl.program_id(2)
        own = pl.ds(pl.multiple_of(qi * tq, tq), tq)
        n_prev = qi * (tq // tk)

    state = []
    for h in range(hb):
        q = head(q_ref, slice(None), h)
        r_run, acc = _sb_block(q, head(k_ref, own, h).astype(BF16), head(v_ref, own, h).astype(BF16),
                               jnp.zeros((tq, 1), F32), jnp.zeros((tq, dh), F32), scale, True)
        state += [r_run, acc]
    rmax = functools.reduce(jnp.maximum, [jnp.max(state[2 * h]) for h in range(hb)])

    src_k, src_v = (pk_ref, pv_ref) if with_past else (k_ref, v_ref)

    def cond(carry):
        kb, rmax = carry[0], carry[1]
        return jnp.logical_and(kb >= 0, rmax > EXP_ZERO_BELOW)

    def body(carry):
        kb, st = carry[0], list(carry[2:])
        rows = pl.ds(pl.multiple_of(kb * tk, tk), tk)
        for h in range(hb):
            q = head(q_ref, slice(None), h)
            st[2 * h], st[2 * h + 1] = _sb_block(q, head(src_k, rows, h).astype(BF16),
                                                 head(src_v, rows, h).astype(BF16),
                                                 st[2 * h], st[2 * h + 1], scale, False)
        rmax = functools.reduce(jnp.maximum, [jnp.max(st[2 * h]) for h in range(hb)])
        return (kb - 1, rmax, *st)

    out = lax.while_loop(cond, body, (jnp.int32(n_prev - 1) if with_past else n_prev - 1, rmax, *state))
    for h in range(hb):
        o_ref[:, h * dh:(h + 1) * dh] = out[3 + 2 * h].astype(o_ref.dtype)


def _sb_prompt(q, k, v, *, n_seq, seq_len, dh, tq=256, hb=4):
    d = q.shape[1]
    nq = seq_len // tq
    assert seq_len % tq == 0 and d % (hb * dh) == 0
    hw = hb * dh
    kv = pl.BlockSpec((seq_len, hw), lambda b, h, i: (b, h))
    return pl.pallas_call(
        functools.partial(_sb_kernel, hb=hb, dh=dh, tk=tq, scale=dh ** -0.5, with_past=False),
        grid=(n_seq, d // hw, nq),
        in_specs=[pl.BlockSpec((tq, hw), lambda b, h, i: (b * nq + i, h)), kv, kv],
        out_specs=pl.BlockSpec((tq, hw), lambda b, h, i: (b * nq + i, h)),
        out_shape=jax.ShapeDtypeStruct((n_seq * seq_len, d), BF16),
        compiler_params=_params("arbitrary", "arbitrary", "arbitrary"),
        name="sb_prompt",
    )(q, k, v)


def _sb_sample(q, k, v, past_k, past_v, *, row0, n_seq, seq_len, dh, tk=256, hb=4):
    d = q.shape[1]
    past_len = past_k.shape[1]
    tk = min(tk, past_len)
    assert past_len % tk == 0 and row0 % seq_len == 0
    hw = hb * dh
    rb0 = row0 // seq_len
    new = pl.BlockSpec((seq_len, hw), lambda s, h: (rb0 + s, h))
    past = pl.BlockSpec((None, past_len, hw), lambda s, h: (s, 0, h))
    return pl.pallas_call(
        functools.partial(_sb_kernel, hb=hb, dh=dh, tk=tk, scale=dh ** -0.5, with_past=True),
        grid=(n_seq, d // hw),
        in_specs=[new, new, new, past, past],
        out_specs=pl.BlockSpec((seq_len, hw), lambda s, h: (s, h)),
        out_shape=jax.ShapeDtypeStruct((n_seq * seq_len, d), BF16),
        compiler_params=_params("arbitrary", "arbitrary"),
        name="sb_sample",
    )(q, k, v, past_k, past_v)


def _moe_up_kernel(be_ref, nbu_ref, x_ref, wg_ref, wu_ref, o_ref, wgb_ref, wub_ref):
    b = pl.program_id(1)
    changed = jnp.logical_or(b == 0, be_ref[b] != be_ref[jnp.maximum(b - 1, 0)])

    @pl.when(changed)
    def _():
        wgb_ref[...] = wg_ref[...].astype(BF16)
        wub_ref[...] = wu_ref[...].astype(BF16)

    @pl.when(b < nbu_ref[0])
    def _():
        x = x_ref[...]
        gt = jnp.dot(x, wgb_ref[...], preferred_element_type=F32)
        up = jnp.dot(x, wub_ref[...], preferred_element_type=F32)
        o_ref[...] = (gt * jax.nn.sigmoid(gt) * up).astype(o_ref.dtype)


def _moe_down_kernel(be_ref, nbu_ref, h_ref, wd_ref, o_ref, wdb_ref):
    b = pl.program_id(1)
    changed = jnp.logical_or(b == 0, be_ref[b] != be_ref[jnp.maximum(b - 1, 0)])

    @pl.when(changed)
    def _():
        wdb_ref[...] = wd_ref[...].astype(BF16)

    @pl.when(b < nbu_ref[0])
    def _():
        o_ref[...] = jnp.dot(h_ref[...], wdb_ref[...], preferred_element_type=F32)


def _moe_experts(xs, blk_expert, n_used, w_gate, w_up, w_down, layer, *, tm, tn_up=512, tn_down=1024):
    pr, d = xs.shape
    de = w_gate.shape[-1]
    nb = pr // tm
    tn_up, tn_down = min(tn_up, de), min(tn_down, d)

    def xrow(c, b, be, nbu):
        return (jnp.minimum(b, nbu[0] - 1), 0)

    hid = pl.pallas_call(
        _moe_up_kernel,
        grid_spec=pltpu.PrefetchScalarGridSpec(
            num_scalar_prefetch=2, grid=(de // tn_up, nb),
            in_specs=[pl.BlockSpec((tm, d), xrow),
                      pl.BlockSpec((None, None, d, tn_up), lambda c, b, be, nbu: (layer, be[b], 0, c)),
                      pl.BlockSpec((None, None, d, tn_up), lambda c, b, be, nbu: (layer, be[b], 0, c))],
            out_specs=pl.BlockSpec((tm, tn_up), lambda c, b, be, nbu: (b, c)),
            scratch_shapes=[pltpu.VMEM((d, tn_up), BF16), pltpu.VMEM((d, tn_up), BF16)]),
        out_shape=jax.ShapeDtypeStruct((pr, de), BF16),
        compiler_params=_params("arbitrary", "arbitrary"),
        name="moe_up",
    )(blk_expert, n_used, xs, w_gate, w_up)
    return pl.pallas_call(
        _moe_down_kernel,
        grid_spec=pltpu.PrefetchScalarGridSpec(
            num_scalar_prefetch=2, grid=(d // tn_down, nb),
            in_specs=[pl.BlockSpec((tm, de), xrow),
                      pl.BlockSpec((None, None, de, tn_down), lambda c, b, be, nbu: (layer, be[b], 0, c))],
            out_specs=pl.BlockSpec((tm, tn_down), lambda c, b, be, nbu: (b, c)),
            scratch_shapes=[pltpu.VMEM((de, tn_down), BF16)]),
        out_shape=jax.ShapeDtypeStruct((pr, d), F32),
        compiler_params=_params("arbitrary", "arbitrary"),
        name="moe_down",
    )(blk_expert, n_used, hid, w_down)


def _route(s, router_b):
    t, e = s.shape
    per = e // N_EXPERT_GROUPS
    sel = s + router_b.astype(F32)
    grp_score = lax.top_k(sel.reshape(t, N_EXPERT_GROUPS, per), 2)[0].sum(-1)
    g_idx = jnp.argmax(grp_score, axis=-1)
    in_grp = (jnp.arange(e) // per)[None, :] == g_idx[:, None]
    _, idx = lax.top_k(jnp.where(in_grp, sel, -jnp.inf), TOP_K)
    gs = jnp.take_along_axis(s, idx, axis=1)
    return idx, gs / jnp.sum(gs, -1, keepdims=True)


def _moe(h, s, router_b, w_gate, w_up, w_down, layer, *, tm=256):
    t, d = h.shape
    e = s.shape[1]
    idx, gates = _route(s, router_b)
    a = t * TOP_K
    flat_e = idx.reshape(-1).astype(jnp.int32)
    order = jnp.argsort(flat_e, stable=True).astype(jnp.int32)
    sorted_e = flat_e[order]
    counts = jnp.bincount(flat_e, length=e).astype(jnp.int32)
    starts = jnp.cumsum(counts) - counts
    padded = (counts + tm - 1) // tm * tm
    pad_ends = jnp.cumsum(padded)
    pad_starts = pad_ends - padded
    slot = pad_starts[sorted_e] + jnp.arange(a, dtype=jnp.int32) - starts[sorted_e]
    nb = -(-a // tm) + e
    tok_of_slot = jnp.full((nb * tm,), t, jnp.int32).at[slot].set(order // TOP_K)
    n_used = (pad_ends[-1] // tm).astype(jnp.int32)
    blk = jnp.minimum(jnp.arange(nb, dtype=jnp.int32), n_used - 1) * tm
    blk_expert = jnp.minimum(jnp.searchsorted(pad_ends, blk, side='right'), e - 1).astype(jnp.int32)
    xs = jnp.concatenate([h, jnp.zeros((1, d), h.dtype)], axis=0)[tok_of_slot]
    yb = _moe_experts(xs, blk_expert, n_used.reshape(1), w_gate, w_up, w_down, layer, tm=tm)
    slot_of = jnp.zeros((a,), jnp.int32).at[order].set(slot).reshape(t, TOP_K)
    return [yb[slot_of[:, k]] for k in range(TOP_K)], [gates[:, k:k + 1] for k in range(TOP_K)]


def kernel(x_prompt, x_sample, c_prompt, c_sample, state_conv, state_ssm, cache_k, cache_v, w_ada, b_ada, ln_g, ln_b, ssm_w_in, ssm_conv_w, ssm_conv_b, ssm_dt_bias, ssm_a_log, ssm_d, ssm_norm_g, ssm_w_out, sb_w_k, sb_w_v, sb_w_q, sb_w_o, router_w, router_b, moe_w_gate, moe_w_up, moe_w_down):
    bp, lp, d = x_prompt.shape
    bs, ls_, _ = x_sample.shape
    assert ls_ == ROWS and lp % ROWS == 0 and w_ada.shape[0] == DEPTH == 2
    n_heads, p, n_state = state_ssm.shape[2:]
    inner = n_heads * p
    conv_dim = ssm_conv_w.shape[-1]
    kw = ssm_conv_w.shape[1]
    n_groups = (conv_dim - inner) // (2 * n_state)
    sb_heads, dh = cache_k.shape[2:]
    past_len = cache_k.shape[1]
    tp, ts = bp * lp, bs * ls_
    t = tp + ts

    x = jnp.concatenate([x_prompt.reshape(tp, d), x_sample.reshape(ts, d)], axis=0)
    c_all = jnp.concatenate([c_prompt, c_sample], axis=0)
    gmap = jnp.concatenate([jnp.repeat(jnp.arange(bp), lp // ROWS), bp + jnp.arange(bs)])
    mod = _ada(c_all, w_ada, b_ada)
    modg = [mod[l][gmap][:, None, :] for l in range(DEPTH)]

    h = _modulate(x, modg[0], 1, 0)
    w_in = ssm_w_in[0]
    n_zx = inner + conv_dim
    zx = _matmul(h, w_in, n_out=n_zx, name="in_proj")
    dt_raw = _matmul(h, w_in, n_off=n_zx // 128, n_out=n_heads, tn=128, name="in_proj_dt")
    hist8 = jnp.pad(state_conv[0], ((0, 0), (8 - (kw - 1), 0), (0, 0)))
    act_p = _conv_silu(zx, inner, ssm_conv_w[0], ssm_conv_b[0], row0=0, n_seq=bp, seq_len=lp, hist8=None)
    act_s = _conv_silu(zx, inner, ssm_conv_w[0], ssm_conv_b[0], row0=tp, n_seq=bs, seq_len=ls_, hist8=hist8)
    ssd_kw = dict(n_heads=n_heads, p=p, n_state=n_state, n_groups=n_groups)
    y_p, hfin_p = _ssd(act_p, zx, dt_raw, ssm_dt_bias[0], ssm_a_log[0], ssm_d[0], ssm_norm_g[0], None,
                       row0=0, act_row0=0, n_seq=bp, seq_len=lp, lc=SSD_CHUNK, **ssd_kw)
    y_s, hfin_s = _ssd(act_s, zx, dt_raw, ssm_dt_bias[0], ssm_a_log[0], ssm_d[0], ssm_norm_g[0],
                       state_ssm[0].reshape(bs, inner, n_state),
                       row0=tp, act_row0=0, n_seq=bs, seq_len=ls_, lc=min(SSD_CHUNK, ls_), **ssd_kw)
    y = jnp.concatenate([y_p, y_s], axis=0)
    kh = inner // 2
    m = _matmul(y, ssm_w_out[0], kblk=kh, x_koff=0, w_koff=0, name="out_proj0")
    m = _matmul(y, ssm_w_out[0], kblk=kh, x_koff=1, w_koff=1, acc=m, name="out_proj1")

    xbc = zx[:, inner:]
    conv_p = xbc[:tp].reshape(bp, lp, conv_dim)[:, lp - (kw - 1):][None]
    conv_s = xbc[tp:].reshape(bs, ls_, conv_dim)[:, ls_ - (kw - 1):][None]
    ssm_p = hfin_p.reshape(1, bp, n_heads, p, n_state)
    ssm_s = hfin_s.reshape(1, bs, n_heads, p, n_state)

    x, h, s = _ln(x, [m], (modg[0], 2), ln_g[0, 0], ln_b[0, 0], next_mod=(modg[0], 4, 3), router_w=router_w)
    ys, gs = _moe(h, s, router_b, moe_w_gate, moe_w_up, moe_w_down, 0)
    x, h, xb = _ln(x, ys, (modg[0], 5), ln_g[0, 1], ln_b[0, 1], m_gates=gs, next_mod=(modg[1], 1, 0), with_xb=True)

    k_rows = _matmul(xb, sb_w_k, name="k_proj")
    v_rows = _matmul(xb, sb_w_v, name="v_proj")

    q = _matmul(h, sb_w_q, lead=(0,), out_dtype=BF16, name="q_proj")
    att_p = _sb_prompt(q, k_rows, v_rows, n_seq=bp, seq_len=lp, dh=dh)
    att_s = _sb_sample(q, k_rows, v_rows, cache_k.reshape(bs, past_len, d), cache_v.reshape(bs, past_len, d),
                       row0=tp, n_seq=bs, seq_len=ls_, dh=dh)
    att = jnp.concatenate([att_p, att_s], axis=0)
    m = _matmul(att, sb_w_o, lead=(0,), name="o_proj")

    x, h, s = _ln(x, [m], (modg[1], 2), ln_g[1, 0], ln_b[1, 0], next_mod=(modg[1], 4, 3), router_w=router_w)
    ys, gs = _moe(h, s, router_b, moe_w_gate, moe_w_up, moe_w_down, 1)
    (x,) = _ln(x, ys, (modg[1], 5), ln_g[1, 1], ln_b[1, 1], m_gates=gs)

    def heads(a, nb_, l):
        return a.reshape(nb_, l, sb_heads, dh)

    return (x[:tp].reshape(bp, lp, d), x[tp:].reshape(bs, ls_, d), conv_p, ssm_p,
            heads(k_rows[:tp], bp, lp), heads(v_rows[:tp], bp, lp), conv_s, ssm_s,
            heads(k_rows[tp:], bs, ls_), heads(v_rows[tp:], bs, ls_))
```

```python
import functools

import jax
import jax.numpy as jnp
from jax import lax
from jax.experimental import pallas as pl
from jax.experimental.pallas import tpu as pltpu

F32 = jnp.float32
BF16 = jnp.bfloat16
HIGHEST = lax.Precision.HIGHEST

DEPTH = 2
N_EXPERT_GROUPS = 4
TOP_K = 2
DN_ALPHA = (2 * DEPTH) ** 0.25
LN_EPS = 1e-5
RMS_EPS = 1e-5
SSD_CHUNK = 128

ROWS = 32
VMEM_LIMIT = 52 * 1024 * 1024
EXP_ZERO_BELOW = -105.0


def _params(*sem):
    return pltpu.CompilerParams(dimension_semantics=sem, vmem_limit_bytes=VMEM_LIMIT)


def _split3(x):
    hi = x.astype(BF16)
    r1 = x - hi.astype(F32)
    mid = r1.astype(BF16)
    lo = (r1 - mid.astype(F32)).astype(BF16)
    return hi, mid, lo


def _dot_f32_01(x, m01):
    hi, mid, lo = _split3(x)
    d = functools.partial(jnp.dot, preferred_element_type=F32)
    return (d(lo, m01) + d(mid, m01)) + d(hi, m01)


def _dot_01_f32(m01, x):
    hi, mid, lo = _split3(x)
    d = functools.partial(jnp.dot, preferred_element_type=F32)
    return (d(m01, lo) + d(m01, mid)) + d(m01, hi)


def _pick_tile(n, cap, mult=128):
    best = None
    for c in range(mult, min(n, cap) + 1, mult):
        if n % c == 0:
            best = c
    assert best is not None, (n, cap)
    return best


def _ada_kernel(c_ref, w_ref, b_ref, o_ref):
    c = c_ref[...]
    s = (c * jax.nn.sigmoid(c)).astype(BF16)
    o_ref[...] = jnp.dot(s, w_ref[...].astype(BF16), preferred_element_type=F32) + b_ref[...]


def _ada(c_all, w_ada, b_ada):
    depth, d, n = w_ada.shape
    bc = c_all.shape[0]
    tn = 512
    return pl.pallas_call(
        _ada_kernel,
        grid=(depth, n // tn),
        in_specs=[pl.BlockSpec((bc, d), lambda l, j: (0, 0)),
                  pl.BlockSpec((None, d, tn), lambda l, j: (l, 0, j)),
                  pl.BlockSpec((None, 1, tn), lambda l, j: (l, 0, j))],
        out_specs=pl.BlockSpec((None, bc, tn), lambda l, j: (l, 0, j)),
        out_shape=jax.ShapeDtypeStruct((depth, bc, n), F32),
        compiler_params=_params("arbitrary", "arbitrary"),
        name="ada",
    )(c_all, w_ada, b_ada.reshape(depth, 1, n))


def _mod_kernel(x_ref, sc_ref, sh_ref, h_ref):
    gb = sc_ref.shape[0]
    rows, d = x_ref.shape
    x = x_ref[...].reshape(gb, rows // gb, d)
    h = x * (1.0 + sc_ref[...]) + sh_ref[...]
    h_ref[...] = h.reshape(rows, d).astype(h_ref.dtype)


def _modulate(x, modg, scale_blk, shift_blk, gb=8):
    t, d = x.shape
    g = modg.shape[0]
    return pl.pallas_call(
        _mod_kernel,
        grid=(g // gb,),
        in_specs=[pl.BlockSpec((gb * ROWS, d), lambda i: (i, 0)),
                  pl.BlockSpec((gb, 1, d), lambda i: (i, 0, scale_blk)),
                  pl.BlockSpec((gb, 1, d), lambda i: (i, 0, shift_blk))],
        out_specs=pl.BlockSpec((gb * ROWS, d), lambda i: (i, 0)),
        out_shape=jax.ShapeDtypeStruct((t, d), BF16),
        compiler_params=_params("arbitrary"),
        name="modulate",
    )(x, modg, modg)


def _ln_kernel(*refs, n_m, gated_m, with_h, with_router, with_xb):
    it = iter(refs)
    x_ref = next(it)
    m_refs = [next(it) for _ in range(n_m)]
    mg_refs = [next(it) for _ in range(n_m)] if gated_m else []
    gate_ref, g_ref, b_ref = next(it), next(it), next(it)
    sc_ref = sh_ref = rw_ref = None
    if with_h:
        sc_ref, sh_ref = next(it), next(it)
    if with_router:
        rw_ref = next(it)
    xo_ref = next(it)
    h_ref = next(it) if with_h else None
    s_ref = next(it) if with_router else None
    xb_ref = next(it) if with_xb else None

    gb = gate_ref.shape[0]
    rows, d = x_ref.shape
    if gated_m:
        m = m_refs[0][...] * mg_refs[0][...]
        for mr, gr in zip(m_refs[1:], mg_refs[1:]):
            m = m + mr[...] * gr[...]
    else:
        m = m_refs[0][...]
        for mr in m_refs[1:]:
            m = m + mr[...]
    v = DN_ALPHA * x_ref[...] + (gate_ref[...] * m.reshape(gb, rows // gb, d)).reshape(rows, d)
    mu = jnp.mean(v, axis=-1, keepdims=True)
    vc = v - mu
    var = jnp.mean(vc * vc, axis=-1, keepdims=True)
    xn = vc * lax.rsqrt(var + LN_EPS) * g_ref[...] + b_ref[...]
    xo_ref[...] = xn
    if with_xb:
        xb_ref[...] = xn.astype(BF16)
    if with_h:
        h = xn.reshape(gb, rows // gb, d) * (1.0 + sc_ref[...]) + sh_ref[...]
        h = h.reshape(rows, d)
        h_ref[...] = h.astype(BF16)
        if with_router:
            logits = jnp.dot(h, rw_ref[...], preferred_element_type=F32, precision=HIGHEST)
            s_ref[...] = jax.nn.sigmoid(logits)


def _ln(x, ms, gate_src, ln_g, ln_b, *, m_gates=None, next_mod=None, router_w=None, with_xb=False, gb=8):
    t, d = x.shape
    modg, gate_blk = gate_src
    g = modg.shape[0]
    tr = gb * ROWS
    row = pl.BlockSpec((tr, d), lambda i: (i, 0))
    vec = pl.BlockSpec((1, d), lambda i: (0, 0))

    def modspec(blk):
        return pl.BlockSpec((gb, 1, d), lambda i: (i, 0, blk))

    args = [x] + list(ms)
    specs = [row] + [row] * len(ms)
    if m_gates is not None:
        args += list(m_gates)
        specs += [pl.BlockSpec((tr, 1), lambda i: (i, 0))] * len(ms)
    args += [modg, ln_g.reshape(1, d), ln_b.reshape(1, d)]
    specs += [modspec(gate_blk), vec, vec]
    outs = [jax.ShapeDtypeStruct((t, d), F32)]
    out_specs = [row]
    if next_mod is not None:
        nm, sc_blk, sh_blk = next_mod
        args += [nm, nm]
        specs += [modspec(sc_blk), modspec(sh_blk)]
        outs.append(jax.ShapeDtypeStruct((t, d), BF16))
        out_specs.append(row)
    if router_w is not None:
        e = router_w.shape[1]
        args.append(router_w)
        specs.append(pl.BlockSpec((d, e), lambda i: (0, 0)))
        outs.append(jax.ShapeDtypeStruct((t, e), F32))
        out_specs.append(pl.BlockSpec((tr, e), lambda i: (i, 0)))
    if with_xb:
        outs.append(jax.ShapeDtypeStruct((t, d), BF16))
        out_specs.append(row)
    kern = functools.partial(_ln_kernel, n_m=len(ms), gated_m=m_gates is not None,
                             with_h=next_mod is not None, with_router=router_w is not None, with_xb=with_xb)
    return pl.pallas_call(
        kern, grid=(g // gb,), in_specs=specs, out_specs=out_specs, out_shape=outs,
        compiler_params=_params("arbitrary"), name="ln",
    )(*args)


def _mm_kernel(*refs, with_acc):
    if with_acc:
        x_ref, w_ref, a_ref, o_ref, wb_ref = refs
    else:
        x_ref, w_ref, o_ref, wb_ref = refs
        a_ref = None

    @pl.when(pl.program_id(1) == 0)
    def _():
        wb_ref[...] = w_ref[...].astype(BF16)

    r = jnp.dot(x_ref[...], wb_ref[...], preferred_element_type=F32)
    if with_acc:
        r = r + a_ref[...]
    o_ref[...] = r.astype(o_ref.dtype)


def _matmul(x, w, *, lead=(), kblk=None, x_koff=0, w_koff=0, n_off=0, n_out=None, tm=1024, tn=512,
            out_dtype=F32, acc=None, name="matmul"):
    t = x.shape[0]
    kfull, nfull = w.shape[-2:]
    kblk = kfull if kblk is None else kblk
    n_out = nfull if n_out is None else n_out
    tm = _pick_tile(t, tm)
    assert n_out % tn == 0
    nl = len(lead)
    in_specs = [pl.BlockSpec((tm, kblk), lambda j, i: (i, x_koff)),
                pl.BlockSpec((None,) * nl + (kblk, tn), lambda j, i: tuple(lead) + (w_koff, j + n_off))]
    args = [x, w]
    if acc is not None:
        in_specs.append(pl.BlockSpec((tm, tn), lambda j, i: (i, j)))
        args.append(acc)
    return pl.pallas_call(
        functools.partial(_mm_kernel, with_acc=acc is not None),
        grid=(n_out // tn, t // tm),
        in_specs=in_specs,
        out_specs=pl.BlockSpec((tm, tn), lambda j, i: (i, j)),
        out_shape=jax.ShapeDtypeStruct((t, n_out), out_dtype),
        scratch_shapes=[pltpu.VMEM((kblk, tn), BF16)],
        compiler_params=_params("arbitrary", "arbitrary"),
        name=name,
    )(*args)


def _conv_kernel(*refs, with_hist, kw):
    if with_hist:
        x_ref, hist_ref, w_ref, b_ref, o_ref, xp_ref = refs
    else:
        x_ref, w_ref, b_ref, o_ref, xp_ref = refs
    tr = x_ref.shape[0]
    if with_hist:
        xp_ref[0:8, :] = hist_ref[...]
    else:
        @pl.when(pl.program_id(2) == 0)
        def _():
            xp_ref[0:8, :] = jnp.zeros((8, xp_ref.shape[1]), F32)
    xp_ref[8:8 + tr, :] = x_ref[...]
    w = w_ref[...]
    y = b_ref[...] + w[kw - 1:kw, :] * x_ref[...]
    for k in range(1, kw):
        y = y + w[kw - 1 - k:kw - k, :] * xp_ref[8 - k:8 - k + tr, :]
    o_ref[...] = (y * jax.nn.sigmoid(y)).astype(o_ref.dtype)
    if not with_hist:
        xp_ref[0:8, :] = xp_ref[tr:tr + 8, :]


def _conv_silu(zx, col0, conv_w, conv_b, *, row0, n_seq, seq_len, hist8, tc=1024):
    kw, c = conv_w.shape
    tr = min(seq_len, 512)
    nrt = seq_len // tr
    assert seq_len % tr == 0 and c % tc == 0 and col0 % tc == 0 and row0 % tr == 0
    with_hist = hist8 is not None
    assert not with_hist or nrt == 1
    rb0, cb0 = row0 // tr, col0 // tc
    in_specs = [pl.BlockSpec((tr, tc), lambda s, j, r: (rb0 + s * nrt + r, cb0 + j))]
    args = [zx]
    if with_hist:
        in_specs.append(pl.BlockSpec((None, 8, tc), lambda s, j, r: (s, 0, j)))
        args.append(hist8)
    in_specs += [pl.BlockSpec((kw, tc), lambda s, j, r: (0, j)),
                 pl.BlockSpec((1, tc), lambda s, j, r: (0, j))]
    args += [conv_w, conv_b.reshape(1, c)]
    return pl.pallas_call(
        functools.partial(_conv_kernel, with_hist=with_hist, kw=kw),
        grid=(n_seq, c // tc, nrt),
        in_specs=in_specs,
        out_specs=pl.BlockSpec((tr, tc), lambda s, j, r: (s * nrt + r, j)),
        out_shape=jax.ShapeDtypeStruct((n_seq * seq_len, c), BF16),
        scratch_shapes=[pltpu.VMEM((tr + 8, tc), F32)],
        compiler_params=_params("arbitrary", "arbitrary", "arbitrary"),
        name="conv_silu",
    )(*args)


def _ssd_kernel(*refs, with_h0, with_into, hpg, p):
    if with_into:
        refs = refs[1:]
    if with_h0:
        (x_ref, b_ref, c_ref, z_ref, dt_ref, dtb_ref, alog_ref, dsk_ref, ng_ref, h0_ref,
         y_ref, hf_ref, ht_ref, yd_ref) = refs
    else:
        (x_ref, b_ref, c_ref, z_ref, dt_ref, dtb_ref, alog_ref, dsk_ref, ng_ref,
         y_ref, hf_ref, ht_ref, yd_ref) = refs
        h0_ref = None
    g = pl.program_id(1)
    ci = pl.program_id(2)
    lc, gw = x_ref.shape
    nh = dt_ref.shape[1]

    @pl.when(ci == 0)
    def _():
        if with_h0:
            ht_ref[...] = h0_ref[...].T
        else:
            ht_ref[...] = jnp.zeros(ht_ref.shape, F32)

    dt = jax.nn.softplus(dt_ref[...] + dtb_ref[...])
    da = dt * (-jnp.exp(alog_ref[...]))
    r_i = lax.broadcasted_iota(jnp.int32, (lc, lc), 0)
    c_i = lax.broadcasted_iota(jnp.int32, (lc, lc), 1)
    tril = r_i >= c_i
    cs_h = _dot_01_f32(tril.astype(BF16), da)
    head_of_col = g * hpg + lax.broadcasted_iota(jnp.int32, (nh, gw), 1) // p
    expand = (lax.broadcasted_iota(jnp.int32, (nh, gw), 0) == head_of_col).astype(BF16)
    dt_e = _dot_f32_01(dt, expand)
    cs = _dot_f32_01(cs_h, expand)
    cs_last = cs[lc - 1:lc, :]
    eye = r_i == c_i

    xs = x_ref[...].astype(F32)
    xdt = xs * dt_e
    xdt_b = xdt.astype(BF16)
    bm = b_ref[...]
    cm = c_ref[...]
    cb = lax.dot_general(cm, bm, (((1,), (1,)), ((), ())), preferred_element_type=F32)

    for j in range(hpg):
        col = jnp.broadcast_to(cs[:, j * p:j * p + 1], (lc, lc))
        row = jnp.sum(jnp.where(eye, col, 0.0), axis=0, keepdims=True)
        dec = jnp.exp(jnp.where(tril, col - row, -jnp.inf))
        mj = (cb * dec).astype(BF16)
        yd_ref[:, j * p:(j + 1) * p] = jnp.dot(mj, xdt_b[:, j * p:(j + 1) * p], preferred_element_type=F32)

    ht = ht_ref[...]
    y_off = jnp.dot(cm, ht.astype(BF16), preferred_element_type=F32) * jnp.exp(cs)
    xw = (xdt * jnp.exp(cs_last - cs)).astype(BF16)
    st_t = lax.dot_general(bm, xw, (((0,), (0,)), ((), ())), preferred_element_type=F32)
    ht_new = jnp.exp(cs_last) * ht + st_t
    ht_ref[...] = ht_new

    @pl.when(ci == pl.num_programs(2) - 1)
    def _():
        hf_ref[...] = ht_new.T

    y = yd_ref[...] + y_off + dsk_ref[...] * xs
    z = z_ref[...]
    y = y * (z * jax.nn.sigmoid(z))
    ms = jnp.mean(y * y, axis=-1, keepdims=True)
    y_ref[...] = (y * lax.rsqrt(ms + RMS_EPS) * ng_ref[...]).astype(y_ref.dtype)


def _ssd(act, zx, dt_raw, dt_bias, a_log, d_skip, norm_g, h0, *, row0, act_row0, n_seq, seq_len, lc,
         n_heads, p, n_state, n_groups, y_into=None):
    inner = n_heads * p
    hpg = n_heads // n_groups
    gw = hpg * p
    nc = seq_len // lc
    assert seq_len % lc == 0 and row0 % lc == 0 and act_row0 % lc == 0 and gw % 128 == 0 and n_state % 128 == 0
    rb0, ab0 = row0 // lc, act_row0 // lc
    bcol = inner // n_state
    with_h0 = h0 is not None
    d_e = jnp.repeat(d_skip.astype(F32), p).reshape(1, inner)
    in_specs = [
        pl.BlockSpec((lc, gw), lambda s, g, c: (ab0 + s * nc + c, g)),
        pl.BlockSpec((lc, n_state), lambda s, g, c: (ab0 + s * nc + c, bcol + g)),
        pl.BlockSpec((lc, n_state), lambda s, g, c: (ab0 + s * nc + c, bcol + n_groups + g)),
        pl.BlockSpec((lc, gw), lambda s, g, c: (rb0 + s * nc + c, g)),
        pl.BlockSpec((lc, n_heads), lambda s, g, c: (rb0 + s * nc + c, 0)),
        pl.BlockSpec((1, n_heads), lambda s, g, c: (0, 0)),
        pl.BlockSpec((1, n_heads), lambda s, g, c: (0, 0)),
        pl.BlockSpec((1, gw), lambda s, g, c: (0, g)),
        pl.BlockSpec((1, gw), lambda s, g, c: (0, g)),
    ]
    args = [act, act, act, zx, dt_raw, dt_bias.reshape(1, n_heads).astype(F32),
            a_log.reshape(1, n_heads).astype(F32), d_e, norm_g.reshape(1, inner).astype(F32)]
    if with_h0:
        in_specs.append(pl.BlockSpec((None, gw, n_state), lambda s, g, c: (s, g, 0)))
        args.append(h0)
    aliases = {}
    if y_into is not None:
        in_specs.insert(0, pl.BlockSpec(memory_space=pl.ANY))
        args.insert(0, y_into)
        aliases = {0: 0}
    return pl.pallas_call(
        functools.partial(_ssd_kernel, with_h0=with_h0, with_into=y_into is not None, hpg=hpg, p=p),
        grid=(n_seq, n_groups, nc),
        in_specs=in_specs,
        out_specs=[pl.BlockSpec((lc, gw), lambda s, g, c: (rb0 + s * nc + c, g)),
                   pl.BlockSpec((None, gw, n_state), lambda s, g, c: (s, g, 0))],
        out_shape=[jax.ShapeDtypeStruct((zx.shape[0], inner), BF16),
                   jax.ShapeDtypeStruct((n_seq, inner, n_state), F32)],
        scratch_shapes=[pltpu.VMEM((n_state, gw), F32), pltpu.VMEM((lc, gw), F32)],
        input_output_aliases=aliases,
        compiler_params=_params("arbitrary", "arbitrary", "arbitrary"),
        name="ssd",
    )(*args)


def _sb_block(q, k, v, r_run, acc, scale, diag):
    tq, tk = q.shape[0], k.shape[0]
    z = lax.dot_general(q, k, (((1,), (1,)), ((), ())), preferred_element_type=F32) * scale
    ls = jnp.minimum(z, 0.0) - jnp.log(1.0 + jnp.exp(-jnp.abs(z)))
    l1 = ls - z
    if diag:
        mask = lax.broadcasted_iota(jnp.int32, (tq, tk), 1) < lax.broadcasted_iota(jnp.int32, (tq, tk), 0)
        l1 = jnp.where(mask, l1, 0.0)
    later = (lax.broadcasted_iota(jnp.int32, (tk, tk), 0) > lax.broadcasted_iota(jnp.int32, (tk, tk), 1))
    after = _dot_f32_01(l1, later.astype(BF16))
    w = jnp.exp(ls + after + r_run)
    if diag:
        w = jnp.where(mask, w, 0.0)
    acc = acc + jnp.dot(w.astype(BF16), v, preferred_element_type=F32)
    r_run = r_run + jnp.sum(l1, axis=1, keepdims=True)
    return r_run, acc


def _sb_kernel(q_ref, k_ref, v_ref, o_ref, *, hb, dh, scale):
    tq = tk = q_ref.shape[0]

    def head(ref, rows, h):
        return ref[rows, h * dh:(h + 1) * dh]

    qi = pl.program_id(2)
    own = pl.ds(pl.multiple_of(qi * tq, tq), tq)

    state = []
    for h in range(hb):
        q = head(q_ref, slice(None), h)
        r_run, acc = _sb_block(q, head(k_ref, own, h).astype(BF16), head(v_ref, own, h).astype(BF16),
                               jnp.zeros((tq, 1), F32), jnp.zeros((tq, dh), F32), scale, True)
        state += [r_run, acc]
    rmax = functools.reduce(jnp.maximum, [jnp.max(state[2 * h]) for h in range(hb)])

    def cond(carry):
        kb, rmax = carry[0], carry[1]
        return jnp.logical_and(kb >= 0, rmax > EXP_ZERO_BELOW)

    def body(carry):
        kb, st = carry[0], list(carry[2:])
        rows = pl.ds(pl.multiple_of(kb * tk, tk), tk)
        for h in range(hb):
            q = head(q_ref, slice(None), h)
            st[2 * h], st[2 * h + 1] = _sb_block(q, head(k_ref, rows, h).astype(BF16),
                                                 head(v_ref, rows, h).astype(BF16),
                                                 st[2 * h], st[2 * h + 1], scale, False)
        rmax = functools.reduce(jnp.maximum, [jnp.max(st[2 * h]) for h in range(hb)])
        return (kb - 1, rmax, *st)

    out = lax.while_loop(cond, body, (qi - 1, rmax, *state))
    for h in range(hb):
        o_ref[:, h * dh:(h + 1) * dh] = out[3 + 2 * h].astype(o_ref.dtype)


def _sb_past_kernel(into_ref, q_ref, k_ref, v_ref, pk_ref, pv_ref, o_ref, rm_ref, r_ref, acc_ref, *, hb, dh, scale):
    pb = pl.program_id(2)
    tq = q_ref.shape[0]

    def cols(h):
        return slice(h * dh, (h + 1) * dh)

    @pl.when(pb == 0)
    def _():
        for h in range(hb):
            r_run, acc = _sb_block(q_ref[:, cols(h)], k_ref[:, cols(h)].astype(BF16), v_ref[:, cols(h)].astype(BF16),
                                   jnp.zeros((tq, 1), F32), jnp.zeros((tq, dh), F32), scale, True)
            r_ref[h] = jnp.broadcast_to(r_run, (tq, 128))
            acc_ref[:, cols(h)] = acc

    rmax = jnp.max(r_ref[...])

    @pl.when(rmax > EXP_ZERO_BELOW)
    def _():
        for h in range(hb):
            r_run, acc = _sb_block(q_ref[:, cols(h)], pk_ref[:, h, :].astype(BF16), pv_ref[:, h, :].astype(BF16),
                                   r_ref[h][:, 0:1], acc_ref[:, cols(h)], scale, False)
            r_ref[h] = jnp.broadcast_to(r_run, (tq, 128))
            acc_ref[:, cols(h)] = acc

    @pl.when(pb == pl.num_programs(2) - 1)
    def _():
        o_ref[...] = acc_ref[...].astype(o_ref.dtype)
        rm_ref[...] = jnp.broadcast_to(jnp.max(r_ref[...]), rm_ref.shape)


def _sb_prompt(q, k, v, *, n_seq, seq_len, dh, tq=256, hb=4):
    d = q.shape[1]
    nq = seq_len // tq
    assert seq_len % tq == 0 and d % (hb * dh) == 0
    hw = hb * dh
    kv = pl.BlockSpec((seq_len, hw), lambda b, h, i: (b, h))
    return pl.pallas_call(
        functools.partial(_sb_kernel, hb=hb, dh=dh, scale=dh ** -0.5),
        grid=(n_seq, d // hw, nq),
        in_specs=[pl.BlockSpec((tq, hw), lambda b, h, i: (b * nq + i, h)), kv, kv],
        out_specs=pl.BlockSpec((tq, hw), lambda b, h, i: (b * nq + i, h)),
        out_shape=jax.ShapeDtypeStruct(q.shape, BF16),
        compiler_params=_params("arbitrary", "arbitrary", "arbitrary"),
        name="sb_prompt",
    )(q, k, v)


def _sb_past(att_into, q, k, v, past_k, past_v, *, row0, seq_len, n_blocks, tk, hb=8):
    d = q.shape[1]
    n_seq, past_len, n_h, dh = past_k.shape
    hb = min(hb, n_h)
    assert past_len % tk == 0 and row0 % seq_len == 0 and n_h % hb == 0
    hw = hb * dh
    ng = n_h // hb
    rb0 = row0 // seq_len
    last = past_len // tk - 1
    new = pl.BlockSpec((seq_len, hw), lambda s, h, b: (rb0 + s, h))
    past = pl.BlockSpec((None, tk, hb, dh), lambda s, h, b: (s, last - b, h, 0))
    return pl.pallas_call(
        functools.partial(_sb_past_kernel, hb=hb, dh=dh, scale=dh ** -0.5),
        grid=(n_seq, ng, n_blocks),
        in_specs=[pl.BlockSpec(memory_space=pl.ANY), new, new, new, past, past],
        out_specs=[new, pl.BlockSpec((None, 8, 128), lambda s, h, b: (s * ng + h, 0, 0))],
        out_shape=[jax.ShapeDtypeStruct(q.shape, BF16), jax.ShapeDtypeStruct((n_seq * ng, 8, 128), F32)],
        scratch_shapes=[pltpu.VMEM((hb, seq_len, 128), F32), pltpu.VMEM((seq_len, hw), F32)],
        input_output_aliases={0: 0},
        compiler_params=_params("arbitrary", "arbitrary", "arbitrary"),
        name="sb_past",
    )(att_into, q, k, v, past_k, past_v)


def _sb_sample(att_into, q, k, v, past_k, past_v, *, row0, seq_len, tk=256):
    past_len = past_k.shape[1]
    tk = min(tk, past_len)
    nblk = past_len // tk
    kw = dict(row0=row0, seq_len=seq_len, tk=tk)
    att, rm = _sb_past(att_into, q, k, v, past_k, past_v, n_blocks=1, **kw)
    if nblk == 1:
        return att
    return lax.cond(jnp.max(rm) > EXP_ZERO_BELOW,
                    lambda a: _sb_past(a, q, k, v, past_k, past_v, n_blocks=nblk, **kw)[0],
                    lambda a: a, att)


def _moe_up_kernel(be_ref, nbu_ref, x_ref, wg_ref, wu_ref, o_ref, wgb_ref, wub_ref):
    b = pl.program_id(1)
    changed = jnp.logical_or(b == 0, be_ref[b] != be_ref[jnp.maximum(b - 1, 0)])

    @pl.when(changed)
    def _():
        wgb_ref[...] = wg_ref[...].astype(BF16)
        wub_ref[...] = wu_ref[...].astype(BF16)

    @pl.when(b < nbu_ref[0])
    def _():
        x = x_ref[...]
        gt = jnp.dot(x, wgb_ref[...], preferred_element_type=F32)
        up = jnp.dot(x, wub_ref[...], preferred_element_type=F32)
        o_ref[...] = (gt * jax.nn.sigmoid(gt) * up).astype(o_ref.dtype)


def _moe_down_kernel(be_ref, nbu_ref, h_ref, wd_ref, o_ref, wdb_ref):
    b = pl.program_id(1)
    changed = jnp.logical_or(b == 0, be_ref[b] != be_ref[jnp.maximum(b - 1, 0)])

    @pl.when(changed)
    def _():
        wdb_ref[...] = wd_ref[...].astype(BF16)

    @pl.when(b < nbu_ref[0])
    def _():
        o_ref[...] = jnp.dot(h_ref[...], wdb_ref[...], preferred_element_type=F32)


def _moe_experts(xs, blk_expert, n_used, w_gate, w_up, w_down, layer, *, tm, tn_up=512, tn_down=1024):
    pr, d = xs.shape
    de = w_gate.shape[-1]
    nb = pr // tm
    tn_up, tn_down = min(tn_up, de), min(tn_down, d)

    def xrow(c, b, be, nbu):
        return (jnp.minimum(b, nbu[0] - 1), 0)

    hid = pl.pallas_call(
        _moe_up_kernel,
        grid_spec=pltpu.PrefetchScalarGridSpec(
            num_scalar_prefetch=2, grid=(de // tn_up, nb),
            in_specs=[pl.BlockSpec((tm, d), xrow),
                      pl.BlockSpec((None, None, d, tn_up), lambda c, b, be, nbu: (layer, be[b], 0, c)),
                      pl.BlockSpec((None, None, d, tn_up), lambda c, b, be, nbu: (layer, be[b], 0, c))],
            out_specs=pl.BlockSpec((tm, tn_up), lambda c, b, be, nbu: (b, c)),
            scratch_shapes=[pltpu.VMEM((d, tn_up), BF16), pltpu.VMEM((d, tn_up), BF16)]),
        out_shape=jax.ShapeDtypeStruct((pr, de), BF16),
        compiler_params=_params("arbitrary", "arbitrary"),
        name="moe_up",
    )(blk_expert, n_used, xs, w_gate, w_up)
    return pl.pallas_call(
        _moe_down_kernel,
        grid_spec=pltpu.PrefetchScalarGridSpec(
            num_scalar_prefetch=2, grid=(d // tn_down, nb),
            in_specs=[pl.BlockSpec((tm, de), xrow),
                      pl.BlockSpec((None, None, de, tn_down), lambda c, b, be, nbu: (layer, be[b], 0, c))],
            out_specs=pl.BlockSpec((tm, tn_down), lambda c, b, be, nbu: (b, c)),
            scratch_shapes=[pltpu.VMEM((de, tn_down), BF16)]),
        out_shape=jax.ShapeDtypeStruct((pr, d), F32),
        compiler_params=_params("arbitrary", "arbitrary"),
        name="moe_down",
    )(blk_expert, n_used, hid, w_down)


def _route_kernel(s_ref, rb_ref, idx_ref, gate_ref, rank_ref, cnt_ref, carry_ref, *, n_groups):
    @pl.when(pl.program_id(0) == 0)
    def _():
        carry_ref[...] = jnp.zeros(carry_ref.shape, F32)

    s = s_ref[...]
    tr, e = s.shape
    per = e // n_groups
    sel = s + rb_ref[...]
    lane_i = lax.broadcasted_iota(jnp.int32, (tr, e), 1)
    lane = lane_i.astype(F32)
    grp = (lane_i // per).astype(F32)
    neg = -jnp.inf

    def top1(v):
        m = jnp.max(v, axis=1, keepdims=True)
        return m, jnp.min(jnp.where(v == m, lane, float(e)), axis=1, keepdims=True)

    best = gi = None
    for g in range(n_groups):
        vg = jnp.where(grp == float(g), sel, neg)
        m1, i1 = top1(vg)
        m2, _ = top1(jnp.where(lane == i1, neg, vg))
        score = m1 + m2
        if best is None:
            best, gi = score, jnp.zeros_like(score)
        else:
            better = score > best
            gi = jnp.where(better, float(g), gi)
            best = jnp.where(better, score, best)
    vc = jnp.where(grp == gi, sel, neg)
    _, e1 = top1(vc)
    _, e2 = top1(jnp.where(lane == e1, neg, vc))
    oh1, oh2 = lane == e1, lane == e2
    g1 = jnp.sum(jnp.where(oh1, s, 0.0), axis=1, keepdims=True)
    g2 = jnp.sum(jnp.where(oh2, s, 0.0), axis=1, keepdims=True)
    den = g1 + g2

    oh = jnp.logical_or(oh1, oh2)
    earlier = (lax.broadcasted_iota(jnp.int32, (tr, tr), 1) < lax.broadcasted_iota(jnp.int32, (tr, tr), 0))
    before = jnp.dot(earlier.astype(BF16), oh.astype(BF16), preferred_element_type=F32) + carry_ref[...]
    r1 = jnp.sum(jnp.where(oh1, before, 0.0), axis=1, keepdims=True)
    r2 = jnp.sum(jnp.where(oh2, before, 0.0), axis=1, keepdims=True)
    carry = carry_ref[...] + jnp.sum(oh.astype(F32), axis=0, keepdims=True)
    carry_ref[...] = carry
    cnt_ref[...] = carry.astype(jnp.int32)

    first = lax.broadcasted_iota(jnp.int32, (tr, TOP_K), 1) == 0
    idx_ref[...] = jnp.where(first, e1, e2).astype(jnp.int32)
    gate_ref[...] = jnp.where(first, g1, g2) / den
    rank_ref[...] = jnp.where(first, r1, r2).astype(jnp.int32)


def _route(s, router_b, tr=512):
    t, e = s.shape
    tr = _pick_tile(t, tr)
    col = pl.BlockSpec((tr, TOP_K), lambda i: (i, 0))
    return pl.pallas_call(
        functools.partial(_route_kernel, n_groups=N_EXPERT_GROUPS),
        grid=(t // tr,),
        in_specs=[pl.BlockSpec((tr, e), lambda i: (i, 0)), pl.BlockSpec((1, e), lambda i: (0, 0))],
        out_specs=[col, col, col, pl.BlockSpec((1, e), lambda i: (0, 0))],
        out_shape=[jax.ShapeDtypeStruct((t, TOP_K), jnp.int32), jax.ShapeDtypeStruct((t, TOP_K), F32),
                   jax.ShapeDtypeStruct((t, TOP_K), jnp.int32), jax.ShapeDtypeStruct((1, e), jnp.int32)],
        scratch_shapes=[pltpu.VMEM((1, e), F32)],
        compiler_params=_params("arbitrary"),
        name="route",
    )(s, router_b.reshape(1, e).astype(F32))


def _moe(h, s, router_b, w_gate, w_up, w_down, layer, *, tm=256):
    t, d = h.shape
    e = s.shape[1]
    idx, gates, rank, counts = _route(s, router_b)
    counts = counts.reshape(e)
    padded = (counts + tm - 1) // tm * tm
    pad_ends = jnp.cumsum(padded)
    pad_starts = pad_ends - padded
    slot = pad_starts[idx] + rank
    nb = -(-(t * TOP_K) // tm) + e
    tok = jnp.broadcast_to(jnp.arange(t, dtype=jnp.int32)[:, None], (t, TOP_K))
    tok_of_slot = jnp.full((nb * tm,), t, jnp.int32).at[slot.reshape(-1)].set(tok.reshape(-1))
    n_used = (pad_ends[-1] // tm).astype(jnp.int32)
    blk = jnp.minimum(jnp.arange(nb, dtype=jnp.int32), n_used - 1) * tm
    blk_expert = jnp.minimum(jnp.searchsorted(pad_ends, blk, side='right'), e - 1).astype(jnp.int32)
    xs = jnp.concatenate([h, jnp.zeros((1, d), h.dtype)], axis=0)[tok_of_slot]
    yb = _moe_experts(xs, blk_expert, n_used.reshape(1), w_gate, w_up, w_down, layer, tm=tm)
    return [yb[slot[:, k]] for k in range(TOP_K)], [gates[:, k:k + 1] for k in range(TOP_K)]


def kernel(x_prompt, x_sample, c_prompt, c_sample, state_conv, state_ssm, cache_k, cache_v, w_ada, b_ada, ln_g, ln_b, ssm_w_in, ssm_conv_w, ssm_conv_b, ssm_dt_bias, ssm_a_log, ssm_d, ssm_norm_g, ssm_w_out, sb_w_k, sb_w_v, sb_w_q, sb_w_o, router_w, router_b, moe_w_gate, moe_w_up, moe_w_down):
    bp, lp, d = x_prompt.shape
    bs, ls_, _ = x_sample.shape
    assert ls_ == ROWS and lp % ROWS == 0 and w_ada.shape[0] == DEPTH == 2
    n_heads, p, n_state = state_ssm.shape[2:]
    inner = n_heads * p
    conv_dim = ssm_conv_w.shape[-1]
    kw = ssm_conv_w.shape[1]
    n_groups = (conv_dim - inner) // (2 * n_state)
    sb_heads, dh = cache_k.shape[2:]
    past_len = cache_k.shape[1]
    tp, ts = bp * lp, bs * ls_
    t = tp + ts

    x = jnp.concatenate([x_prompt.reshape(tp, d), x_sample.reshape(ts, d)], axis=0)
    c_all = jnp.concatenate([c_prompt, c_sample], axis=0)
    gmap = jnp.concatenate([jnp.repeat(jnp.arange(bp), lp // ROWS), bp + jnp.arange(bs)])
    mod = _ada(c_all, w_ada, b_ada)
    modg = [mod[l][gmap][:, None, :] for l in range(DEPTH)]

    h = _modulate(x, modg[0], 1, 0)
    w_in = ssm_w_in[0]
    n_zx = inner + conv_dim
    zx = _matmul(h, w_in, n_out=n_zx, name="in_proj")
    dt_raw = _matmul(h, w_in, n_off=n_zx // 128, n_out=n_heads, tn=128, name="in_proj_dt")
    hist8 = jnp.pad(state_conv[0], ((0, 0), (8 - (kw - 1), 0), (0, 0)))
    act_p = _conv_silu(zx, inner, ssm_conv_w[0], ssm_conv_b[0], row0=0, n_seq=bp, seq_len=lp, hist8=None)
    act_s = _conv_silu(zx, inner, ssm_conv_w[0], ssm_conv_b[0], row0=tp, n_seq=bs, seq_len=ls_, hist8=hist8)
    ssd_kw = dict(n_heads=n_heads, p=p, n_state=n_state, n_groups=n_groups)
    y, hfin_p = _ssd(act_p, zx, dt_raw, ssm_dt_bias[0], ssm_a_log[0], ssm_d[0], ssm_norm_g[0], None,
                     row0=0, act_row0=0, n_seq=bp, seq_len=lp, lc=min(SSD_CHUNK, lp), **ssd_kw)
    y, hfin_s = _ssd(act_s, zx, dt_raw, ssm_dt_bias[0], ssm_a_log[0], ssm_d[0], ssm_norm_g[0],
                     state_ssm[0].reshape(bs, inner, n_state),
                     row0=tp, act_row0=0, n_seq=bs, seq_len=ls_, lc=min(SSD_CHUNK, ls_), y_into=y, **ssd_kw)
    kh = inner // 2
    m = _matmul(y, ssm_w_out[0], kblk=kh, x_koff=0, w_koff=0, name="out_proj0")
    m = _matmul(y, ssm_w_out[0], kblk=kh, x_koff=1, w_koff=1, acc=m, name="out_proj1")

    conv_p = zx[:tp].reshape(bp, lp, n_zx)[:, lp - (kw - 1):, inner:][None]
    conv_s = zx[tp:].reshape(bs, ls_, n_zx)[:, ls_ - (kw - 1):, inner:][None]
    ssm_p = hfin_p.reshape(1, bp, n_heads, p, n_state)
    ssm_s = hfin_s.reshape(1, bs, n_heads, p, n_state)

    x, h, s = _ln(x, [m], (modg[0], 2), ln_g[0, 0], ln_b[0, 0], next_mod=(modg[0], 4, 3), router_w=router_w)
    ys, gs = _moe(h, s, router_b, moe_w_gate, moe_w_up, moe_w_down, 0)
    x, h, xb = _ln(x, ys, (modg[0], 5), ln_g[0, 1], ln_b[0, 1], m_gates=gs, next_mod=(modg[1], 1, 0), with_xb=True)

    k_rows = _matmul(xb, sb_w_k, name="k_proj")
    v_rows = _matmul(xb, sb_w_v, name="v_proj")

    q = _matmul(h, sb_w_q, lead=(0,), out_dtype=BF16, name="q_proj")
    att = _sb_prompt(q, k_rows, v_rows, n_seq=bp, seq_len=lp, dh=dh)
    att = _sb_sample(att, q, k_rows, v_rows, cache_k, cache_v, row0=tp, seq_len=ls_)
    m = _matmul(att, sb_w_o, lead=(0,), name="o_proj")

    x, h, s = _ln(x, [m], (modg[1], 2), ln_g[1, 0], ln_b[1, 0], next_mod=(modg[1], 4, 3), router_w=router_w)
    ys, gs = _moe(h, s, router_b, moe_w_gate, moe_w_up, moe_w_down, 1)
    (x,) = _ln(x, ys, (modg[1], 5), ln_g[1, 1], ln_b[1, 1], m_gates=gs)

    def heads(a, nb_, l):
        return a.reshape(nb_, l, sb_heads, dh)

    return (x[:tp].reshape(bp, lp, d), x[tp:].reshape(bs, ls_, d), conv_p, ssm_p,
            heads(k_rows[:tp], bp, lp), heads(v_rows[:tp], bp, lp), conv_s, ssm_s,
            heads(k_rows[tp:], bs, ls_), heads(v_rows[tp:], bs, ls_))
```

```python
import functools
import math

import jax
import jax.numpy as jnp
from jax import lax
from jax.experimental import pallas as pl
from jax.experimental.pallas import tpu as pltpu

F32 = jnp.float32
BF16 = jnp.bfloat16
HIGHEST = lax.Precision.HIGHEST

DEPTH = 2
N_EXPERT_GROUPS = 4
TOP_K = 2
DN_ALPHA = (2 * DEPTH) ** 0.25
LN_EPS = 1e-5
RMS_EPS = 1e-5
SSD_CHUNK = 128

ROWS = 32
VMEM_LIMIT = 52 * 1024 * 1024
EXP_ZERO_BELOW = -105.0


def _params(*sem):
    return pltpu.CompilerParams(dimension_semantics=sem, vmem_limit_bytes=VMEM_LIMIT)


def _split3(x):
    hi = x.astype(BF16)
    r1 = x - hi.astype(F32)
    mid = r1.astype(BF16)
    lo = (r1 - mid.astype(F32)).astype(BF16)
    return hi, mid, lo


def _dot_f32_01(x, m01):
    hi, mid, lo = _split3(x)
    d = functools.partial(jnp.dot, preferred_element_type=F32)
    return (d(lo, m01) + d(mid, m01)) + d(hi, m01)


def _dot_01_f32(m01, x):
    hi, mid, lo = _split3(x)
    d = functools.partial(jnp.dot, preferred_element_type=F32)
    return (d(m01, lo) + d(m01, mid)) + d(m01, hi)


def _pick_tile(n, cap, mult=128):
    best = None
    for c in range(mult, min(n, cap) + 1, mult):
        if n % c == 0:
            best = c
    assert best is not None, (n, cap)
    return best


def _ada_kernel(c_ref, w_ref, b_ref, o_ref):
    c = c_ref[...]
    s = (c * jax.nn.sigmoid(c)).astype(BF16)
    o_ref[...] = jnp.dot(s, w_ref[...].astype(BF16), preferred_element_type=F32) + b_ref[...]


def _ada(c_all, w_ada, b_ada):
    depth, d, n = w_ada.shape
    bc = c_all.shape[0]
    tn = 512
    return pl.pallas_call(
        _ada_kernel,
        grid=(depth, n // tn),
        in_specs=[pl.BlockSpec((bc, d), lambda l, j: (0, 0)),
                  pl.BlockSpec((None, d, tn), lambda l, j: (l, 0, j)),
                  pl.BlockSpec((None, 1, tn), lambda l, j: (l, 0, j))],
        out_specs=pl.BlockSpec((None, bc, tn), lambda l, j: (l, 0, j)),
        out_shape=jax.ShapeDtypeStruct((depth, bc, n), F32),
        compiler_params=_params("arbitrary", "arbitrary"),
        name="ada",
    )(c_all, w_ada, b_ada.reshape(depth, 1, n))


def _mod_kernel(x_ref, sc_ref, sh_ref, h_ref):
    gb = sc_ref.shape[0]
    rows, d = x_ref.shape
    x = x_ref[...].reshape(gb, rows // gb, d)
    h = x * (1.0 + sc_ref[...]) + sh_ref[...]
    h_ref[...] = h.reshape(rows, d).astype(h_ref.dtype)


def _modulate(x, modg, scale_blk, shift_blk, gb=8):
    t, d = x.shape
    g = modg.shape[0]
    return pl.pallas_call(
        _mod_kernel,
        grid=(g // gb,),
        in_specs=[pl.BlockSpec((gb * ROWS, d), lambda i: (i, 0)),
                  pl.BlockSpec((gb, 1, d), lambda i: (i, 0, scale_blk)),
                  pl.BlockSpec((gb, 1, d), lambda i: (i, 0, shift_blk))],
        out_specs=pl.BlockSpec((gb * ROWS, d), lambda i: (i, 0)),
        out_shape=jax.ShapeDtypeStruct((t, d), BF16),
        compiler_params=_params("arbitrary"),
        name="modulate",
    )(x, modg, modg)


def _ln_kernel(*refs, n_m, gated_m, with_h, with_router, with_xb, n_first):
    it = iter(refs)
    x_ref = next(it)
    m_refs = [next(it) for _ in range(n_m)]
    mg_refs = [next(it) for _ in range(n_m)] if gated_m else []
    gate_ref, g_ref, b_ref = next(it), next(it), next(it)
    sc_ref = sh_ref = rw_ref = None
    if with_h:
        sc_ref, sh_ref = next(it), next(it)
    if with_router:
        rw_ref = next(it)
    xo_ref = next(it)
    xo2_ref = next(it) if n_first is not None else None
    h_ref = next(it) if with_h else None
    s_ref = next(it) if with_router else None
    xb_ref = next(it) if with_xb else None

    gb = gate_ref.shape[0]
    rows, d = x_ref.shape
    if gated_m:
        m = m_refs[0][...] * mg_refs[0][...]
        for mr, gr in zip(m_refs[1:], mg_refs[1:]):
            m = m + mr[...] * gr[...]
    else:
        m = m_refs[0][...]
        for mr in m_refs[1:]:
            m = m + mr[...]
    v = DN_ALPHA * x_ref[...] + (gate_ref[...] * m.reshape(gb, rows // gb, d)).reshape(rows, d)
    mu = jnp.mean(v, axis=-1, keepdims=True)
    vc = v - mu
    var = jnp.mean(vc * vc, axis=-1, keepdims=True)
    xn = vc * lax.rsqrt(var + LN_EPS) * g_ref[...] + b_ref[...]
    if n_first is None:
        xo_ref[...] = xn
    else:
        @pl.when(pl.program_id(0) < n_first)
        def _():
            xo_ref[...] = xn

        @pl.when(pl.program_id(0) >= n_first)
        def _():
            xo2_ref[...] = xn
    if with_xb:
        xb_ref[...] = xn.astype(BF16)
    if with_h:
        h = xn.reshape(gb, rows // gb, d) * (1.0 + sc_ref[...]) + sh_ref[...]
        h = h.reshape(rows, d)
        h_ref[...] = h.astype(BF16)
        if with_router:
            logits = jnp.dot(h, rw_ref[...], preferred_element_type=F32, precision=HIGHEST)
            s_ref[...] = jax.nn.sigmoid(logits)


def _ln(x, ms, gate_src, ln_g, ln_b, *, m_gates=None, next_mod=None, router_w=None, with_xb=False,
        split_rows=None, gb=8):
    t, d = x.shape
    modg, gate_blk = gate_src
    g = modg.shape[0]
    tr = gb * ROWS
    row = pl.BlockSpec((tr, d), lambda i: (i, 0))
    vec = pl.BlockSpec((1, d), lambda i: (0, 0))

    def modspec(blk):
        return pl.BlockSpec((gb, 1, d), lambda i: (i, 0, blk))

    args = [x] + list(ms)
    specs = [row] + [row] * len(ms)
    if m_gates is not None:
        args += list(m_gates)
        specs += [pl.BlockSpec((tr, 1), lambda i: (i, 0))] * len(ms)
    args += [modg, ln_g.reshape(1, d), ln_b.reshape(1, d)]
    specs += [modspec(gate_blk), vec, vec]
    if split_rows is None:
        n_first = None
        outs = [jax.ShapeDtypeStruct((t, d), F32)]
        out_specs = [row]
    else:
        assert split_rows % tr == 0 and 0 < split_rows < t
        n_first = split_rows // tr
        outs = [jax.ShapeDtypeStruct((split_rows, d), F32), jax.ShapeDtypeStruct((t - split_rows, d), F32)]
        out_specs = [pl.BlockSpec((tr, d), lambda i: (jnp.minimum(i, n_first - 1), 0)),
                     pl.BlockSpec((tr, d), lambda i: (jnp.maximum(i - n_first, 0), 0))]
    if next_mod is not None:
        nm, sc_blk, sh_blk = next_mod
        args += [nm, nm]
        specs += [modspec(sc_blk), modspec(sh_blk)]
        outs.append(jax.ShapeDtypeStruct((t, d), BF16))
        out_specs.append(row)
    if router_w is not None:
        e = router_w.shape[1]
        args.append(router_w)
        specs.append(pl.BlockSpec((d, e), lambda i: (0, 0)))
        outs.append(jax.ShapeDtypeStruct((t, e), F32))
        out_specs.append(pl.BlockSpec((tr, e), lambda i: (i, 0)))
    if with_xb:
        outs.append(jax.ShapeDtypeStruct((t, d), BF16))
        out_specs.append(row)
    kern = functools.partial(_ln_kernel, n_m=len(ms), gated_m=m_gates is not None,
                             with_h=next_mod is not None, with_router=router_w is not None, with_xb=with_xb,
                             n_first=n_first)
    return pl.pallas_call(
        kern, grid=(g // gb,), in_specs=specs, out_specs=out_specs, out_shape=outs,
        compiler_params=_params("arbitrary"), name="ln",
    )(*args)


def _mm_kernel(*refs, with_acc, n_first):
    x_ref, w_ref = refs[:2]
    a_ref = refs[2] if with_acc else None
    o_refs, wb_ref = refs[2 + with_acc:-1], refs[-1]

    @pl.when(pl.program_id(1) == 0)
    def _():
        wb_ref[...] = w_ref[...].astype(BF16)

    r = jnp.dot(x_ref[...], wb_ref[...], preferred_element_type=F32)
    if with_acc:
        r = r + a_ref[...]
    if n_first is None:
        o_refs[0][...] = r.astype(o_refs[0].dtype)
    else:
        @pl.when(pl.program_id(1) < n_first)
        def _():
            o_refs[0][...] = r.astype(o_refs[0].dtype)

        @pl.when(pl.program_id(1) >= n_first)
        def _():
            o_refs[1][...] = r.astype(o_refs[1].dtype)


def _matmul(x, w, *, lead=(), kblk=None, x_koff=0, w_koff=0, n_off=0, n_out=None, tm=1024, tn=512,
            out_dtype=F32, acc=None, split_rows=None, name="matmul"):
    t = x.shape[0]
    kfull, nfull = w.shape[-2:]
    kblk = kfull if kblk is None else kblk
    n_out = nfull if n_out is None else n_out
    tm = _pick_tile(t if split_rows is None else math.gcd(t, split_rows), tm)
    assert n_out % tn == 0
    nl = len(lead)
    in_specs = [pl.BlockSpec((tm, kblk), lambda j, i: (i, x_koff)),
                pl.BlockSpec((None,) * nl + (kblk, tn), lambda j, i: tuple(lead) + (w_koff, j + n_off))]
    args = [x, w]
    if acc is not None:
        in_specs.append(pl.BlockSpec((tm, tn), lambda j, i: (i, j)))
        args.append(acc)
    if split_rows is None:
        n_first = None
        out_specs = pl.BlockSpec((tm, tn), lambda j, i: (i, j))
        out_shape = jax.ShapeDtypeStruct((t, n_out), out_dtype)
    else:
        assert split_rows % tm == 0 and 0 < split_rows < t
        n_first = split_rows // tm
        out_specs = [pl.BlockSpec((tm, tn), lambda j, i: (jnp.minimum(i, n_first - 1), j)),
                     pl.BlockSpec((tm, tn), lambda j, i: (jnp.maximum(i - n_first, 0), j))]
        out_shape = [jax.ShapeDtypeStruct((split_rows, n_out), out_dtype),
                     jax.ShapeDtypeStruct((t - split_rows, n_out), out_dtype)]
    return pl.pallas_call(
        functools.partial(_mm_kernel, with_acc=acc is not None, n_first=n_first),
        grid=(n_out // tn, t // tm),
        in_specs=in_specs,
        out_specs=out_specs,
        out_shape=out_shape,
        scratch_shapes=[pltpu.VMEM((kblk, tn), BF16)],
        compiler_params=_params("arbitrary", "arbitrary"),
        name=name,
    )(*args)


def _conv_kernel(*refs, with_hist, kw):
    if with_hist:
        x_ref, hist_ref, w_ref, b_ref, o_ref, xp_ref = refs
        gs = hist_ref.shape[0]
    else:
        x_ref, w_ref, b_ref, o_ref, xp_ref = refs
        gs = 1
    tr = x_ref.shape[0] // gs
    w = w_ref[...]
    if not with_hist:
        @pl.when(pl.program_id(2) == 0)
        def _():
            xp_ref[0:8, :] = jnp.zeros((8, xp_ref.shape[1]), F32)
    for s in range(gs):
        rows = slice(s * tr, (s + 1) * tr)
        if with_hist:
            xp_ref[0:8, :] = hist_ref[s]
        xp_ref[8:8 + tr, :] = x_ref[rows, :]
        y = b_ref[...] + w[kw - 1:kw, :] * x_ref[rows, :]
        for k in range(1, kw):
            y = y + w[kw - 1 - k:kw - k, :] * xp_ref[8 - k:8 - k + tr, :]
        o_ref[rows, :] = (y * jax.nn.sigmoid(y)).astype(o_ref.dtype)
    if not with_hist:
        xp_ref[0:8, :] = xp_ref[tr:tr + 8, :]


def _conv_silu(zx, col0, conv_w, conv_b, *, row0, n_seq, seq_len, hist8, tc=1024):
    kw, c = conv_w.shape
    tr = min(seq_len, 512)
    nrt = seq_len // tr
    assert seq_len % tr == 0 and c % tc == 0 and col0 % tc == 0 and row0 % tr == 0
    with_hist = hist8 is not None
    assert not with_hist or nrt == 1
    gs = 1
    if with_hist:
        gs = max(g for g in (1, 2, 4, 8) if n_seq % g == 0 and g * tr <= 512)
    br = gs * tr
    assert row0 % br == 0
    rb0, cb0 = row0 // br, col0 // tc
    in_specs = [pl.BlockSpec((br, tc), lambda s, j, r: (rb0 + s * nrt + r, cb0 + j))]
    args = [zx]
    if with_hist:
        in_specs.append(pl.BlockSpec((gs, 8, tc), lambda s, j, r: (s, 0, j)))
        args.append(hist8)
    in_specs += [pl.BlockSpec((kw, tc), lambda s, j, r: (0, j)),
                 pl.BlockSpec((1, tc), lambda s, j, r: (0, j))]
    args += [conv_w, conv_b.reshape(1, c)]
    return pl.pallas_call(
        functools.partial(_conv_kernel, with_hist=with_hist, kw=kw),
        grid=(n_seq // gs, c // tc, nrt),
        in_specs=in_specs,
        out_specs=pl.BlockSpec((br, tc), lambda s, j, r: (s * nrt + r, j)),
        out_shape=jax.ShapeDtypeStruct((n_seq * seq_len, c), BF16),
        scratch_shapes=[pltpu.VMEM((tr + 8, tc), F32)],
        compiler_params=_params("arbitrary", "arbitrary", "arbitrary"),
        name="conv_silu",
    )(*args)


def _ssd_kernel(*refs, with_h0, with_into, hpg, p):
    if with_into:
        refs = refs[1:]
    if with_h0:
        (x_ref, b_ref, c_ref, z_ref, dt_ref, dtb_ref, alog_ref, dsk_ref, ng_ref, h0_ref,
         y_ref, hf_ref, ht_ref, yd_ref) = refs
    else:
        (x_ref, b_ref, c_ref, z_ref, dt_ref, dtb_ref, alog_ref, dsk_ref, ng_ref,
         y_ref, hf_ref, ht_ref, yd_ref) = refs
        h0_ref = None
    g = pl.program_id(1)
    ci = pl.program_id(2)
    lc, gw = x_ref.shape
    nh = dt_ref.shape[1]

    @pl.when(ci == 0)
    def _():
        if with_h0:
            ht_ref[...] = h0_ref[...].T
        else:
            ht_ref[...] = jnp.zeros(ht_ref.shape, F32)

    dt = jax.nn.softplus(dt_ref[...] + dtb_ref[...])
    da = dt * (-jnp.exp(alog_ref[...]))
    r_i = lax.broadcasted_iota(jnp.int32, (lc, lc), 0)
    c_i = lax.broadcasted_iota(jnp.int32, (lc, lc), 1)
    tril = r_i >= c_i
    cs_h = _dot_01_f32(tril.astype(BF16), da)
    head_of_col = g * hpg + lax.broadcasted_iota(jnp.int32, (nh, gw), 1) // p
    expand = (lax.broadcasted_iota(jnp.int32, (nh, gw), 0) == head_of_col).astype(BF16)
    dt_e = _dot_f32_01(dt, expand)
    cs = _dot_f32_01(cs_h, expand)
    cs_last = cs[lc - 1:lc, :]
    eye = r_i == c_i

    xs = x_ref[...].astype(F32)
    xdt = xs * dt_e
    xdt_b = xdt.astype(BF16)
    bm = b_ref[...]
    cm = c_ref[...]
    cb = lax.dot_general(cm, bm, (((1,), (1,)), ((), ())), preferred_element_type=F32)

    for j in range(hpg):
        col = jnp.broadcast_to(cs[:, j * p:j * p + 1], (lc, lc))
        row = jnp.sum(jnp.where(eye, col, 0.0), axis=0, keepdims=True)
        dec = jnp.exp(jnp.where(tril, col - row, -jnp.inf))
        mj = (cb * dec).astype(BF16)
        yd_ref[:, j * p:(j + 1) * p] = jnp.dot(mj, xdt_b[:, j * p:(j + 1) * p], preferred_element_type=F32)

    ht = ht_ref[...]
    y_off = jnp.dot(cm, ht.astype(BF16), preferred_element_type=F32) * jnp.exp(cs)
    xw = (xdt * jnp.exp(cs_last - cs)).astype(BF16)
    st_t = lax.dot_general(bm, xw, (((0,), (0,)), ((), ())), preferred_element_type=F32)
    ht_new = jnp.exp(cs_last) * ht + st_t
    ht_ref[...] = ht_new

    @pl.when(ci == pl.num_programs(2) - 1)
    def _():
        hf_ref[...] = ht_new.T

    y = yd_ref[...] + y_off + dsk_ref[...] * xs
    z = z_ref[...]
    y = y * (z * jax.nn.sigmoid(z))
    ms = jnp.mean(y * y, axis=-1, keepdims=True)
    y_ref[...] = (y * lax.rsqrt(ms + RMS_EPS) * ng_ref[...]).astype(y_ref.dtype)


def _ssd(act, zx, dt_raw, dt_bias, a_log, d_skip, norm_g, h0, *, row0, act_row0, n_seq, seq_len, lc,
         n_heads, p, n_state, n_groups, y_into=None):
    inner = n_heads * p
    hpg = n_heads // n_groups
    gw = hpg * p
    nc = seq_len // lc
    assert seq_len % lc == 0 and row0 % lc == 0 and act_row0 % lc == 0 and gw % 128 == 0 and n_state % 128 == 0
    rb0, ab0 = row0 // lc, act_row0 // lc
    bcol = inner // n_state
    with_h0 = h0 is not None
    d_e = jnp.repeat(d_skip.astype(F32), p).reshape(1, inner)
    in_specs = [
        pl.BlockSpec((lc, gw), lambda s, g, c: (ab0 + s * nc + c, g)),
        pl.BlockSpec((lc, n_state), lambda s, g, c: (ab0 + s * nc + c, bcol + g)),
        pl.BlockSpec((lc, n_state), lambda s, g, c: (ab0 + s * nc + c, bcol + n_groups + g)),
        pl.BlockSpec((lc, gw), lambda s, g, c: (rb0 + s * nc + c, g)),
        pl.BlockSpec((lc, n_heads), lambda s, g, c: (rb0 + s * nc + c, 0)),
        pl.BlockSpec((1, n_heads), lambda s, g, c: (0, 0)),
        pl.BlockSpec((1, n_heads), lambda s, g, c: (0, 0)),
        pl.BlockSpec((1, gw), lambda s, g, c: (0, g)),
        pl.BlockSpec((1, gw), lambda s, g, c: (0, g)),
    ]
    args = [act, act, act, zx, dt_raw, dt_bias.reshape(1, n_heads).astype(F32),
            a_log.reshape(1, n_heads).astype(F32), d_e, norm_g.reshape(1, inner).astype(F32)]
    if with_h0:
        in_specs.append(pl.BlockSpec((None, gw, n_state), lambda s, g, c: (s, g, 0)))
        args.append(h0)
    aliases = {}
    if y_into is not None:
        in_specs.insert(0, pl.BlockSpec(memory_space=pl.ANY))
        args.insert(0, y_into)
        aliases = {0: 0}
    return pl.pallas_call(
        functools.partial(_ssd_kernel, with_h0=with_h0, with_into=y_into is not None, hpg=hpg, p=p),
        grid=(n_seq, n_groups, nc),
        in_specs=in_specs,
        out_specs=[pl.BlockSpec((lc, gw), lambda s, g, c: (rb0 + s * nc + c, g)),
                   pl.BlockSpec((None, gw, n_state), lambda s, g, c: (s, g, 0))],
        out_shape=[jax.ShapeDtypeStruct((zx.shape[0], inner), BF16),
                   jax.ShapeDtypeStruct((n_seq, inner, n_state), F32)],
        scratch_shapes=[pltpu.VMEM((n_state, gw), F32), pltpu.VMEM((lc, gw), F32)],
        input_output_aliases=aliases,
        compiler_params=_params("arbitrary", "arbitrary", "arbitrary"),
        name="ssd",
    )(*args)


def _sb_block(q, k, v, r_run, acc, scale, diag):
    tq, tk = q.shape[0], k.shape[0]
    z = lax.dot_general(q, k, (((1,), (1,)), ((), ())), preferred_element_type=F32) * scale
    ls = jnp.minimum(z, 0.0) - jnp.log(1.0 + jnp.exp(-jnp.abs(z)))
    l1 = ls - z
    if diag:
        mask = lax.broadcasted_iota(jnp.int32, (tq, tk), 1) < lax.broadcasted_iota(jnp.int32, (tq, tk), 0)
        l1 = jnp.where(mask, l1, 0.0)
    later = (lax.broadcasted_iota(jnp.int32, (tk, tk), 0) > lax.broadcasted_iota(jnp.int32, (tk, tk), 1))
    after = _dot_f32_01(l1, later.astype(BF16))
    w = jnp.exp(ls + after + r_run)
    if diag:
        w = jnp.where(mask, w, 0.0)
    acc = acc + jnp.dot(w.astype(BF16), v, preferred_element_type=F32)
    r_run = r_run + jnp.sum(l1, axis=1, keepdims=True)
    return r_run, acc


def _sb_kernel(q_ref, k_ref, v_ref, o_ref, *, hb, dh, scale):
    tq = tk = q_ref.shape[0]

    def head(ref, rows, h):
        return ref[rows, h * dh:(h + 1) * dh]

    qi = pl.program_id(2)
    own = pl.ds(pl.multiple_of(qi * tq, tq), tq)

    state = []
    for h in range(hb):
        q = head(q_ref, slice(None), h)
        r_run, acc = _sb_block(q, head(k_ref, own, h).astype(BF16), head(v_ref, own, h).astype(BF16),
                               jnp.zeros((tq, 1), F32), jnp.zeros((tq, dh), F32), scale, True)
        state += [r_run, acc]
    rmax = functools.reduce(jnp.maximum, [jnp.max(state[2 * h]) for h in range(hb)])

    def cond(carry):
        kb, rmax = carry[0], carry[1]
        return jnp.logical_and(kb >= 0, rmax > EXP_ZERO_BELOW)

    def body(carry):
        kb, st = carry[0], list(carry[2:])
        rows = pl.ds(pl.multiple_of(kb * tk, tk), tk)
        for h in range(hb):
            q = head(q_ref, slice(None), h)
            st[2 * h], st[2 * h + 1] = _sb_block(q, head(k_ref, rows, h).astype(BF16),
                                                 head(v_ref, rows, h).astype(BF16),
                                                 st[2 * h], st[2 * h + 1], scale, False)
        rmax = functools.reduce(jnp.maximum, [jnp.max(st[2 * h]) for h in range(hb)])
        return (kb - 1, rmax, *st)

    out = lax.while_loop(cond, body, (qi - 1, rmax, *state))
    for h in range(hb):
        o_ref[:, h * dh:(h + 1) * dh] = out[3 + 2 * h].astype(o_ref.dtype)


def _sb_past_kernel(into_ref, q_ref, k_ref, v_ref, pk_hbm, pv_hbm, o_ref, rm_ref, r_ref, acc_ref, pk_ref, pv_ref, sem,
                    *, hb, dh, tk, last, scale, single):
    s, hg, pb = pl.program_id(0), pl.program_id(1), pl.program_id(2)
    tq = q_ref.shape[0]
    rows = pl.ds((last - pb) * tk, tk)
    copies = []
    for h in range(hb):
        copies.append(pltpu.make_async_copy(pk_hbm.at[s, rows, hg * hb + h, :], pk_ref.at[h], sem.at[0, h]))
        copies.append(pltpu.make_async_copy(pv_hbm.at[s, rows, hg * hb + h, :], pv_ref.at[h], sem.at[1, h]))
    for cp in copies:
        cp.start()

    def cols(h):
        return slice(h * dh, (h + 1) * dh)

    def own(h):
        return _sb_block(q_ref[:, cols(h)], k_ref[:, cols(h)].astype(BF16), v_ref[:, cols(h)].astype(BF16),
                         jnp.zeros((tq, 1), F32), jnp.zeros((tq, dh), F32), scale, True)

    if single:
        state = [own(h) for h in range(hb)]
        for cp in copies:
            cp.wait()
        rmax = None
        for h in range(hb):
            r_run, acc = _sb_block(q_ref[:, cols(h)], pk_ref[h].astype(BF16), pv_ref[h].astype(BF16),
                                   state[h][0], state[h][1], scale, False)
            o_ref[:, cols(h)] = acc.astype(o_ref.dtype)
            rmax = r_run if rmax is None else jnp.maximum(rmax, r_run)
        rm_ref[...] = jnp.broadcast_to(jnp.max(rmax), rm_ref.shape)
        return

    @pl.when(pb == 0)
    def _():
        for h in range(hb):
            r_run, acc = own(h)
            r_ref[h] = jnp.broadcast_to(r_run, (tq, 128))
            acc_ref[:, cols(h)] = acc

    for cp in copies:
        cp.wait()
    rmax = jnp.max(r_ref[...])

    @pl.when(rmax > EXP_ZERO_BELOW)
    def _():
        for h in range(hb):
            r_run, acc = _sb_block(q_ref[:, cols(h)], pk_ref[h].astype(BF16), pv_ref[h].astype(BF16),
                                   r_ref[h][:, 0:1], acc_ref[:, cols(h)], scale, False)
            r_ref[h] = jnp.broadcast_to(r_run, (tq, 128))
            acc_ref[:, cols(h)] = acc

    @pl.when(pb == pl.num_programs(2) - 1)
    def _():
        o_ref[...] = acc_ref[...].astype(o_ref.dtype)
        rm_ref[...] = jnp.broadcast_to(jnp.max(r_ref[...]), rm_ref.shape)


def _sb_prompt(q, k, v, *, n_seq, seq_len, dh, tq=256, hb=4):
    d = q.shape[1]
    nq = seq_len // tq
    assert seq_len % tq == 0 and d % (hb * dh) == 0
    hw = hb * dh
    kv = pl.BlockSpec((seq_len, hw), lambda b, h, i: (b, h))
    return pl.pallas_call(
        functools.partial(_sb_kernel, hb=hb, dh=dh, scale=dh ** -0.5),
        grid=(n_seq, d // hw, nq),
        in_specs=[pl.BlockSpec((tq, hw), lambda b, h, i: (b * nq + i, h)), kv, kv],
        out_specs=pl.BlockSpec((tq, hw), lambda b, h, i: (b * nq + i, h)),
        out_shape=jax.ShapeDtypeStruct(q.shape, BF16),
        compiler_params=_params("arbitrary", "arbitrary", "arbitrary"),
        name="sb_prompt",
    )(q, k, v)


def _sb_past(att_into, q, k, v, past_k, past_v, *, row0, seq_len, n_blocks, tk, hb=8):
    d = q.shape[1]
    n_seq, past_len, n_h, dh = past_k.shape
    hb = min(hb, n_h)
    assert past_len % tk == 0 and row0 % seq_len == 0 and n_h % hb == 0
    hw = hb * dh
    ng = n_h // hb
    rb0 = row0 // seq_len
    last = past_len // tk - 1
    new = pl.BlockSpec((seq_len, hw), lambda s, h, b: (rb0 + s, h))
    new_kv = pl.BlockSpec((seq_len, hw), lambda s, h, b: (s, h))
    hbm = pl.BlockSpec(memory_space=pl.ANY)
    return pl.pallas_call(
        functools.partial(_sb_past_kernel, hb=hb, dh=dh, tk=tk, last=last, scale=dh ** -0.5, single=n_blocks == 1),
        grid=(n_seq, ng, n_blocks),
        in_specs=[hbm, new, new_kv, new_kv, hbm, hbm],
        out_specs=[new, pl.BlockSpec((None, 8, 128), lambda s, h, b: (s * ng + h, 0, 0))],
        out_shape=[jax.ShapeDtypeStruct(q.shape, BF16), jax.ShapeDtypeStruct((n_seq * ng, 8, 128), F32)],
        scratch_shapes=[pltpu.VMEM((hb, seq_len, 128), F32), pltpu.VMEM((seq_len, hw), F32),
                        pltpu.VMEM((hb, tk, dh), F32), pltpu.VMEM((hb, tk, dh), F32),
                        pltpu.SemaphoreType.DMA((2, hb))],
        input_output_aliases={0: 0},
        compiler_params=_params("arbitrary", "arbitrary", "arbitrary"),
        name="sb_past",
    )(att_into, q, k, v, past_k, past_v)


def _sb_sample(att_into, q, k, v, past_k, past_v, *, row0, seq_len, tk=256):
    past_len = past_k.shape[1]
    tk = min(tk, past_len)
    nblk = past_len // tk
    kw = dict(row0=row0, seq_len=seq_len, tk=tk)
    att, rm = _sb_past(att_into, q, k, v, past_k, past_v, n_blocks=1, **kw)
    if nblk == 1:
        return att
    return lax.cond(jnp.max(rm) > EXP_ZERO_BELOW,
                    lambda a: _sb_past(a, q, k, v, past_k, past_v, n_blocks=nblk, **kw)[0],
                    lambda a: a, att)


def _gather_rows_kernel(tok_ref, src_hbm, dst_hbm, sem, *, n_rows, window):
    def copy(i):
        return pltpu.make_async_copy(src_hbm.at[tok_ref[i]], dst_hbm.at[i], sem.at[i % window])

    def issue(i, carry):
        @pl.when(i >= window)
        def _():
            copy(i - window).wait()
        copy(i).start()
        return carry

    lax.fori_loop(0, n_rows, issue, 0)

    def drain(i, carry):
        copy(i).wait()
        return carry

    lax.fori_loop(n_rows - window, n_rows, drain, 0)


def _gather_rows(tok, x, window=32):
    t, d = x.shape
    n = tok.shape[0]
    assert d % 128 == 0 and (d // 128 in (1, 2, 4) or d % (8 * 128) == 0) and n >= window
    x3 = x.reshape(t, d // 128, 128)
    out = pl.pallas_call(
        functools.partial(_gather_rows_kernel, n_rows=n, window=window),
        grid_spec=pltpu.PrefetchScalarGridSpec(
            num_scalar_prefetch=1, grid=(),
            in_specs=[pl.BlockSpec(memory_space=pl.ANY)],
            out_specs=pl.BlockSpec(memory_space=pl.ANY),
            scratch_shapes=[pltpu.SemaphoreType.DMA((window,))]),
        out_shape=jax.ShapeDtypeStruct((n, d // 128, 128), x.dtype),
        name="gather_rows",
    )(tok, x3)
    return out.reshape(n, d)


def _expert_mm_kernel(start_ref, cnt_ref, x_hbm, *rest, n_w, tm, tn):
    w_refs, o_hbm = rest[:n_w], rest[n_w]
    wb_refs = rest[n_w + 1:2 * n_w + 1]
    xbuf, obuf, sem_in, sem_out = rest[2 * n_w + 1:]
    e, c = pl.program_id(0), pl.program_id(1)
    n = cnt_ref[e]
    b0 = start_ref[e]

    def in_copy(r, slot):
        return pltpu.make_async_copy(x_hbm.at[pl.ds((b0 + r) * tm, tm)], xbuf.at[slot], sem_in.at[slot])

    def out_copy(r, slot):
        return pltpu.make_async_copy(obuf.at[slot], o_hbm.at[pl.ds((b0 + r) * tm, tm), pl.ds(c * tn, tn)],
                                     sem_out.at[slot])

    @pl.when(n > 0)
    def _():
        in_copy(0, 0).start()

    for w_ref, wb_ref in zip(w_refs, wb_refs):
        wb_ref[...] = w_ref[...].astype(BF16)

    def block(r, carry):
        slot = r % 2

        @pl.when(r + 1 < n)
        def _():
            in_copy(r + 1, 1 - slot).start()

        in_copy(r, slot).wait()

        @pl.when(r >= 2)
        def _():
            out_copy(r - 2, slot).wait()

        x = xbuf[slot]
        res = jnp.dot(x, wb_refs[0][...], preferred_element_type=F32)
        if n_w == 2:
            res = res * jax.nn.sigmoid(res) * jnp.dot(x, wb_refs[1][...], preferred_element_type=F32)
        obuf[slot] = res.astype(obuf.dtype)
        out_copy(r, slot).start()
        return carry

    lax.fori_loop(0, n, block, 0)

    @pl.when(n >= 2)
    def _():
        out_copy(n - 2, n % 2).wait()

    @pl.when(n >= 1)
    def _():
        out_copy(n - 1, (n + 1) % 2).wait()


def _expert_mm(x, ws, layer, blk_start, blk_cnt, *, tm, tn, out_dtype, name):
    pr, k = x.shape
    n_e, n = ws[0].shape[1], ws[0].shape[-1]
    tn = min(tn, n)
    n_w = len(ws)
    hbm = pl.BlockSpec(memory_space=pl.ANY)
    wspec = pl.BlockSpec((None, None, k, tn), lambda e, c, st, cn: (layer, e, 0, c))
    return pl.pallas_call(
        functools.partial(_expert_mm_kernel, n_w=n_w, tm=tm, tn=tn),
        grid_spec=pltpu.PrefetchScalarGridSpec(
            num_scalar_prefetch=2, grid=(n_e, n // tn),
            in_specs=[hbm] + [wspec] * n_w,
            out_specs=hbm,
            scratch_shapes=[pltpu.VMEM((k, tn), BF16)] * n_w + [
                pltpu.VMEM((2, tm, k), BF16), pltpu.VMEM((2, tm, tn), out_dtype),
                pltpu.SemaphoreType.DMA((2,)), pltpu.SemaphoreType.DMA((2,))]),
        out_shape=jax.ShapeDtypeStruct((pr, n), out_dtype),
        compiler_params=_params("arbitrary", "arbitrary"),
        name=name,
    )(blk_start, blk_cnt, x, *ws)


def _route_kernel(s_ref, rb_ref, idx_ref, gate_ref, rank_ref, cnt_ref, carry_ref, *, n_groups):
    @pl.when(pl.program_id(0) == 0)
    def _():
        carry_ref[...] = jnp.zeros(carry_ref.shape, F32)

    s = s_ref[...]
    tr, e = s.shape
    per = e // n_groups
    sel = s + rb_ref[...]
    lane_i = lax.broadcasted_iota(jnp.int32, (tr, e), 1)
    lane = lane_i.astype(F32)
    grp = (lane_i // per).astype(F32)
    neg = -jnp.inf

    def top1(v):
        m = jnp.max(v, axis=1, keepdims=True)
        return m, jnp.min(jnp.where(v == m, lane, float(e)), axis=1, keepdims=True)

    best = gi = None
    for g in range(n_groups):
        vg = jnp.where(grp == float(g), sel, neg)
        m1, i1 = top1(vg)
        m2, _ = top1(jnp.where(lane == i1, neg, vg))
        score = m1 + m2
        if best is None:
            best, gi = score, jnp.zeros_like(score)
        else:
            better = score > best
            gi = jnp.where(better, float(g), gi)
            best = jnp.where(better, score, best)
    vc = jnp.where(grp == gi, sel, neg)
    _, e1 = top1(vc)
    _, e2 = top1(jnp.where(lane == e1, neg, vc))
    oh1, oh2 = lane == e1, lane == e2
    g1 = jnp.sum(jnp.where(oh1, s, 0.0), axis=1, keepdims=True)
    g2 = jnp.sum(jnp.where(oh2, s, 0.0), axis=1, keepdims=True)
    den = g1 + g2

    oh = jnp.logical_or(oh1, oh2)
    earlier = (lax.broadcasted_iota(jnp.int32, (tr, tr), 1) < lax.broadcasted_iota(jnp.int32, (tr, tr), 0))
    before = jnp.dot(earlier.astype(BF16), oh.astype(BF16), preferred_element_type=F32) + carry_ref[...]
    r1 = jnp.sum(jnp.where(oh1, before, 0.0), axis=1, keepdims=True)
    r2 = jnp.sum(jnp.where(oh2, before, 0.0), axis=1, keepdims=True)
    carry = carry_ref[...] + jnp.sum(oh.astype(F32), axis=0, keepdims=True)
    carry_ref[...] = carry
    cnt_ref[...] = carry.astype(jnp.int32)

    first = lax.broadcasted_iota(jnp.int32, (tr, TOP_K), 1) == 0
    idx_ref[...] = jnp.where(first, e1, e2).astype(jnp.int32)
    gate_ref[...] = jnp.where(first, g1, g2) / den
    rank_ref[...] = jnp.where(first, r1, r2).astype(jnp.int32)


def _route(s, router_b, tr=512):
    t, e = s.shape
    tr = _pick_tile(t, tr)
    col = pl.BlockSpec((tr, TOP_K), lambda i: (i, 0))
    return pl.pallas_call(
        functools.partial(_route_kernel, n_groups=N_EXPERT_GROUPS),
        grid=(t // tr,),
        in_specs=[pl.BlockSpec((tr, e), lambda i: (i, 0)), pl.BlockSpec((1, e), lambda i: (0, 0))],
        out_specs=[col, col, col, pl.BlockSpec((1, e), lambda i: (0, 0))],
        out_shape=[jax.ShapeDtypeStruct((t, TOP_K), jnp.int32), jax.ShapeDtypeStruct((t, TOP_K), F32),
                   jax.ShapeDtypeStruct((t, TOP_K), jnp.int32), jax.ShapeDtypeStruct((1, e), jnp.int32)],
        scratch_shapes=[pltpu.VMEM((1, e), F32)],
        compiler_params=_params("arbitrary"),
        name="route",
    )(s, router_b.reshape(1, e).astype(F32))


def _moe(h, s, router_b, w_gate, w_up, w_down, layer, *, tm=256):
    t, d = h.shape
    e = s.shape[1]
    idx, gates, rank, counts = _route(s, router_b)
    counts = counts.reshape(e)
    padded = (counts + tm - 1) // tm * tm
    pad_ends = jnp.cumsum(padded)
    pad_starts = pad_ends - padded
    slot = pad_starts[idx] + rank
    nb = -(-(t * TOP_K) // tm) + e
    tok = jnp.broadcast_to(jnp.arange(t, dtype=jnp.int32)[:, None], (t, TOP_K))
    tok_of_slot = jnp.zeros((nb * tm,), jnp.int32).at[slot.reshape(-1)].set(tok.reshape(-1))
    xs = _gather_rows(tok_of_slot, h)
    blk_start, blk_cnt = (pad_starts // tm).astype(jnp.int32), (padded // tm).astype(jnp.int32)
    hid = _expert_mm(xs, [w_gate, w_up], layer, blk_start, blk_cnt, tm=tm, tn=512, out_dtype=BF16, name="moe_up")
    yb = _expert_mm(hid, [w_down], layer, blk_start, blk_cnt, tm=tm, tn=1024, out_dtype=F32, name="moe_down")
    return [yb[slot[:, k]] for k in range(TOP_K)], [gates[:, k:k + 1] for k in range(TOP_K)]


def kernel(x_prompt, x_sample, c_prompt, c_sample, state_conv, state_ssm, cache_k, cache_v, w_ada, b_ada, ln_g, ln_b, ssm_w_in, ssm_conv_w, ssm_conv_b, ssm_dt_bias, ssm_a_log, ssm_d, ssm_norm_g, ssm_w_out, sb_w_k, sb_w_v, sb_w_q, sb_w_o, router_w, router_b, moe_w_gate, moe_w_up, moe_w_down):
    bp, lp, d = x_prompt.shape
    bs, ls_, _ = x_sample.shape
    assert ls_ == ROWS and lp % ROWS == 0 and w_ada.shape[0] == DEPTH == 2
    n_heads, p, n_state = state_ssm.shape[2:]
    inner = n_heads * p
    conv_dim = ssm_conv_w.shape[-1]
    kw = ssm_conv_w.shape[1]
    n_groups = (conv_dim - inner) // (2 * n_state)
    sb_heads, dh = cache_k.shape[2:]
    past_len = cache_k.shape[1]
    tp, ts = bp * lp, bs * ls_
    t = tp + ts

    x = jnp.concatenate([x_prompt.reshape(tp, d), x_sample.reshape(ts, d)], axis=0)
    c_all = jnp.concatenate([c_prompt, c_sample], axis=0)
    gmap = jnp.concatenate([jnp.repeat(jnp.arange(bp), lp // ROWS), bp + jnp.arange(bs)])
    mod = _ada(c_all, w_ada, b_ada)
    modg = [mod[l][gmap][:, None, :] for l in range(DEPTH)]

    h = _modulate(x, modg[0], 1, 0)
    w_in = ssm_w_in[0]
    n_zx = inner + conv_dim
    zx = _matmul(h, w_in, n_out=n_zx, name="in_proj")
    dt_raw = _matmul(h, w_in, n_off=n_zx // 128, n_out=n_heads, tn=128, name="in_proj_dt")
    hist8 = jnp.pad(state_conv[0], ((0, 0), (8 - (kw - 1), 0), (0, 0)))
    act_p = _conv_silu(zx, inner, ssm_conv_w[0], ssm_conv_b[0], row0=0, n_seq=bp, seq_len=lp, hist8=None)
    act_s = _conv_silu(zx, inner, ssm_conv_w[0], ssm_conv_b[0], row0=tp, n_seq=bs, seq_len=ls_, hist8=hist8)
    ssd_kw = dict(n_heads=n_heads, p=p, n_state=n_state, n_groups=n_groups)
    y, hfin_p = _ssd(act_p, zx, dt_raw, ssm_dt_bias[0], ssm_a_log[0], ssm_d[0], ssm_norm_g[0], None,
                     row0=0, act_row0=0, n_seq=bp, seq_len=lp, lc=min(SSD_CHUNK, lp), **ssd_kw)
    y, hfin_s = _ssd(act_s, zx, dt_raw, ssm_dt_bias[0], ssm_a_log[0], ssm_d[0], ssm_norm_g[0],
                     state_ssm[0].reshape(bs, inner, n_state),
                     row0=tp, act_row0=0, n_seq=bs, seq_len=ls_, lc=min(SSD_CHUNK, ls_), y_into=y, **ssd_kw)
    kh = inner // 2
    m = _matmul(y, ssm_w_out[0], kblk=kh, x_koff=0, w_koff=0, name="out_proj0")
    m = _matmul(y, ssm_w_out[0], kblk=kh, x_koff=1, w_koff=1, acc=m, name="out_proj1")

    def last_rows(row0, n_seq, seq_len):
        return jnp.stack([lax.slice(zx, (row0 + (i + 1) * seq_len - (kw - 1), inner), (row0 + (i + 1) * seq_len, n_zx))
                          for i in range(n_seq)])[None]

    conv_p, conv_s = last_rows(0, bp, lp), last_rows(tp, bs, ls_)
    ssm_p = hfin_p.reshape(1, bp, n_heads, p, n_state)
    ssm_s = hfin_s.reshape(1, bs, n_heads, p, n_state)

    x, h, s = _ln(x, [m], (modg[0], 2), ln_g[0, 0], ln_b[0, 0], next_mod=(modg[0], 4, 3), router_w=router_w)
    ys, gs = _moe(h, s, router_b, moe_w_gate, moe_w_up, moe_w_down, 0)
    x, h, xb = _ln(x, ys, (modg[0], 5), ln_g[0, 1], ln_b[0, 1], m_gates=gs, next_mod=(modg[1], 1, 0), with_xb=True)

    k_p, k_s = _matmul(xb, sb_w_k, split_rows=tp, name="k_proj")
    v_p, v_s = _matmul(xb, sb_w_v, split_rows=tp, name="v_proj")

    q = _matmul(h, sb_w_q, lead=(0,), out_dtype=BF16, name="q_proj")
    att = _sb_prompt(q, k_p, v_p, n_seq=bp, seq_len=lp, dh=dh)
    att = _sb_sample(att, q, k_s, v_s, cache_k, cache_v, row0=tp, seq_len=ls_)
    m = _matmul(att, sb_w_o, lead=(0,), name="o_proj")

    x, h, s = _ln(x, [m], (modg[1], 2), ln_g[1, 0], ln_b[1, 0], next_mod=(modg[1], 4, 3), router_w=router_w)
    ys, gs = _moe(h, s, router_b, moe_w_gate, moe_w_up, moe_w_down, 1)
    y_p, y_s = _ln(x, ys, (modg[1], 5), ln_g[1, 1], ln_b[1, 1], m_gates=gs, split_rows=tp)

    def heads(a, nb_, l):
        return a.reshape(nb_, l, sb_heads, dh)

    return (y_p.reshape(bp, lp, d), y_s.reshape(bs, ls_, d), conv_p, ssm_p,
            heads(k_p, bp, lp), heads(v_p, bp, lp), conv_s, ssm_s, heads(k_s, bs, ls_), heads(v_s, bs, ls_))
```

```python
import functools
import math

import jax
import jax.numpy as jnp
from jax import lax
from jax.experimental import pallas as pl
from jax.experimental.pallas import tpu as pltpu

F32 = jnp.float32
BF16 = jnp.bfloat16
HIGHEST = lax.Precision.HIGHEST

DEPTH = 2
N_EXPERT_GROUPS = 4
TOP_K = 2
DN_ALPHA = (2 * DEPTH) ** 0.25
LN_EPS = 1e-5
RMS_EPS = 1e-5
SSD_CHUNK = 128

ROWS = 32
VMEM_LIMIT = 52 * 1024 * 1024
EXP_ZERO_BELOW = -105.0


def _params(*sem):
    return pltpu.CompilerParams(dimension_semantics=sem, vmem_limit_bytes=VMEM_LIMIT)


def _split3(x):
    hi = x.astype(BF16)
    r1 = x - hi.astype(F32)
    mid = r1.astype(BF16)
    lo = (r1 - mid.astype(F32)).astype(BF16)
    return hi, mid, lo


def _dot_f32_01(x, m01):
    hi, mid, lo = _split3(x)
    d = functools.partial(jnp.dot, preferred_element_type=F32)
    return (d(lo, m01) + d(mid, m01)) + d(hi, m01)


def _dot_01_f32(m01, x):
    hi, mid, lo = _split3(x)
    d = functools.partial(jnp.dot, preferred_element_type=F32)
    return (d(m01, lo) + d(m01, mid)) + d(m01, hi)


def _pick_tile(n, cap, mult=128):
    best = None
    for c in range(mult, min(n, cap) + 1, mult):
        if n % c == 0:
            best = c
    assert best is not None, (n, cap)
    return best


def _ada_kernel(c_ref, w_ref, b_ref, o_ref):
    c = c_ref[...]
    s = (c * jax.nn.sigmoid(c)).astype(BF16)
    o_ref[...] = jnp.dot(s, w_ref[...].astype(BF16), preferred_element_type=F32) + b_ref[...]


def _ada(c_all, w_ada, b_ada):
    depth, d, n = w_ada.shape
    bc = c_all.shape[0]
    tn = 512
    return pl.pallas_call(
        _ada_kernel,
        grid=(depth, n // tn),
        in_specs=[pl.BlockSpec((bc, d), lambda l, j: (0, 0)),
                  pl.BlockSpec((None, d, tn), lambda l, j: (l, 0, j)),
                  pl.BlockSpec((None, 1, tn), lambda l, j: (l, 0, j))],
        out_specs=pl.BlockSpec((None, bc, tn), lambda l, j: (l, 0, j)),
        out_shape=jax.ShapeDtypeStruct((depth, bc, n), F32),
        compiler_params=_params("arbitrary", "arbitrary"),
        name="ada",
    )(c_all, w_ada, b_ada.reshape(depth, 1, n))


def _mod_kernel(x_ref, sc_ref, sh_ref, h_ref):
    gb = sc_ref.shape[0]
    rows, d = x_ref.shape
    x = x_ref[...].reshape(gb, rows // gb, d)
    h = x * (1.0 + sc_ref[...]) + sh_ref[...]
    h_ref[...] = h.reshape(rows, d).astype(h_ref.dtype)


def _modulate(x, modg, scale_blk, shift_blk, gb=8):
    t, d = x.shape
    g = modg.shape[0]
    return pl.pallas_call(
        _mod_kernel,
        grid=(g // gb,),
        in_specs=[pl.BlockSpec((gb * ROWS, d), lambda i: (i, 0)),
                  pl.BlockSpec((gb, 1, d), lambda i: (i, 0, scale_blk)),
                  pl.BlockSpec((gb, 1, d), lambda i: (i, 0, shift_blk))],
        out_specs=pl.BlockSpec((gb * ROWS, d), lambda i: (i, 0)),
        out_shape=jax.ShapeDtypeStruct((t, d), BF16),
        compiler_params=_params("arbitrary"),
        name="modulate",
    )(x, modg, modg)


def _ln_kernel(*refs, n_m, gated_m, with_h, with_router, with_xb):
    it = iter(refs)
    x_ref = next(it)
    m_refs = [next(it) for _ in range(n_m)]
    mg_refs = [next(it) for _ in range(n_m)] if gated_m else []
    gate_ref, g_ref, b_ref = next(it), next(it), next(it)
    sc_ref = sh_ref = rw_ref = None
    if with_h:
        sc_ref, sh_ref = next(it), next(it)
    if with_router:
        rw_ref = next(it)
    xo_ref = next(it)
    h_ref = next(it) if with_h else None
    s_ref = next(it) if with_router else None
    xb_ref = next(it) if with_xb else None

    gb = gate_ref.shape[0]
    rows, d = x_ref.shape
    if gated_m:
        m = m_refs[0][...] * mg_refs[0][...]
        for mr, gr in zip(m_refs[1:], mg_refs[1:]):
            m = m + mr[...] * gr[...]
    else:
        m = m_refs[0][...]
        for mr in m_refs[1:]:
            m = m + mr[...]
    v = DN_ALPHA * x_ref[...] + (gate_ref[...] * m.reshape(gb, rows // gb, d)).reshape(rows, d)
    mu = jnp.mean(v, axis=-1, keepdims=True)
    vc = v - mu
    var = jnp.mean(vc * vc, axis=-1, keepdims=True)
    xn = vc * lax.rsqrt(var + LN_EPS) * g_ref[...] + b_ref[...]
    xo_ref[...] = xn
    if with_xb:
        xb_ref[...] = xn.astype(BF16)
    if with_h:
        h = xn.reshape(gb, rows // gb, d) * (1.0 + sc_ref[...]) + sh_ref[...]
        h = h.reshape(rows, d)
        h_ref[...] = h.astype(h_ref.dtype)
        if with_router:
            logits = jnp.dot(h, rw_ref[...], preferred_element_type=F32, precision=HIGHEST)
            s_ref[...] = jax.nn.sigmoid(logits)


def _ln(x, ms, gate_src, ln_g, ln_b, *, m_gates=None, next_mod=None, h_dtype=BF16, router_w=None, with_xb=False,
        rows=None, gb=8):
    d = x.shape[1]
    modg, gate_blk = gate_src
    tr = gb * ROWS
    r0, t = (0, x.shape[0]) if rows is None else rows
    assert r0 % tr == 0 and t % tr == 0
    off = r0 // tr
    row = pl.BlockSpec((tr, d), lambda i: (i + off, 0))
    out_row = pl.BlockSpec((tr, d), lambda i: (i, 0))
    vec = pl.BlockSpec((1, d), lambda i: (0, 0))

    def modspec(blk):
        return pl.BlockSpec((gb, 1, d), lambda i: (i + off, 0, blk))

    args = [x] + list(ms)
    specs = [row] + [row] * len(ms)
    if m_gates is not None:
        args += list(m_gates)
        specs += [pl.BlockSpec((tr, 1), lambda i: (i + off, 0))] * len(ms)
    args += [modg, ln_g.reshape(1, d), ln_b.reshape(1, d)]
    specs += [modspec(gate_blk), vec, vec]
    outs = [jax.ShapeDtypeStruct((t, d), F32)]
    out_specs = [out_row]
    row = out_row
    if next_mod is not None:
        nm, sc_blk, sh_blk = next_mod
        args += [nm, nm]
        specs += [modspec(sc_blk), modspec(sh_blk)]
        outs.append(jax.ShapeDtypeStruct((t, d), h_dtype))
        out_specs.append(row)
    if router_w is not None:
        e = router_w.shape[1]
        args.append(router_w)
        specs.append(pl.BlockSpec((d, e), lambda i: (0, 0)))
        outs.append(jax.ShapeDtypeStruct((t, e), F32))
        out_specs.append(pl.BlockSpec((tr, e), lambda i: (i, 0)))
    if with_xb:
        outs.append(jax.ShapeDtypeStruct((t, d), BF16))
        out_specs.append(row)
    kern = functools.partial(_ln_kernel, n_m=len(ms), gated_m=m_gates is not None,
                             with_h=next_mod is not None, with_router=router_w is not None, with_xb=with_xb)
    return pl.pallas_call(
        kern, grid=(t // tr,), in_specs=specs, out_specs=out_specs, out_shape=outs,
        compiler_params=_params("arbitrary"), name="ln",
    )(*args)


def _mm_kernel(*refs, with_acc, n_first):
    x_ref, w_ref = refs[:2]
    a_ref = refs[2] if with_acc else None
    o_refs, wb_ref = refs[2 + with_acc:-1], refs[-1]

    @pl.when(pl.program_id(1) == 0)
    def _():
        wb_ref[...] = w_ref[...].astype(BF16)

    r = jnp.dot(x_ref[...], wb_ref[...], preferred_element_type=F32)
    if with_acc:
        r = r + a_ref[...]
    if n_first is None:
        o_refs[0][...] = r.astype(o_refs[0].dtype)
    else:
        @pl.when(pl.program_id(1) < n_first)
        def _():
            o_refs[0][...] = r.astype(o_refs[0].dtype)

        @pl.when(pl.program_id(1) >= n_first)
        def _():
            o_refs[1][...] = r.astype(o_refs[1].dtype)


def _matmul(x, w, *, lead=(), kblk=None, x_koff=0, w_koff=0, n_off=0, n_out=None, tm=1024, tn=512,
            out_dtype=F32, acc=None, split_rows=None, name="matmul"):
    t = x.shape[0]
    kfull, nfull = w.shape[-2:]
    kblk = kfull if kblk is None else kblk
    n_out = nfull if n_out is None else n_out
    tm = _pick_tile(t if split_rows is None else math.gcd(t, split_rows), tm)
    assert n_out % tn == 0
    nl = len(lead)
    in_specs = [pl.BlockSpec((tm, kblk), lambda j, i: (i, x_koff)),
                pl.BlockSpec((None,) * nl + (kblk, tn), lambda j, i: tuple(lead) + (w_koff, j + n_off))]
    args = [x, w]
    if acc is not None:
        in_specs.append(pl.BlockSpec((tm, tn), lambda j, i: (i, j)))
        args.append(acc)
    if split_rows is None:
        n_first = None
        out_specs = pl.BlockSpec((tm, tn), lambda j, i: (i, j))
        out_shape = jax.ShapeDtypeStruct((t, n_out), out_dtype)
    else:
        assert split_rows % tm == 0 and 0 < split_rows < t
        n_first = split_rows // tm
        out_specs = [pl.BlockSpec((tm, tn), lambda j, i: (jnp.minimum(i, n_first - 1), j)),
                     pl.BlockSpec((tm, tn), lambda j, i: (jnp.maximum(i - n_first, 0), j))]
        out_shape = [jax.ShapeDtypeStruct((split_rows, n_out), out_dtype),
                     jax.ShapeDtypeStruct((t - split_rows, n_out), out_dtype)]
    return pl.pallas_call(
        functools.partial(_mm_kernel, with_acc=acc is not None, n_first=n_first),
        grid=(n_out // tn, t // tm),
        in_specs=in_specs,
        out_specs=out_specs,
        out_shape=out_shape,
        scratch_shapes=[pltpu.VMEM((kblk, tn), BF16)],
        compiler_params=_params("arbitrary", "arbitrary"),
        name=name,
    )(*args)


def _conv_kernel(*refs, with_hist, kw):
    if with_hist:
        x_ref, hist_ref, w_ref, b_ref, o_ref, xp_ref = refs
        gs = hist_ref.shape[0]
    else:
        x_ref, w_ref, b_ref, o_ref, xp_ref = refs
        gs = 1
    tr = x_ref.shape[0] // gs
    w = w_ref[...]
    if not with_hist:
        @pl.when(pl.program_id(2) == 0)
        def _():
            xp_ref[0:8, :] = jnp.zeros((8, xp_ref.shape[1]), F32)
    for s in range(gs):
        rows = slice(s * tr, (s + 1) * tr)
        if with_hist:
            xp_ref[0:8, :] = hist_ref[s]
        xp_ref[8:8 + tr, :] = x_ref[rows, :]
        y = b_ref[...] + w[kw - 1:kw, :] * x_ref[rows, :]
        for k in range(1, kw):
            y = y + w[kw - 1 - k:kw - k, :] * xp_ref[8 - k:8 - k + tr, :]
        o_ref[rows, :] = (y * jax.nn.sigmoid(y)).astype(o_ref.dtype)
    if not with_hist:
        xp_ref[0:8, :] = xp_ref[tr:tr + 8, :]


def _conv_silu(zx, col0, conv_w, conv_b, *, row0, n_seq, seq_len, hist8, tc=1024):
    kw, c = conv_w.shape
    tr = min(seq_len, 512)
    nrt = seq_len // tr
    assert seq_len % tr == 0 and c % tc == 0 and col0 % tc == 0 and row0 % tr == 0
    with_hist = hist8 is not None
    assert not with_hist or nrt == 1
    gs = 1
    if with_hist:
        gs = max(g for g in (1, 2, 4, 8) if n_seq % g == 0 and g * tr <= 512)
    br = gs * tr
    assert row0 % br == 0
    rb0, cb0 = row0 // br, col0 // tc
    in_specs = [pl.BlockSpec((br, tc), lambda s, j, r: (rb0 + s * nrt + r, cb0 + j))]
    args = [zx]
    if with_hist:
        in_specs.append(pl.BlockSpec((gs, 8, tc), lambda s, j, r: (s, 0, j)))
        args.append(hist8)
    in_specs += [pl.BlockSpec((kw, tc), lambda s, j, r: (0, j)),
                 pl.BlockSpec((1, tc), lambda s, j, r: (0, j))]
    args += [conv_w, conv_b.reshape(1, c)]
    return pl.pallas_call(
        functools.partial(_conv_kernel, with_hist=with_hist, kw=kw),
        grid=(n_seq // gs, c // tc, nrt),
        in_specs=in_specs,
        out_specs=pl.BlockSpec((br, tc), lambda s, j, r: (s * nrt + r, j)),
        out_shape=jax.ShapeDtypeStruct((n_seq * seq_len, c), BF16),
        scratch_shapes=[pltpu.VMEM((tr + 8, tc), F32)],
        compiler_params=_params("arbitrary", "arbitrary", "arbitrary"),
        name="conv_silu",
    )(*args)


def _ssd_kernel(*refs, with_h0, with_into, hpg, p):
    if with_into:
        refs = refs[1:]
    if with_h0:
        (x_ref, b_ref, c_ref, z_ref, dt_ref, dtb_ref, alog_ref, dsk_ref, ng_ref, h0_ref,
         y_ref, hf_ref, ht_ref, yd_ref) = refs
    else:
        (x_ref, b_ref, c_ref, z_ref, dt_ref, dtb_ref, alog_ref, dsk_ref, ng_ref,
         y_ref, hf_ref, ht_ref, yd_ref) = refs
        h0_ref = None
    g = pl.program_id(1)
    ci = pl.program_id(2)
    lc, gw = x_ref.shape
    nh = dt_ref.shape[1]

    @pl.when(ci == 0)
    def _():
        if with_h0:
            ht_ref[...] = h0_ref[...].T
        else:
            ht_ref[...] = jnp.zeros(ht_ref.shape, F32)

    dt = jax.nn.softplus(dt_ref[...] + dtb_ref[...])
    da = dt * (-jnp.exp(alog_ref[...]))
    r_i = lax.broadcasted_iota(jnp.int32, (lc, lc), 0)
    c_i = lax.broadcasted_iota(jnp.int32, (lc, lc), 1)
    tril = r_i >= c_i
    cs_h = _dot_01_f32(tril.astype(BF16), da)
    head_of_col = g * hpg + lax.broadcasted_iota(jnp.int32, (nh, gw), 1) // p
    expand = (lax.broadcasted_iota(jnp.int32, (nh, gw), 0) == head_of_col).astype(BF16)
    dt_e = _dot_f32_01(dt, expand)
    cs = _dot_f32_01(cs_h, expand)
    cs_last = cs[lc - 1:lc, :]
    eye = r_i == c_i

    xs = x_ref[...].astype(F32)
    xdt = xs * dt_e
    xdt_b = xdt.astype(BF16)
    bm = b_ref[...]
    cm = c_ref[...]
    cb = lax.dot_general(cm, bm, (((1,), (1,)), ((), ())), preferred_element_type=F32)

    for j in range(hpg):
        col = jnp.broadcast_to(cs[:, j * p:j * p + 1], (lc, lc))
        row = jnp.sum(jnp.where(eye, col, 0.0), axis=0, keepdims=True)
        dec = jnp.exp(jnp.where(tril, col - row, -jnp.inf))
        mj = (cb * dec).astype(BF16)
        yd_ref[:, j * p:(j + 1) * p] = jnp.dot(mj, xdt_b[:, j * p:(j + 1) * p], preferred_element_type=F32)

    ht = ht_ref[...]
    y_off = jnp.dot(cm, ht.astype(BF16), preferred_element_type=F32) * jnp.exp(cs)
    xw = (xdt * jnp.exp(cs_last - cs)).astype(BF16)
    st_t = lax.dot_general(bm, xw, (((0,), (0,)), ((), ())), preferred_element_type=F32)
    ht_new = jnp.exp(cs_last) * ht + st_t
    ht_ref[...] = ht_new

    @pl.when(ci == pl.num_programs(2) - 1)
    def _():
        hf_ref[...] = ht_new.T

    y = yd_ref[...] + y_off + dsk_ref[...] * xs
    z = z_ref[...]
    y = y * (z * jax.nn.sigmoid(z))
    ms = jnp.mean(y * y, axis=-1, keepdims=True)
    y_ref[...] = (y * lax.rsqrt(ms + RMS_EPS) * ng_ref[...]).astype(y_ref.dtype)


def _ssd(act, zx, dt_raw, dt_bias, a_log, d_skip, norm_g, h0, *, row0, act_row0, n_seq, seq_len, lc,
         n_heads, p, n_state, n_groups, y_into=None):
    inner = n_heads * p
    hpg = n_heads // n_groups
    gw = hpg * p
    nc = seq_len // lc
    assert seq_len % lc == 0 and row0 % lc == 0 and act_row0 % lc == 0 and gw % 128 == 0 and n_state % 128 == 0
    rb0, ab0 = row0 // lc, act_row0 // lc
    bcol = inner // n_state
    with_h0 = h0 is not None
    d_e = jnp.repeat(d_skip.astype(F32), p).reshape(1, inner)
    in_specs = [
        pl.BlockSpec((lc, gw), lambda s, g, c: (ab0 + s * nc + c, g)),
        pl.BlockSpec((lc, n_state), lambda s, g, c: (ab0 + s * nc + c, bcol + g)),
        pl.BlockSpec((lc, n_state), lambda s, g, c: (ab0 + s * nc + c, bcol + n_groups + g)),
        pl.BlockSpec((lc, gw), lambda s, g, c: (rb0 + s * nc + c, g)),
        pl.BlockSpec((lc, n_heads), lambda s, g, c: (rb0 + s * nc + c, 0)),
        pl.BlockSpec((1, n_heads), lambda s, g, c: (0, 0)),
        pl.BlockSpec((1, n_heads), lambda s, g, c: (0, 0)),
        pl.BlockSpec((1, gw), lambda s, g, c: (0, g)),
        pl.BlockSpec((1, gw), lambda s, g, c: (0, g)),
    ]
    args = [act, act, act, zx, dt_raw, dt_bias.reshape(1, n_heads).astype(F32),
            a_log.reshape(1, n_heads).astype(F32), d_e, norm_g.reshape(1, inner).astype(F32)]
    if with_h0:
        in_specs.append(pl.BlockSpec((None, gw, n_state), lambda s, g, c: (s, g, 0)))
        args.append(h0)
    aliases = {}
    if y_into is not None:
        in_specs.insert(0, pl.BlockSpec(memory_space=pl.ANY))
        args.insert(0, y_into)
        aliases = {0: 0}
    return pl.pallas_call(
        functools.partial(_ssd_kernel, with_h0=with_h0, with_into=y_into is not None, hpg=hpg, p=p),
        grid=(n_seq, n_groups, nc),
        in_specs=in_specs,
        out_specs=[pl.BlockSpec((lc, gw), lambda s, g, c: (rb0 + s * nc + c, g)),
                   pl.BlockSpec((None, gw, n_state), lambda s, g, c: (s, g, 0))],
        out_shape=[jax.ShapeDtypeStruct((zx.shape[0], inner), BF16),
                   jax.ShapeDtypeStruct((n_seq, inner, n_state), F32)],
        scratch_shapes=[pltpu.VMEM((n_state, gw), F32), pltpu.VMEM((lc, gw), F32)],
        input_output_aliases=aliases,
        compiler_params=_params("arbitrary", "arbitrary", "arbitrary"),
        name="ssd",
    )(*args)


def _sb_block(q, k, v, r_run, acc, scale, diag):
    tq, tk = q.shape[0], k.shape[0]
    z = lax.dot_general(q, k, (((1,), (1,)), ((), ())), preferred_element_type=F32) * scale
    ls = jnp.minimum(z, 0.0) - jnp.log(1.0 + jnp.exp(-jnp.abs(z)))
    l1 = ls - z
    if diag:
        mask = lax.broadcasted_iota(jnp.int32, (tq, tk), 1) < lax.broadcasted_iota(jnp.int32, (tq, tk), 0)
        l1 = jnp.where(mask, l1, 0.0)
    later = (lax.broadcasted_iota(jnp.int32, (tk, tk), 0) > lax.broadcasted_iota(jnp.int32, (tk, tk), 1))
    after = _dot_f32_01(l1, later.astype(BF16))
    w = jnp.exp(ls + after + r_run)
    if diag:
        w = jnp.where(mask, w, 0.0)
    acc = acc + jnp.dot(w.astype(BF16), v, preferred_element_type=F32)
    r_run = r_run + jnp.sum(l1, axis=1, keepdims=True)
    return r_run, acc


def _sb_kernel(q_ref, k_ref, v_ref, o_ref, *, hb, dh, scale):
    tq = tk = q_ref.shape[0]

    def head(ref, rows, h):
        return ref[rows, h * dh:(h + 1) * dh]

    qi = pl.program_id(2)
    own = pl.ds(pl.multiple_of(qi * tq, tq), tq)

    state = []
    for h in range(hb):
        q = head(q_ref, slice(None), h)
        r_run, acc = _sb_block(q, head(k_ref, own, h).astype(BF16), head(v_ref, own, h).astype(BF16),
                               jnp.zeros((tq, 1), F32), jnp.zeros((tq, dh), F32), scale, True)
        state += [r_run, acc]
    rmax = functools.reduce(jnp.maximum, [jnp.max(state[2 * h]) for h in range(hb)])

    def cond(carry):
        kb, rmax = carry[0], carry[1]
        return jnp.logical_and(kb >= 0, rmax > EXP_ZERO_BELOW)

    def body(carry):
        kb, st = carry[0], list(carry[2:])
        rows = pl.ds(pl.multiple_of(kb * tk, tk), tk)
        for h in range(hb):
            q = head(q_ref, slice(None), h)
            st[2 * h], st[2 * h + 1] = _sb_block(q, head(k_ref, rows, h).astype(BF16),
                                                 head(v_ref, rows, h).astype(BF16),
                                                 st[2 * h], st[2 * h + 1], scale, False)
        rmax = functools.reduce(jnp.maximum, [jnp.max(st[2 * h]) for h in range(hb)])
        return (kb - 1, rmax, *st)

    out = lax.while_loop(cond, body, (qi - 1, rmax, *state))
    for h in range(hb):
        o_ref[:, h * dh:(h + 1) * dh] = out[3 + 2 * h].astype(o_ref.dtype)


def _sb_past_kernel(into_ref, q_ref, k_ref, v_ref, pk_hbm, pv_hbm, o_ref, rm_ref, r_ref, acc_ref, pk_ref, pv_ref, sem,
                    *, hb, dh, tk, last, scale, single):
    s, hg, pb = pl.program_id(0), pl.program_id(1), pl.program_id(2)
    tq = q_ref.shape[0]
    rows = pl.ds((last - pb) * tk, tk)
    copies = []
    for h in range(hb):
        copies.append(pltpu.make_async_copy(pk_hbm.at[s, rows, hg * hb + h, :], pk_ref.at[h], sem.at[0, h]))
        copies.append(pltpu.make_async_copy(pv_hbm.at[s, rows, hg * hb + h, :], pv_ref.at[h], sem.at[1, h]))
    for cp in copies:
        cp.start()

    def cols(h):
        return slice(h * dh, (h + 1) * dh)

    def own(h):
        return _sb_block(q_ref[:, cols(h)], k_ref[:, cols(h)].astype(BF16), v_ref[:, cols(h)].astype(BF16),
                         jnp.zeros((tq, 1), F32), jnp.zeros((tq, dh), F32), scale, True)

    if single:
        state = [own(h) for h in range(hb)]
        for cp in copies:
            cp.wait()
        rmax = None
        for h in range(hb):
            r_run, acc = _sb_block(q_ref[:, cols(h)], pk_ref[h].astype(BF16), pv_ref[h].astype(BF16),
                                   state[h][0], state[h][1], scale, False)
            o_ref[:, cols(h)] = acc.astype(o_ref.dtype)
            rmax = r_run if rmax is None else jnp.maximum(rmax, r_run)
        rm_ref[...] = jnp.broadcast_to(jnp.max(rmax), rm_ref.shape)
        return

    @pl.when(pb == 0)
    def _():
        for h in range(hb):
            r_run, acc = own(h)
            r_ref[h] = jnp.broadcast_to(r_run, (tq, 128))
            acc_ref[:, cols(h)] = acc

    for cp in copies:
        cp.wait()
    rmax = jnp.max(r_ref[...])

    @pl.when(rmax > EXP_ZERO_BELOW)
    def _():
        for h in range(hb):
            r_run, acc = _sb_block(q_ref[:, cols(h)], pk_ref[h].astype(BF16), pv_ref[h].astype(BF16),
                                   r_ref[h][:, 0:1], acc_ref[:, cols(h)], scale, False)
            r_ref[h] = jnp.broadcast_to(r_run, (tq, 128))
            acc_ref[:, cols(h)] = acc

    @pl.when(pb == pl.num_programs(2) - 1)
    def _():
        o_ref[...] = acc_ref[...].astype(o_ref.dtype)
        rm_ref[...] = jnp.broadcast_to(jnp.max(r_ref[...]), rm_ref.shape)


def _sb_prompt(q, k, v, *, n_seq, seq_len, dh, tq=256, hb=4):
    d = q.shape[1]
    nq = seq_len // tq
    assert seq_len % tq == 0 and d % (hb * dh) == 0
    hw = hb * dh
    kv = pl.BlockSpec((seq_len, hw), lambda b, h, i: (b, h))
    return pl.pallas_call(
        functools.partial(_sb_kernel, hb=hb, dh=dh, scale=dh ** -0.5),
        grid=(n_seq, d // hw, nq),
        in_specs=[pl.BlockSpec((tq, hw), lambda b, h, i: (b * nq + i, h)), kv, kv],
        out_specs=pl.BlockSpec((tq, hw), lambda b, h, i: (b * nq + i, h)),
        out_shape=jax.ShapeDtypeStruct(q.shape, BF16),
        compiler_params=_params("arbitrary", "arbitrary", "arbitrary"),
        name="sb_prompt",
    )(q, k, v)


def _sb_past(att_into, q, k, v, past_k, past_v, *, row0, seq_len, n_blocks, tk, hb=8):
    d = q.shape[1]
    n_seq, past_len, n_h, dh = past_k.shape
    hb = min(hb, n_h)
    assert past_len % tk == 0 and row0 % seq_len == 0 and n_h % hb == 0
    hw = hb * dh
    ng = n_h // hb
    rb0 = row0 // seq_len
    last = past_len // tk - 1
    new = pl.BlockSpec((seq_len, hw), lambda s, h, b: (rb0 + s, h))
    new_kv = pl.BlockSpec((seq_len, hw), lambda s, h, b: (s, h))
    hbm = pl.BlockSpec(memory_space=pl.ANY)
    return pl.pallas_call(
        functools.partial(_sb_past_kernel, hb=hb, dh=dh, tk=tk, last=last, scale=dh ** -0.5, single=n_blocks == 1),
        grid=(n_seq, ng, n_blocks),
        in_specs=[hbm, new, new_kv, new_kv, hbm, hbm],
        out_specs=[new, pl.BlockSpec((None, 8, 128), lambda s, h, b: (s * ng + h, 0, 0))],
        out_shape=[jax.ShapeDtypeStruct(q.shape, BF16), jax.ShapeDtypeStruct((n_seq * ng, 8, 128), F32)],
        scratch_shapes=[pltpu.VMEM((hb, seq_len, 128), F32), pltpu.VMEM((seq_len, hw), F32),
                        pltpu.VMEM((hb, tk, dh), F32), pltpu.VMEM((hb, tk, dh), F32),
                        pltpu.SemaphoreType.DMA((2, hb))],
        input_output_aliases={0: 0},
        compiler_params=_params("arbitrary", "arbitrary", "arbitrary"),
        name="sb_past",
    )(att_into, q, k, v, past_k, past_v)


def _sb_sample(att_into, q, k, v, past_k, past_v, *, row0, seq_len, tk=256):
    past_len = past_k.shape[1]
    tk = min(tk, past_len)
    nblk = past_len // tk
    kw = dict(row0=row0, seq_len=seq_len, tk=tk)
    att, rm = _sb_past(att_into, q, k, v, past_k, past_v, n_blocks=1, **kw)
    if nblk == 1:
        return att
    return lax.cond(jnp.max(rm) > EXP_ZERO_BELOW,
                    lambda a: _sb_past(a, q, k, v, past_k, past_v, n_blocks=nblk, **kw)[0],
                    lambda a: a, att)


def _gather_rows_kernel(tok_ref, nused_ref, src_hbm, o_ref, buf_ref, sem, *, tm):
    b = pl.program_id(0)

    @pl.when(b < nused_ref[0])
    def _():
        for j in range(tm):
            tok = tok_ref[b * tm + j]
            pltpu.make_async_copy(src_hbm.at[tok // 8, tok % 8], buf_ref.at[j // 8, j % 8],
                                  sem.at[0]).start(priority=j % 2)
        pltpu.make_async_copy(src_hbm.at[pl.ds(0, tm // 8)], buf_ref, sem.at[0]).wait()
        o_ref[...] = buf_ref[...].reshape(o_ref.shape).astype(o_ref.dtype)


def _gather_rows(tok, n_used, x, *, tm):
    t, d = x.shape
    n = tok.shape[0]
    assert t % 8 == 0 and tm % 8 == 0 and n % tm == 0
    return pl.pallas_call(
        functools.partial(_gather_rows_kernel, tm=tm),
        grid_spec=pltpu.PrefetchScalarGridSpec(
            num_scalar_prefetch=2, grid=(n // tm,),
            in_specs=[pl.BlockSpec(memory_space=pl.ANY)],
            out_specs=pl.BlockSpec((tm, d), lambda b, tok, nu: (b, 0)),
            scratch_shapes=[pltpu.VMEM((tm // 8, 8, d), F32), pltpu.SemaphoreType.DMA((1,))]),
        out_shape=jax.ShapeDtypeStruct((n, d), BF16),
        compiler_params=_params("arbitrary"),
        name="gather_rows",
    )(tok, n_used, x.reshape(t // 8, 8, d))


def _expert_mm_kernel(start_ref, cnt_ref, x_hbm, *rest, n_w, tm, tn):
    w_refs, o_hbm = rest[:n_w], rest[n_w]
    wb_refs = rest[n_w + 1:2 * n_w + 1]
    xbuf, obuf, sem_in, sem_out = rest[2 * n_w + 1:]
    e, c = pl.program_id(0), pl.program_id(1)
    n = cnt_ref[e]
    b0 = start_ref[e]

    def in_copy(r, slot):
        return pltpu.make_async_copy(x_hbm.at[pl.ds((b0 + r) * tm, tm)], xbuf.at[slot], sem_in.at[slot])

    def out_copy(r, slot):
        return pltpu.make_async_copy(obuf.at[slot], o_hbm.at[pl.ds((b0 + r) * tm, tm), pl.ds(c * tn, tn)],
                                     sem_out.at[slot])

    @pl.when(n > 0)
    def _():
        in_copy(0, 0).start(priority=1)

    for w_ref, wb_ref in zip(w_refs, wb_refs):
        wb_ref[...] = w_ref[...].astype(BF16)

    def block(r, carry):
        slot = r % 2

        @pl.when(r + 1 < n)
        def _():
            in_copy(r + 1, 1 - slot).start(priority=1)

        in_copy(r, slot).wait()

        @pl.when(r >= 2)
        def _():
            out_copy(r - 2, slot).wait()

        x = xbuf[slot]
        res = jnp.dot(x, wb_refs[0][...], preferred_element_type=F32)
        if n_w == 2:
            res = res * jax.nn.sigmoid(res) * jnp.dot(x, wb_refs[1][...], preferred_element_type=F32)
        obuf[slot] = res.astype(obuf.dtype)
        out_copy(r, slot).start()
        return carry

    lax.fori_loop(0, n, block, 0)

    @pl.when(n >= 2)
    def _():
        out_copy(n - 2, n % 2).wait()

    @pl.when(n >= 1)
    def _():
        out_copy(n - 1, (n + 1) % 2).wait()


def _expert_mm(x, ws, layer, blk_start, blk_cnt, *, tm, tn, out_dtype, name):
    pr, k = x.shape
    n_e, n = ws[0].shape[1], ws[0].shape[-1]
    tn = min(tn, n)
    n_w = len(ws)
    hbm = pl.BlockSpec(memory_space=pl.ANY)
    wspec = pl.BlockSpec((None, None, k, tn), lambda e, c, st, cn: (layer, e, 0, c))
    return pl.pallas_call(
        functools.partial(_expert_mm_kernel, n_w=n_w, tm=tm, tn=tn),
        grid_spec=pltpu.PrefetchScalarGridSpec(
            num_scalar_prefetch=2, grid=(n_e, n // tn),
            in_specs=[hbm] + [wspec] * n_w,
            out_specs=hbm,
            scratch_shapes=[pltpu.VMEM((k, tn), BF16)] * n_w + [
                pltpu.VMEM((2, tm, k), BF16), pltpu.VMEM((2, tm, tn), out_dtype),
                pltpu.SemaphoreType.DMA((2,)), pltpu.SemaphoreType.DMA((2,))]),
        out_shape=jax.ShapeDtypeStruct((pr, n), out_dtype),
        compiler_params=_params("arbitrary", "arbitrary"),
        name=name,
    )(blk_start, blk_cnt, x, *ws)


def _route_kernel(s_ref, rb_ref, idx_ref, gate_ref, rank_ref, cnt_ref, carry_ref, *, n_groups):
    @pl.when(pl.program_id(0) == 0)
    def _():
        carry_ref[...] = jnp.zeros(carry_ref.shape, F32)

    s = s_ref[...]
    tr, e = s.shape
    per = e // n_groups
    sel = s + rb_ref[...]
    lane_i = lax.broadcasted_iota(jnp.int32, (tr, e), 1)
    lane = lane_i.astype(F32)
    grp = (lane_i // per).astype(F32)
    neg = -jnp.inf

    def top1(v):
        m = jnp.max(v, axis=1, keepdims=True)
        return m, jnp.min(jnp.where(v == m, lane, float(e)), axis=1, keepdims=True)

    best = gi = None
    for g in range(n_groups):
        vg = jnp.where(grp == float(g), sel, neg)
        m1, i1 = top1(vg)
        m2, _ = top1(jnp.where(lane == i1, neg, vg))
        score = m1 + m2
        if best is None:
            best, gi = score, jnp.zeros_like(score)
        else:
            better = score > best
            gi = jnp.where(better, float(g), gi)
            best = jnp.where(better, score, best)
    vc = jnp.where(grp == gi, sel, neg)
    _, e1 = top1(vc)
    _, e2 = top1(jnp.where(lane == e1, neg, vc))
    oh1, oh2 = lane == e1, lane == e2
    g1 = jnp.sum(jnp.where(oh1, s, 0.0), axis=1, keepdims=True)
    g2 = jnp.sum(jnp.where(oh2, s, 0.0), axis=1, keepdims=True)
    den = g1 + g2

    oh = jnp.logical_or(oh1, oh2)
    earlier = (lax.broadcasted_iota(jnp.int32, (tr, tr), 1) < lax.broadcasted_iota(jnp.int32, (tr, tr), 0))
    before = jnp.dot(earlier.astype(BF16), oh.astype(BF16), preferred_element_type=F32) + carry_ref[...]
    r1 = jnp.sum(jnp.where(oh1, before, 0.0), axis=1, keepdims=True)
    r2 = jnp.sum(jnp.where(oh2, before, 0.0), axis=1, keepdims=True)
    carry = carry_ref[...] + jnp.sum(oh.astype(F32), axis=0, keepdims=True)
    carry_ref[...] = carry
    cnt_ref[...] = carry.astype(jnp.int32)

    first = lax.broadcasted_iota(jnp.int32, (tr, TOP_K), 1) == 0
    idx_ref[...] = jnp.where(first, e1, e2).astype(jnp.int32)
    gate_ref[...] = jnp.where(first, g1, g2) / den
    rank_ref[...] = jnp.where(first, r1, r2).astype(jnp.int32)


def _route(s, router_b, tr=512):
    t, e = s.shape
    tr = _pick_tile(t, tr)
    col = pl.BlockSpec((tr, TOP_K), lambda i: (i, 0))
    return pl.pallas_call(
        functools.partial(_route_kernel, n_groups=N_EXPERT_GROUPS),
        grid=(t // tr,),
        in_specs=[pl.BlockSpec((tr, e), lambda i: (i, 0)), pl.BlockSpec((1, e), lambda i: (0, 0))],
        out_specs=[col, col, col, pl.BlockSpec((1, e), lambda i: (0, 0))],
        out_shape=[jax.ShapeDtypeStruct((t, TOP_K), jnp.int32), jax.ShapeDtypeStruct((t, TOP_K), F32),
                   jax.ShapeDtypeStruct((t, TOP_K), jnp.int32), jax.ShapeDtypeStruct((1, e), jnp.int32)],
        scratch_shapes=[pltpu.VMEM((1, e), F32)],
        compiler_params=_params("arbitrary"),
        name="route",
    )(s, router_b.reshape(1, e).astype(F32))


def _moe(h, s, router_b, w_gate, w_up, w_down, layer, *, tm=256):
    t, d = h.shape
    e = s.shape[1]
    idx, gates, rank, counts = _route(s, router_b)
    counts = counts.reshape(e)
    padded = (counts + tm - 1) // tm * tm
    pad_ends = jnp.cumsum(padded)
    pad_starts = pad_ends - padded
    slot = pad_starts[idx] + rank
    nb = -(-(t * TOP_K) // tm) + e
    tok = jnp.broadcast_to(jnp.arange(t, dtype=jnp.int32)[:, None], (t, TOP_K))
    tok_of_slot = jnp.zeros((nb * tm,), jnp.int32).at[slot.reshape(-1)].set(tok.reshape(-1))
    xs = _gather_rows(tok_of_slot, (pad_ends[-1:] // tm).astype(jnp.int32), h, tm=tm)
    blk_start, blk_cnt = (pad_starts // tm).astype(jnp.int32), (padded // tm).astype(jnp.int32)
    hid = _expert_mm(xs, [w_gate, w_up], layer, blk_start, blk_cnt, tm=tm, tn=512, out_dtype=BF16, name="moe_up")
    yb = _expert_mm(hid, [w_down], layer, blk_start, blk_cnt, tm=tm, tn=1024, out_dtype=F32, name="moe_down")
    return [yb[slot[:, k]] for k in range(TOP_K)], [gates[:, k:k + 1] for k in range(TOP_K)]


def kernel(x_prompt, x_sample, c_prompt, c_sample, state_conv, state_ssm, cache_k, cache_v, w_ada, b_ada, ln_g, ln_b, ssm_w_in, ssm_conv_w, ssm_conv_b, ssm_dt_bias, ssm_a_log, ssm_d, ssm_norm_g, ssm_w_out, sb_w_k, sb_w_v, sb_w_q, sb_w_o, router_w, router_b, moe_w_gate, moe_w_up, moe_w_down):
    bp, lp, d = x_prompt.shape
    bs, ls_, _ = x_sample.shape
    assert ls_ == ROWS and lp % ROWS == 0 and w_ada.shape[0] == DEPTH == 2
    n_heads, p, n_state = state_ssm.shape[2:]
    inner = n_heads * p
    conv_dim = ssm_conv_w.shape[-1]
    kw = ssm_conv_w.shape[1]
    n_groups = (conv_dim - inner) // (2 * n_state)
    sb_heads, dh = cache_k.shape[2:]
    past_len = cache_k.shape[1]
    tp, ts = bp * lp, bs * ls_
    t = tp + ts

    x = jnp.concatenate([x_prompt.reshape(tp, d), x_sample.reshape(ts, d)], axis=0)
    c_all = jnp.concatenate([c_prompt, c_sample], axis=0)
    gmap = jnp.concatenate([jnp.repeat(jnp.arange(bp), lp // ROWS), bp + jnp.arange(bs)])
    mod = _ada(c_all, w_ada, b_ada)
    modg = [mod[l][gmap][:, None, :] for l in range(DEPTH)]

    h = _modulate(x, modg[0], 1, 0)
    w_in = ssm_w_in[0]
    n_zx = inner + conv_dim
    zx = _matmul(h, w_in, n_out=n_zx, name="in_proj")
    dt_raw = _matmul(h, w_in, n_off=n_zx // 128, n_out=n_heads, tn=128, name="in_proj_dt")
    hist8 = jnp.pad(state_conv[0], ((0, 0), (8 - (kw - 1), 0), (0, 0)))
    act_p = _conv_silu(zx, inner, ssm_conv_w[0], ssm_conv_b[0], row0=0, n_seq=bp, seq_len=lp, hist8=None)
    act_s = _conv_silu(zx, inner, ssm_conv_w[0], ssm_conv_b[0], row0=tp, n_seq=bs, seq_len=ls_, hist8=hist8)
    ssd_kw = dict(n_heads=n_heads, p=p, n_state=n_state, n_groups=n_groups)
    y, hfin_p = _ssd(act_p, zx, dt_raw, ssm_dt_bias[0], ssm_a_log[0], ssm_d[0], ssm_norm_g[0], None,
                     row0=0, act_row0=0, n_seq=bp, seq_len=lp, lc=min(SSD_CHUNK, lp), **ssd_kw)
    y, hfin_s = _ssd(act_s, zx, dt_raw, ssm_dt_bias[0], ssm_a_log[0], ssm_d[0], ssm_norm_g[0],
                     state_ssm[0].reshape(bs, inner, n_state),
                     row0=tp, act_row0=0, n_seq=bs, seq_len=ls_, lc=min(SSD_CHUNK, ls_), y_into=y, **ssd_kw)
    kh = inner // 2
    m = _matmul(y, ssm_w_out[0], kblk=kh, x_koff=0, w_koff=0, name="out_proj0")
    m = _matmul(y, ssm_w_out[0], kblk=kh, x_koff=1, w_koff=1, acc=m, name="out_proj1")

    def last_rows(row0, n_seq, seq_len):
        return jnp.stack([lax.slice(zx, (row0 + (i + 1) * seq_len - (kw - 1), inner), (row0 + (i + 1) * seq_len, n_zx))
                          for i in range(n_seq)])[None]

    conv_p, conv_s = last_rows(0, bp, lp), last_rows(tp, bs, ls_)
    ssm_p = hfin_p.reshape(1, bp, n_heads, p, n_state)
    ssm_s = hfin_s.reshape(1, bs, n_heads, p, n_state)

    x, h, s = _ln(x, [m], (modg[0], 2), ln_g[0, 0], ln_b[0, 0], next_mod=(modg[0], 4, 3), h_dtype=F32,
                  router_w=router_w)
    ys, gs = _moe(h, s, router_b, moe_w_gate, moe_w_up, moe_w_down, 0)
    x, h, xb = _ln(x, ys, (modg[0], 5), ln_g[0, 1], ln_b[0, 1], m_gates=gs, next_mod=(modg[1], 1, 0), with_xb=True)

    k_p, k_s = _matmul(xb, sb_w_k, split_rows=tp, name="k_proj")
    v_p, v_s = _matmul(xb, sb_w_v, split_rows=tp, name="v_proj")

    q = _matmul(h, sb_w_q, lead=(0,), out_dtype=BF16, name="q_proj")
    att = _sb_prompt(q, k_p, v_p, n_seq=bp, seq_len=lp, dh=dh)
    att = _sb_sample(att, q, k_s, v_s, cache_k, cache_v, row0=tp, seq_len=ls_)
    m = _matmul(att, sb_w_o, lead=(0,), name="o_proj")

    x, h, s = _ln(x, [m], (modg[1], 2), ln_g[1, 0], ln_b[1, 0], next_mod=(modg[1], 4, 3), h_dtype=F32,
                  router_w=router_w)
    ys, gs = _moe(h, s, router_b, moe_w_gate, moe_w_up, moe_w_down, 1)
    (y_p,) = _ln(x, ys, (modg[1], 5), ln_g[1, 1], ln_b[1, 1], m_gates=gs, rows=(0, tp))
    (y_s,) = _ln(x, ys, (modg[1], 5), ln_g[1, 1], ln_b[1, 1], m_gates=gs, rows=(tp, ts))

    def heads(a, nb_, l):
        return a.reshape(nb_, l, sb_heads, dh)

    return (y_p.reshape(bp, lp, d), y_s.reshape(bs, ls_, d), conv_p, ssm_p,
            heads(k_p, bp, lp), heads(v_p, bp, lp), conv_s, ssm_s, heads(k_s, bs, ls_), heads(v_s, bs, ls_))
```

```python
import functools
import math

import jax
import jax.numpy as jnp
from jax import lax
from jax.experimental import pallas as pl
from jax.experimental.pallas import tpu as pltpu

F32 = jnp.float32
BF16 = jnp.bfloat16
HIGHEST = lax.Precision.HIGHEST

DEPTH = 2
N_EXPERT_GROUPS = 4
TOP_K = 2
DN_ALPHA = (2 * DEPTH) ** 0.25
LN_EPS = 1e-5
RMS_EPS = 1e-5
SSD_CHUNK = 128

ROWS = 32
VMEM_LIMIT = 52 * 1024 * 1024
EXP_ZERO_BELOW = -105.0


def _params(*sem):
    return pltpu.CompilerParams(dimension_semantics=sem, vmem_limit_bytes=VMEM_LIMIT)


def _split3(x):
    hi = x.astype(BF16)
    r1 = x - hi.astype(F32)
    mid = r1.astype(BF16)
    lo = (r1 - mid.astype(F32)).astype(BF16)
    return hi, mid, lo


def _dot_f32_01(x, m01):
    hi, mid, lo = _split3(x)
    d = functools.partial(jnp.dot, preferred_element_type=F32)
    return (d(lo, m01) + d(mid, m01)) + d(hi, m01)


def _dot_01_f32(m01, x):
    hi, mid, lo = _split3(x)
    d = functools.partial(jnp.dot, preferred_element_type=F32)
    return (d(m01, lo) + d(m01, mid)) + d(m01, hi)


def _pick_tile(n, cap, mult=128):
    best = None
    for c in range(mult, min(n, cap) + 1, mult):
        if n % c == 0:
            best = c
    assert best is not None, (n, cap)
    return best


def _ada_kernel(c_ref, w_ref, b_ref, o_ref):
    c = c_ref[...]
    s = (c * jax.nn.sigmoid(c)).astype(BF16)
    o_ref[...] = jnp.dot(s, w_ref[...].astype(BF16), preferred_element_type=F32) + b_ref[...]


def _ada(c_all, w_ada, b_ada):
    depth, d, n = w_ada.shape
    bc = c_all.shape[0]
    tn = 512
    return pl.pallas_call(
        _ada_kernel,
        grid=(depth, n // tn),
        in_specs=[pl.BlockSpec((bc, d), lambda l, j: (0, 0)),
                  pl.BlockSpec((None, d, tn), lambda l, j: (l, 0, j)),
                  pl.BlockSpec((None, 1, tn), lambda l, j: (l, 0, j))],
        out_specs=pl.BlockSpec((None, bc, tn), lambda l, j: (l, 0, j)),
        out_shape=jax.ShapeDtypeStruct((depth, bc, n), F32),
        compiler_params=_params("arbitrary", "arbitrary"),
        name="ada",
    )(c_all, w_ada, b_ada.reshape(depth, 1, n))


def _mod_kernel(x_ref, sc_ref, sh_ref, h_ref):
    gb = sc_ref.shape[0]
    rows, d = x_ref.shape
    x = x_ref[...].reshape(gb, rows // gb, d)
    h = x * (1.0 + sc_ref[...]) + sh_ref[...]
    h_ref[...] = h.reshape(rows, d).astype(h_ref.dtype)


def _modulate(x, modg, scale_blk, shift_blk, gb=8):
    t, d = x.shape
    g = modg.shape[0]
    return pl.pallas_call(
        _mod_kernel,
        grid=(g // gb,),
        in_specs=[pl.BlockSpec((gb * ROWS, d), lambda i: (i, 0)),
                  pl.BlockSpec((gb, 1, d), lambda i: (i, 0, scale_blk)),
                  pl.BlockSpec((gb, 1, d), lambda i: (i, 0, shift_blk))],
        out_specs=pl.BlockSpec((gb * ROWS, d), lambda i: (i, 0)),
        out_shape=jax.ShapeDtypeStruct((t, d), BF16),
        compiler_params=_params("arbitrary"),
        name="modulate",
    )(x, modg, modg)


def _ln_kernel(*refs, n_m, gated_m, with_h, with_router, with_xb):
    it = iter(refs)
    x_ref = next(it)
    m_refs = [next(it) for _ in range(n_m)]
    mg_refs = [next(it) for _ in range(n_m)] if gated_m else []
    gate_ref, g_ref, b_ref = next(it), next(it), next(it)
    sc_ref = sh_ref = rw_ref = None
    if with_h:
        sc_ref, sh_ref = next(it), next(it)
    if with_router:
        rw_ref = next(it)
    xo_ref = next(it)
    h_ref = next(it) if with_h else None
    s_ref = next(it) if with_router else None
    xb_ref = next(it) if with_xb else None

    gb = gate_ref.shape[0]
    rows, d = x_ref.shape
    if gated_m:
        m = m_refs[0][...] * mg_refs[0][...]
        for mr, gr in zip(m_refs[1:], mg_refs[1:]):
            m = m + mr[...] * gr[...]
    else:
        m = m_refs[0][...]
        for mr in m_refs[1:]:
            m = m + mr[...]
    v = DN_ALPHA * x_ref[...] + (gate_ref[...] * m.reshape(gb, rows // gb, d)).reshape(rows, d)
    mu = jnp.mean(v, axis=-1, keepdims=True)
    vc = v - mu
    var = jnp.mean(vc * vc, axis=-1, keepdims=True)
    xn = vc * lax.rsqrt(var + LN_EPS) * g_ref[...] + b_ref[...]
    xo_ref[...] = xn
    if with_xb:
        xb_ref[...] = xn.astype(BF16)
    if with_h:
        h = xn.reshape(gb, rows // gb, d) * (1.0 + sc_ref[...]) + sh_ref[...]
        h = h.reshape(rows, d)
        h_ref[...] = h.astype(h_ref.dtype)
        if with_router:
            logits = jnp.dot(h, rw_ref[...], preferred_element_type=F32, precision=HIGHEST)
            s_ref[...] = jax.nn.sigmoid(logits)


def _ln(x, ms, gate_src, ln_g, ln_b, *, m_gates=None, next_mod=None, h_dtype=BF16, router_w=None, with_xb=False,
        rows=None, gb=8):
    d = x.shape[1]
    modg, gate_blk = gate_src
    tr = gb * ROWS
    r0, t = (0, x.shape[0]) if rows is None else rows
    assert r0 % tr == 0 and t % tr == 0
    off = r0 // tr
    row = pl.BlockSpec((tr, d), lambda i: (i + off, 0))
    out_row = pl.BlockSpec((tr, d), lambda i: (i, 0))
    vec = pl.BlockSpec((1, d), lambda i: (0, 0))

    def modspec(blk):
        return pl.BlockSpec((gb, 1, d), lambda i: (i + off, 0, blk))

    args = [x] + list(ms)
    specs = [row] + [row] * len(ms)
    if m_gates is not None:
        args += list(m_gates)
        specs += [pl.BlockSpec((tr, 1), lambda i: (i + off, 0))] * len(ms)
    args += [modg, ln_g.reshape(1, d), ln_b.reshape(1, d)]
    specs += [modspec(gate_blk), vec, vec]
    outs = [jax.ShapeDtypeStruct((t, d), F32)]
    out_specs = [out_row]
    row = out_row
    if next_mod is not None:
        nm, sc_blk, sh_blk = next_mod
        args += [nm, nm]
        specs += [modspec(sc_blk), modspec(sh_blk)]
        outs.append(jax.ShapeDtypeStruct((t, d), h_dtype))
        out_specs.append(row)
    if router_w is not None:
        e = router_w.shape[1]
        args.append(router_w)
        specs.append(pl.BlockSpec((d, e), lambda i: (0, 0)))
        outs.append(jax.ShapeDtypeStruct((t, e), F32))
        out_specs.append(pl.BlockSpec((tr, e), lambda i: (i, 0)))
    if with_xb:
        outs.append(jax.ShapeDtypeStruct((t, d), BF16))
        out_specs.append(row)
    kern = functools.partial(_ln_kernel, n_m=len(ms), gated_m=m_gates is not None,
                             with_h=next_mod is not None, with_router=router_w is not None, with_xb=with_xb)
    return pl.pallas_call(
        kern, grid=(t // tr,), in_specs=specs, out_specs=out_specs, out_shape=outs,
        compiler_params=_params("arbitrary"), name="ln",
    )(*args)


def _mm_kernel(*refs, with_acc, n_first):
    x_ref, w_ref = refs[:2]
    a_ref = refs[2] if with_acc else None
    o_refs, wb_ref = refs[2 + with_acc:-1], refs[-1]

    @pl.when(pl.program_id(1) == 0)
    def _():
        wb_ref[...] = w_ref[...].astype(BF16)

    r = jnp.dot(x_ref[...], wb_ref[...], preferred_element_type=F32)
    if with_acc:
        r = r + a_ref[...]
    if n_first is None:
        o_refs[0][...] = r.astype(o_refs[0].dtype)
    else:
        @pl.when(pl.program_id(1) < n_first)
        def _():
            o_refs[0][...] = r.astype(o_refs[0].dtype)

        @pl.when(pl.program_id(1) >= n_first)
        def _():
            o_refs[1][...] = r.astype(o_refs[1].dtype)


def _matmul(x, w, *, lead=(), kblk=None, x_koff=0, w_koff=0, n_off=0, n_out=None, tm=1024, tn=512,
            out_dtype=F32, acc=None, split_rows=None, name="matmul"):
    t = x.shape[0]
    kfull, nfull = w.shape[-2:]
    kblk = kfull if kblk is None else kblk
    n_out = nfull if n_out is None else n_out
    tm = _pick_tile(t if split_rows is None else math.gcd(t, split_rows), tm)
    assert n_out % tn == 0
    nl = len(lead)
    in_specs = [pl.BlockSpec((tm, kblk), lambda j, i: (i, x_koff)),
                pl.BlockSpec((None,) * nl + (kblk, tn), lambda j, i: tuple(lead) + (w_koff, j + n_off))]
    args = [x, w]
    if acc is not None:
        in_specs.append(pl.BlockSpec((tm, tn), lambda j, i: (i, j)))
        args.append(acc)
    if split_rows is None:
        n_first = None
        out_specs = pl.BlockSpec((tm, tn), lambda j, i: (i, j))
        out_shape = jax.ShapeDtypeStruct((t, n_out), out_dtype)
    else:
        assert split_rows % tm == 0 and 0 < split_rows < t
        n_first = split_rows // tm
        out_specs = [pl.BlockSpec((tm, tn), lambda j, i: (jnp.minimum(i, n_first - 1), j)),
                     pl.BlockSpec((tm, tn), lambda j, i: (jnp.maximum(i - n_first, 0), j))]
        out_shape = [jax.ShapeDtypeStruct((split_rows, n_out), out_dtype),
                     jax.ShapeDtypeStruct((t - split_rows, n_out), out_dtype)]
    return pl.pallas_call(
        functools.partial(_mm_kernel, with_acc=acc is not None, n_first=n_first),
        grid=(n_out // tn, t // tm),
        in_specs=in_specs,
        out_specs=out_specs,
        out_shape=out_shape,
        scratch_shapes=[pltpu.VMEM((kblk, tn), BF16)],
        compiler_params=_params("arbitrary", "arbitrary"),
        name=name,
    )(*args)


def _conv_kernel(*refs, with_hist, kw):
    if with_hist:
        x_ref, hist_ref, w_ref, b_ref, o_ref, xp_ref = refs
        gs = hist_ref.shape[0]
    else:
        x_ref, w_ref, b_ref, o_ref, xp_ref = refs
        gs = 1
    tr = x_ref.shape[0] // gs
    w = w_ref[...]
    if not with_hist:
        @pl.when(pl.program_id(2) == 0)
        def _():
            xp_ref[0:8, :] = jnp.zeros((8, xp_ref.shape[1]), F32)
    for s in range(gs):
        rows = slice(s * tr, (s + 1) * tr)
        if with_hist:
            xp_ref[0:8, :] = hist_ref[s]
        xp_ref[8:8 + tr, :] = x_ref[rows, :]
        y = b_ref[...] + w[kw - 1:kw, :] * x_ref[rows, :]
        for k in range(1, kw):
            y = y + w[kw - 1 - k:kw - k, :] * xp_ref[8 - k:8 - k + tr, :]
        o_ref[rows, :] = (y * jax.nn.sigmoid(y)).astype(o_ref.dtype)
    if not with_hist:
        xp_ref[0:8, :] = xp_ref[tr:tr + 8, :]


def _conv_silu(zx, col0, conv_w, conv_b, *, row0, n_seq, seq_len, hist8, tc=1024):
    kw, c = conv_w.shape
    tr = min(seq_len, 512)
    nrt = seq_len // tr
    assert seq_len % tr == 0 and c % tc == 0 and col0 % tc == 0 and row0 % tr == 0
    with_hist = hist8 is not None
    assert not with_hist or nrt == 1
    gs = 1
    if with_hist:
        gs = max(g for g in (1, 2, 4, 8) if n_seq % g == 0 and g * tr <= 512)
    br = gs * tr
    assert row0 % br == 0
    rb0, cb0 = row0 // br, col0 // tc
    in_specs = [pl.BlockSpec((br, tc), lambda s, j, r: (rb0 + s * nrt + r, cb0 + j))]
    args = [zx]
    if with_hist:
        in_specs.append(pl.BlockSpec((gs, 8, tc), lambda s, j, r: (s, 0, j)))
        args.append(hist8)
    in_specs += [pl.BlockSpec((kw, tc), lambda s, j, r: (0, j)),
                 pl.BlockSpec((1, tc), lambda s, j, r: (0, j))]
    args += [conv_w, conv_b.reshape(1, c)]
    return pl.pallas_call(
        functools.partial(_conv_kernel, with_hist=with_hist, kw=kw),
        grid=(n_seq // gs, c // tc, nrt),
        in_specs=in_specs,
        out_specs=pl.BlockSpec((br, tc), lambda s, j, r: (s * nrt + r, j)),
        out_shape=jax.ShapeDtypeStruct((n_seq * seq_len, c), BF16),
        scratch_shapes=[pltpu.VMEM((tr + 8, tc), F32)],
        compiler_params=_params("arbitrary", "arbitrary", "arbitrary"),
        name="conv_silu",
    )(*args)


def _ssd_kernel(*refs, with_h0, with_into, hpg, p):
    if with_into:
        refs = refs[1:]
    if with_h0:
        (x_ref, b_ref, c_ref, z_ref, dt_ref, dtb_ref, alog_ref, dsk_ref, ng_ref, h0_ref,
         y_ref, hf_ref, ht_ref, yd_ref) = refs
    else:
        (x_ref, b_ref, c_ref, z_ref, dt_ref, dtb_ref, alog_ref, dsk_ref, ng_ref,
         y_ref, hf_ref, ht_ref, yd_ref) = refs
        h0_ref = None
    g = pl.program_id(1)
    ci = pl.program_id(2)
    lc, gw = x_ref.shape
    nh = dt_ref.shape[1]

    @pl.when(ci == 0)
    def _():
        if with_h0:
            ht_ref[...] = h0_ref[...].T
        else:
            ht_ref[...] = jnp.zeros(ht_ref.shape, F32)

    dt = jax.nn.softplus(dt_ref[...] + dtb_ref[...])
    da = dt * (-jnp.exp(alog_ref[...]))
    r_i = lax.broadcasted_iota(jnp.int32, (lc, lc), 0)
    c_i = lax.broadcasted_iota(jnp.int32, (lc, lc), 1)
    tril = r_i >= c_i
    cs_h = _dot_01_f32(tril.astype(BF16), da)
    head_of_col = g * hpg + lax.broadcasted_iota(jnp.int32, (nh, gw), 1) // p
    expand = (lax.broadcasted_iota(jnp.int32, (nh, gw), 0) == head_of_col).astype(BF16)
    dt_e = _dot_f32_01(dt, expand)
    cs = _dot_f32_01(cs_h, expand)
    cs_last = cs[lc - 1:lc, :]
    eye = r_i == c_i

    xs = x_ref[...].astype(F32)
    xdt = xs * dt_e
    xdt_b = xdt.astype(BF16)
    bm = b_ref[...]
    cm = c_ref[...]
    cb = lax.dot_general(cm, bm, (((1,), (1,)), ((), ())), preferred_element_type=F32)

    for j in range(hpg):
        col = jnp.broadcast_to(cs[:, j * p:j * p + 1], (lc, lc))
        row = jnp.sum(jnp.where(eye, col, 0.0), axis=0, keepdims=True)
        dec = jnp.exp(jnp.where(tril, col - row, -jnp.inf))
        mj = (cb * dec).astype(BF16)
        yd_ref[:, j * p:(j + 1) * p] = jnp.dot(mj, xdt_b[:, j * p:(j + 1) * p], preferred_element_type=F32)

    ht = ht_ref[...]
    y_off = jnp.dot(cm, ht.astype(BF16), preferred_element_type=F32) * jnp.exp(cs)
    xw = (xdt * jnp.exp(cs_last - cs)).astype(BF16)
    st_t = lax.dot_general(bm, xw, (((0,), (0,)), ((), ())), preferred_element_type=F32)
    ht_new = jnp.exp(cs_last) * ht + st_t
    ht_ref[...] = ht_new

    @pl.when(ci == pl.num_programs(2) - 1)
    def _():
        hf_ref[...] = ht_new.T

    y = yd_ref[...] + y_off + dsk_ref[...] * xs
    z = z_ref[...]
    y = y * (z * jax.nn.sigmoid(z))
    ms = jnp.mean(y * y, axis=-1, keepdims=True)
    y_ref[...] = (y * lax.rsqrt(ms + RMS_EPS) * ng_ref[...]).astype(y_ref.dtype)


def _ssd(act, zx, dt_raw, dt_bias, a_log, d_skip, norm_g, h0, *, row0, act_row0, n_seq, seq_len, lc,
         n_heads, p, n_state, n_groups, y_into=None):
    inner = n_heads * p
    hpg = n_heads // n_groups
    gw = hpg * p
    nc = seq_len // lc
    assert seq_len % lc == 0 and row0 % lc == 0 and act_row0 % lc == 0 and gw % 128 == 0 and n_state % 128 == 0
    rb0, ab0 = row0 // lc, act_row0 // lc
    bcol = inner // n_state
    with_h0 = h0 is not None
    d_e = jnp.repeat(d_skip.astype(F32), p).reshape(1, inner)
    in_specs = [
        pl.BlockSpec((lc, gw), lambda s, g, c: (ab0 + s * nc + c, g)),
        pl.BlockSpec((lc, n_state), lambda s, g, c: (ab0 + s * nc + c, bcol + g)),
        pl.BlockSpec((lc, n_state), lambda s, g, c: (ab0 + s * nc + c, bcol + n_groups + g)),
        pl.BlockSpec((lc, gw), lambda s, g, c: (rb0 + s * nc + c, g)),
        pl.BlockSpec((lc, n_heads), lambda s, g, c: (rb0 + s * nc + c, 0)),
        pl.BlockSpec((1, n_heads), lambda s, g, c: (0, 0)),
        pl.BlockSpec((1, n_heads), lambda s, g, c: (0, 0)),
        pl.BlockSpec((1, gw), lambda s, g, c: (0, g)),
        pl.BlockSpec((1, gw), lambda s, g, c: (0, g)),
    ]
    args = [act, act, act, zx, dt_raw, dt_bias.reshape(1, n_heads).astype(F32),
            a_log.reshape(1, n_heads).astype(F32), d_e, norm_g.reshape(1, inner).astype(F32)]
    if with_h0:
        in_specs.append(pl.BlockSpec((None, gw, n_state), lambda s, g, c: (s, g, 0)))
        args.append(h0)
    aliases = {}
    if y_into is not None:
        in_specs.insert(0, pl.BlockSpec(memory_space=pl.ANY))
        args.insert(0, y_into)
        aliases = {0: 0}
    return pl.pallas_call(
        functools.partial(_ssd_kernel, with_h0=with_h0, with_into=y_into is not None, hpg=hpg, p=p),
        grid=(n_seq, n_groups, nc),
        in_specs=in_specs,
        out_specs=[pl.BlockSpec((lc, gw), lambda s, g, c: (rb0 + s * nc + c, g)),
                   pl.BlockSpec((None, gw, n_state), lambda s, g, c: (s, g, 0))],
        out_shape=[jax.ShapeDtypeStruct((zx.shape[0], inner), BF16),
                   jax.ShapeDtypeStruct((n_seq, inner, n_state), F32)],
        scratch_shapes=[pltpu.VMEM((n_state, gw), F32), pltpu.VMEM((lc, gw), F32)],
        input_output_aliases=aliases,
        compiler_params=_params("arbitrary", "arbitrary", "arbitrary"),
        name="ssd",
    )(*args)


def _sb_block(q, k, v, r_run, acc, scale, diag):
    tq, tk = q.shape[0], k.shape[0]
    z = lax.dot_general(q, k, (((1,), (1,)), ((), ())), preferred_element_type=F32) * scale
    ls = jnp.minimum(z, 0.0) - jnp.log(1.0 + jnp.exp(-jnp.abs(z)))
    l1 = ls - z
    if diag:
        mask = lax.broadcasted_iota(jnp.int32, (tq, tk), 1) < lax.broadcasted_iota(jnp.int32, (tq, tk), 0)
        l1 = jnp.where(mask, l1, 0.0)
    later = (lax.broadcasted_iota(jnp.int32, (tk, tk), 0) > lax.broadcasted_iota(jnp.int32, (tk, tk), 1))
    after = _dot_f32_01(l1, later.astype(BF16))
    w = jnp.exp(ls + after + r_run)
    if diag:
        w = jnp.where(mask, w, 0.0)
    acc = acc + jnp.dot(w.astype(BF16), v, preferred_element_type=F32)
    r_run = r_run + jnp.sum(l1, axis=1, keepdims=True)
    return r_run, acc


def _sb_kernel(q_ref, k_ref, v_ref, o_ref, *, hb, dh, scale):
    tq = tk = q_ref.shape[0]

    def head(ref, rows, h):
        return ref[rows, h * dh:(h + 1) * dh]

    qi = pl.program_id(2)
    own = pl.ds(pl.multiple_of(qi * tq, tq), tq)

    state = []
    for h in range(hb):
        q = head(q_ref, slice(None), h)
        r_run, acc = _sb_block(q, head(k_ref, own, h).astype(BF16), head(v_ref, own, h).astype(BF16),
                               jnp.zeros((tq, 1), F32), jnp.zeros((tq, dh), F32), scale, True)
        state += [r_run, acc]
    rmax = functools.reduce(jnp.maximum, [jnp.max(state[2 * h]) for h in range(hb)])

    def cond(carry):
        kb, rmax = carry[0], carry[1]
        return jnp.logical_and(kb >= 0, rmax > EXP_ZERO_BELOW)

    def body(carry):
        kb, st = carry[0], list(carry[2:])
        rows = pl.ds(pl.multiple_of(kb * tk, tk), tk)
        for h in range(hb):
            q = head(q_ref, slice(None), h)
            st[2 * h], st[2 * h + 1] = _sb_block(q, head(k_ref, rows, h).astype(BF16),
                                                 head(v_ref, rows, h).astype(BF16),
                                                 st[2 * h], st[2 * h + 1], scale, False)
        rmax = functools.reduce(jnp.maximum, [jnp.max(st[2 * h]) for h in range(hb)])
        return (kb - 1, rmax, *st)

    out = lax.while_loop(cond, body, (qi - 1, rmax, *state))
    for h in range(hb):
        o_ref[:, h * dh:(h + 1) * dh] = out[3 + 2 * h].astype(o_ref.dtype)


def _sb_past_kernel(into_ref, q_ref, k_ref, v_ref, pk_hbm, pv_hbm, o_ref, rm_ref, r_ref, acc_ref, pk_ref, pv_ref, sem,
                    *, hb, dh, tk, last, scale, single):
    s, hg, pb = pl.program_id(0), pl.program_id(1), pl.program_id(2)
    tq = q_ref.shape[0]
    rows = pl.ds((last - pb) * tk, tk)
    copies = []
    for h in range(hb):
        copies.append(pltpu.make_async_copy(pk_hbm.at[s, rows, hg * hb + h, :], pk_ref.at[h], sem.at[0, h]))
        copies.append(pltpu.make_async_copy(pv_hbm.at[s, rows, hg * hb + h, :], pv_ref.at[h], sem.at[1, h]))
    for cp in copies:
        cp.start()

    def cols(h):
        return slice(h * dh, (h + 1) * dh)

    def own(h):
        return _sb_block(q_ref[:, cols(h)], k_ref[:, cols(h)].astype(BF16), v_ref[:, cols(h)].astype(BF16),
                         jnp.zeros((tq, 1), F32), jnp.zeros((tq, dh), F32), scale, True)

    if single:
        state = [own(h) for h in range(hb)]
        for cp in copies:
            cp.wait()
        rmax = None
        for h in range(hb):
            r_run, acc = _sb_block(q_ref[:, cols(h)], pk_ref[h].astype(BF16), pv_ref[h].astype(BF16),
                                   state[h][0], state[h][1], scale, False)
            o_ref[:, cols(h)] = acc.astype(o_ref.dtype)
            rmax = r_run if rmax is None else jnp.maximum(rmax, r_run)
        rm_ref[...] = jnp.broadcast_to(jnp.max(rmax), rm_ref.shape)
        return

    @pl.when(pb == 0)
    def _():
        for h in range(hb):
            r_run, acc = own(h)
            r_ref[h] = jnp.broadcast_to(r_run, (tq, 128))
            acc_ref[:, cols(h)] = acc

    for cp in copies:
        cp.wait()
    rmax = jnp.max(r_ref[...])

    @pl.when(rmax > EXP_ZERO_BELOW)
    def _():
        for h in range(hb):
            r_run, acc = _sb_block(q_ref[:, cols(h)], pk_ref[h].astype(BF16), pv_ref[h].astype(BF16),
                                   r_ref[h][:, 0:1], acc_ref[:, cols(h)], scale, False)
            r_ref[h] = jnp.broadcast_to(r_run, (tq, 128))
            acc_ref[:, cols(h)] = acc

    @pl.when(pb == pl.num_programs(2) - 1)
    def _():
        o_ref[...] = acc_ref[...].astype(o_ref.dtype)
        rm_ref[...] = jnp.broadcast_to(jnp.max(r_ref[...]), rm_ref.shape)


def _sb_prompt(q, k, v, *, n_seq, seq_len, dh, tq=256, hb=4):
    d = q.shape[1]
    nq = seq_len // tq
    assert seq_len % tq == 0 and d % (hb * dh) == 0
    hw = hb * dh
    kv = pl.BlockSpec((seq_len, hw), lambda b, h, i: (b, h))
    return pl.pallas_call(
        functools.partial(_sb_kernel, hb=hb, dh=dh, scale=dh ** -0.5),
        grid=(n_seq, d // hw, nq),
        in_specs=[pl.BlockSpec((tq, hw), lambda b, h, i: (b * nq + i, h)), kv, kv],
        out_specs=pl.BlockSpec((tq, hw), lambda b, h, i: (b * nq + i, h)),
        out_shape=jax.ShapeDtypeStruct(q.shape, BF16),
        compiler_params=_params("arbitrary", "arbitrary", "arbitrary"),
        name="sb_prompt",
    )(q, k, v)


def _sb_past(att_into, q, k, v, past_k, past_v, *, row0, seq_len, n_blocks, tk, hb=8):
    d = q.shape[1]
    n_seq, past_len, n_h, dh = past_k.shape
    hb = min(hb, n_h)
    assert past_len % tk == 0 and row0 % seq_len == 0 and n_h % hb == 0
    hw = hb * dh
    ng = n_h // hb
    rb0 = row0 // seq_len
    last = past_len // tk - 1
    new = pl.BlockSpec((seq_len, hw), lambda s, h, b: (rb0 + s, h))
    new_kv = pl.BlockSpec((seq_len, hw), lambda s, h, b: (s, h))
    hbm = pl.BlockSpec(memory_space=pl.ANY)
    return pl.pallas_call(
        functools.partial(_sb_past_kernel, hb=hb, dh=dh, tk=tk, last=last, scale=dh ** -0.5, single=n_blocks == 1),
        grid=(n_seq, ng, n_blocks),
        in_specs=[hbm, new, new_kv, new_kv, hbm, hbm],
        out_specs=[new, pl.BlockSpec((None, 8, 128), lambda s, h, b: (s * ng + h, 0, 0))],
        out_shape=[jax.ShapeDtypeStruct(q.shape, BF16), jax.ShapeDtypeStruct((n_seq * ng, 8, 128), F32)],
        scratch_shapes=[pltpu.VMEM((hb, seq_len, 128), F32), pltpu.VMEM((seq_len, hw), F32),
                        pltpu.VMEM((hb, tk, dh), F32), pltpu.VMEM((hb, tk, dh), F32),
                        pltpu.SemaphoreType.DMA((2, hb))],
        input_output_aliases={0: 0},
        compiler_params=_params("arbitrary", "arbitrary", "arbitrary"),
        name="sb_past",
    )(att_into, q, k, v, past_k, past_v)


def _sb_sample(att_into, q, k, v, past_k, past_v, *, row0, seq_len, tk=256):
    past_len = past_k.shape[1]
    tk = min(tk, past_len)
    nblk = past_len // tk
    kw = dict(row0=row0, seq_len=seq_len, tk=tk)
    att, rm = _sb_past(att_into, q, k, v, past_k, past_v, n_blocks=1, **kw)
    if nblk == 1:
        return att
    return lax.cond(jnp.max(rm) > EXP_ZERO_BELOW,
                    lambda a: _sb_past(a, q, k, v, past_k, past_v, n_blocks=nblk, **kw)[0],
                    lambda a: a, att)


def _gather_rows_kernel(tok_ref, nused_ref, src_hbm, o_ref, buf_ref, sem, *, tm):
    b = pl.program_id(0)
    n_used = nused_ref[0]

    def issue(blk, slot):
        for j in range(tm):
            tok = tok_ref[blk * tm + j]
            pltpu.make_async_copy(src_hbm.at[tok // 8, tok % 8], buf_ref.at[slot, j // 8, j % 8],
                                  sem.at[slot]).start(priority=j % 2)

    @pl.when(jnp.logical_and(b == 0, n_used > 0))
    def _():
        issue(0, 0)

    @pl.when(b + 1 < n_used)
    def _():
        issue(b + 1, (b + 1) % 2)

    @pl.when(b < n_used)
    def _():
        slot = b % 2
        pltpu.make_async_copy(src_hbm.at[pl.ds(0, tm // 8)], buf_ref.at[slot], sem.at[slot]).wait()
        o_ref[...] = buf_ref[slot].reshape(o_ref.shape).astype(o_ref.dtype)


def _gather_rows(tok, n_used, x, *, tm):
    t, d = x.shape
    n = tok.shape[0]
    assert t % 8 == 0 and tm % 8 == 0 and n % tm == 0
    return pl.pallas_call(
        functools.partial(_gather_rows_kernel, tm=tm),
        grid_spec=pltpu.PrefetchScalarGridSpec(
            num_scalar_prefetch=2, grid=(n // tm,),
            in_specs=[pl.BlockSpec(memory_space=pl.ANY)],
            out_specs=pl.BlockSpec((tm, d), lambda b, tok, nu: (b, 0)),
            scratch_shapes=[pltpu.VMEM((2, tm // 8, 8, d), F32), pltpu.SemaphoreType.DMA((2,))]),
        out_shape=jax.ShapeDtypeStruct((n, d), BF16),
        compiler_params=_params("arbitrary"),
        name="gather_rows",
    )(tok, n_used, x.reshape(t // 8, 8, d))


def _expert_mm_kernel(start_ref, cnt_ref, x_hbm, *rest, n_w, tm, tn, layer, n_groups, groups_per_block):
    w_hbms, o_hbm = rest[:n_w], rest[n_w]
    wbuf = rest[n_w + 1]
    wb_refs = rest[n_w + 2:2 * n_w + 2]
    xbuf, obuf, sem_w, sem_in, sem_out = rest[2 * n_w + 2:]
    e, c = pl.program_id(0), pl.program_id(1)
    n_c = pl.num_programs(1)
    step = e * n_c + c
    wslot = step % 2
    has_next = step + 1 < pl.num_programs(0) * n_c
    e2 = jnp.minimum((step + 1) // n_c, pl.num_programs(0) - 1)
    c2 = (step + 1) % n_c
    n = cnt_ref[e]
    b0 = start_ref[e]
    kg = wbuf.shape[2] // n_groups

    def in_copy(first_block, r, slot):
        return pltpu.make_async_copy(x_hbm.at[pl.ds((first_block + r) * tm, tm)], xbuf.at[slot], sem_in.at[slot])

    def out_copy(r, slot):
        return pltpu.make_async_copy(obuf.at[slot], o_hbm.at[pl.ds((b0 + r) * tm, tm), pl.ds(c * tn, tn)],
                                     sem_out.at[slot])

    def fetch_group(e_, c_, slot_, g):
        for j in range(n_w):
            pltpu.make_async_copy(w_hbms[j].at[layer, e_, pl.ds(g * kg, kg), pl.ds(c_ * tn, tn)],
                                  wbuf.at[slot_, j, pl.ds(g * kg, kg)], sem_w.at[slot_, j]).start()

    @pl.when(step == 0)
    def _():
        lax.fori_loop(0, n_groups, lambda g, carry: (fetch_group(e, c, wslot, g), carry)[1], 0)

        @pl.when(n > 0)
        def _():
            in_copy(b0, 0, 0).start()

    for j in range(n_w):
        pltpu.make_async_copy(w_hbms[j].at[layer, e, :, pl.ds(c * tn, tn)], wbuf.at[wslot, j], sem_w.at[wslot, j]).wait()
        wb_refs[j][...] = wbuf[wslot, j].astype(BF16)

    def block(r, carry):
        slot = r % 2

        @pl.when(r + 1 < n)
        def _():
            in_copy(b0, r + 1, 1 - slot).start()

        for k in range(groups_per_block):
            g = r * groups_per_block + k

            @pl.when(jnp.logical_and(has_next, g < n_groups))
            def _():
                fetch_group(e2, c2, 1 - wslot, g)

        in_copy(b0, r, slot).wait()

        @pl.when(r >= 2)
        def _():
            out_copy(r - 2, slot).wait()

        x = xbuf[slot]
        res = jnp.dot(x, wb_refs[0][...], preferred_element_type=F32)
        if n_w == 2:
            res = res * jax.nn.sigmoid(res) * jnp.dot(x, wb_refs[1][...], preferred_element_type=F32)
        obuf[slot] = res.astype(obuf.dtype)
        out_copy(r, slot).start()
        return carry

    lax.fori_loop(0, n, block, 0)

    @pl.when(has_next)
    def _():
        first = jnp.minimum(n * groups_per_block, n_groups)
        lax.fori_loop(first, n_groups, lambda g, carry: (fetch_group(e2, c2, 1 - wslot, g), carry)[1], 0)

        @pl.when(cnt_ref[e2] > 0)
        def _():
            in_copy(start_ref[e2], 0, 0).start()

    @pl.when(n >= 2)
    def _():
        out_copy(n - 2, n % 2).wait()

    @pl.when(n >= 1)
    def _():
        out_copy(n - 1, (n + 1) % 2).wait()


def _expert_mm(x, ws, layer, blk_start, blk_cnt, *, tm, tn, out_dtype, groups_per_block, name, n_groups=8):
    pr, k = x.shape
    n_e, n = ws[0].shape[1], ws[0].shape[-1]
    tn = min(tn, n)
    n_w = len(ws)
    assert k % (8 * n_groups) == 0
    hbm = pl.BlockSpec(memory_space=pl.ANY)
    return pl.pallas_call(
        functools.partial(_expert_mm_kernel, n_w=n_w, tm=tm, tn=tn, layer=layer, n_groups=n_groups,
                          groups_per_block=groups_per_block),
        grid_spec=pltpu.PrefetchScalarGridSpec(
            num_scalar_prefetch=2, grid=(n_e, n // tn),
            in_specs=[hbm] * (1 + n_w),
            out_specs=hbm,
            scratch_shapes=[pltpu.VMEM((2, n_w, k, tn), F32)] + [pltpu.VMEM((k, tn), BF16)] * n_w + [
                pltpu.VMEM((2, tm, k), BF16), pltpu.VMEM((2, tm, tn), out_dtype),
                pltpu.SemaphoreType.DMA((2, n_w)), pltpu.SemaphoreType.DMA((2,)), pltpu.SemaphoreType.DMA((2,))]),
        out_shape=jax.ShapeDtypeStruct((pr, n), out_dtype),
        compiler_params=_params("arbitrary", "arbitrary"),
        name=name,
    )(blk_start, blk_cnt, x, *ws)


def _route_kernel(s_ref, rb_ref, idx_ref, gate_ref, rank_ref, cnt_ref, carry_ref, *, n_groups):
    @pl.when(pl.program_id(0) == 0)
    def _():
        carry_ref[...] = jnp.zeros(carry_ref.shape, F32)

    s = s_ref[...]
    tr, e = s.shape
    per = e // n_groups
    sel = s + rb_ref[...]
    lane_i = lax.broadcasted_iota(jnp.int32, (tr, e), 1)
    lane = lane_i.astype(F32)
    grp = (lane_i // per).astype(F32)
    neg = -jnp.inf

    def top1(v):
        m = jnp.max(v, axis=1, keepdims=True)
        return m, jnp.min(jnp.where(v == m, lane, float(e)), axis=1, keepdims=True)

    best = gi = None
    for g in range(n_groups):
        vg = jnp.where(grp == float(g), sel, neg)
        m1, i1 = top1(vg)
        m2, _ = top1(jnp.where(lane == i1, neg, vg))
        score = m1 + m2
        if best is None:
            best, gi = score, jnp.zeros_like(score)
        else:
            better = score > best
            gi = jnp.where(better, float(g), gi)
            best = jnp.where(better, score, best)
    vc = jnp.where(grp == gi, sel, neg)
    _, e1 = top1(vc)
    _, e2 = top1(jnp.where(lane == e1, neg, vc))
    oh1, oh2 = lane == e1, lane == e2
    g1 = jnp.sum(jnp.where(oh1, s, 0.0), axis=1, keepdims=True)
    g2 = jnp.sum(jnp.where(oh2, s, 0.0), axis=1, keepdims=True)
    den = g1 + g2

    oh = jnp.logical_or(oh1, oh2)
    earlier = (lax.broadcasted_iota(jnp.int32, (tr, tr), 1) < lax.broadcasted_iota(jnp.int32, (tr, tr), 0))
    before = jnp.dot(earlier.astype(BF16), oh.astype(BF16), preferred_element_type=F32) + carry_ref[...]
    r1 = jnp.sum(jnp.where(oh1, before, 0.0), axis=1, keepdims=True)
    r2 = jnp.sum(jnp.where(oh2, before, 0.0), axis=1, keepdims=True)
    carry = carry_ref[...] + jnp.sum(oh.astype(F32), axis=0, keepdims=True)
    carry_ref[...] = carry
    cnt_ref[...] = carry.astype(jnp.int32)

    first = lax.broadcasted_iota(jnp.int32, (tr, TOP_K), 1) == 0
    idx_ref[...] = jnp.where(first, e1, e2).astype(jnp.int32)
    gate_ref[...] = jnp.where(first, g1, g2) / den
    rank_ref[...] = jnp.where(first, r1, r2).astype(jnp.int32)


def _route(s, router_b, tr=512):
    t, e = s.shape
    tr = _pick_tile(t, tr)
    col = pl.BlockSpec((tr, TOP_K), lambda i: (i, 0))
    return pl.pallas_call(
        functools.partial(_route_kernel, n_groups=N_EXPERT_GROUPS),
        grid=(t // tr,),
        in_specs=[pl.BlockSpec((tr, e), lambda i: (i, 0)), pl.BlockSpec((1, e), lambda i: (0, 0))],
        out_specs=[col, col, col, pl.BlockSpec((1, e), lambda i: (0, 0))],
        out_shape=[jax.ShapeDtypeStruct((t, TOP_K), jnp.int32), jax.ShapeDtypeStruct((t, TOP_K), F32),
                   jax.ShapeDtypeStruct((t, TOP_K), jnp.int32), jax.ShapeDtypeStruct((1, e), jnp.int32)],
        scratch_shapes=[pltpu.VMEM((1, e), F32)],
        compiler_params=_params("arbitrary"),
        name="route",
    )(s, router_b.reshape(1, e).astype(F32))


def _moe(h, s, router_b, w_gate, w_up, w_down, layer, *, tm=256):
    t, d = h.shape
    e = s.shape[1]
    idx, gates, rank, counts = _route(s, router_b)
    counts = counts.reshape(e)
    padded = (counts + tm - 1) // tm * tm
    pad_ends = jnp.cumsum(padded)
    pad_starts = pad_ends - padded
    slot = pad_starts[idx] + rank
    nb = -(-(t * TOP_K) // tm) + e
    tok = jnp.broadcast_to(jnp.arange(t, dtype=jnp.int32)[:, None], (t, TOP_K))
    tok_of_slot = jnp.zeros((nb * tm,), jnp.int32).at[slot.reshape(-1)].set(tok.reshape(-1))
    xs = _gather_rows(tok_of_slot, (pad_ends[-1:] // tm).astype(jnp.int32), h, tm=tm)
    blk_start, blk_cnt = (pad_starts // tm).astype(jnp.int32), (padded // tm).astype(jnp.int32)
    hid = _expert_mm(xs, [w_gate, w_up], layer, blk_start, blk_cnt, tm=tm, tn=512, out_dtype=BF16,
                     groups_per_block=3, name="moe_up")
    yb = _expert_mm(hid, [w_down], layer, blk_start, blk_cnt, tm=tm, tn=1024, out_dtype=F32,
                    groups_per_block=4, name="moe_down")
    return [yb[slot[:, k]] for k in range(TOP_K)], [gates[:, k:k + 1] for k in range(TOP_K)]


def kernel(x_prompt, x_sample, c_prompt, c_sample, state_conv, state_ssm, cache_k, cache_v, w_ada, b_ada, ln_g, ln_b, ssm_w_in, ssm_conv_w, ssm_conv_b, ssm_dt_bias, ssm_a_log, ssm_d, ssm_norm_g, ssm_w_out, sb_w_k, sb_w_v, sb_w_q, sb_w_o, router_w, router_b, moe_w_gate, moe_w_up, moe_w_down):
    bp, lp, d = x_prompt.shape
    bs, ls_, _ = x_sample.shape
    assert ls_ == ROWS and lp % ROWS == 0 and w_ada.shape[0] == DEPTH == 2
    n_heads, p, n_state = state_ssm.shape[2:]
    inner = n_heads * p
    conv_dim = ssm_conv_w.shape[-1]
    kw = ssm_conv_w.shape[1]
    n_groups = (conv_dim - inner) // (2 * n_state)
    sb_heads, dh = cache_k.shape[2:]
    past_len = cache_k.shape[1]
    tp, ts = bp * lp, bs * ls_
    t = tp + ts

    x = jnp.concatenate([x_prompt.reshape(tp, d), x_sample.reshape(ts, d)], axis=0)
    c_all = jnp.concatenate([c_prompt, c_sample], axis=0)
    gmap = jnp.concatenate([jnp.repeat(jnp.arange(bp), lp // ROWS), bp + jnp.arange(bs)])
    mod = _ada(c_all, w_ada, b_ada)
    modg = [mod[l][gmap][:, None, :] for l in range(DEPTH)]

    h = _modulate(x, modg[0], 1, 0)
    w_in = ssm_w_in[0]
    n_zx = inner + conv_dim
    zx = _matmul(h, w_in, n_out=n_zx, name="in_proj")
    dt_raw = _matmul(h, w_in, n_off=n_zx // 128, n_out=n_heads, tn=128, name="in_proj_dt")
    hist8 = jnp.pad(state_conv[0], ((0, 0), (8 - (kw - 1), 0), (0, 0)))
    act_p = _conv_silu(zx, inner, ssm_conv_w[0], ssm_conv_b[0], row0=0, n_seq=bp, seq_len=lp, hist8=None)
    act_s = _conv_silu(zx, inner, ssm_conv_w[0], ssm_conv_b[0], row0=tp, n_seq=bs, seq_len=ls_, hist8=hist8)
    ssd_kw = dict(n_heads=n_heads, p=p, n_state=n_state, n_groups=n_groups)
    y, hfin_p = _ssd(act_p, zx, dt_raw, ssm_dt_bias[0], ssm_a_log[0], ssm_d[0], ssm_norm_g[0], None,
                     row0=0, act_row0=0, n_seq=bp, seq_len=lp, lc=min(SSD_CHUNK, lp), **ssd_kw)
    y, hfin_s = _ssd(act_s, zx, dt_raw, ssm_dt_bias[0], ssm_a_log[0], ssm_d[0], ssm_norm_g[0],
                     state_ssm[0].reshape(bs, inner, n_state),
                     row0=tp, act_row0=0, n_seq=bs, seq_len=ls_, lc=min(SSD_CHUNK, ls_), y_into=y, **ssd_kw)
    kh = inner // 2
    m = _matmul(y, ssm_w_out[0], kblk=kh, x_koff=0, w_koff=0, name="out_proj0")
    m = _matmul(y, ssm_w_out[0], kblk=kh, x_koff=1, w_koff=1, acc=m, name="out_proj1")

    def last_rows(row0, n_seq, seq_len):
        return jnp.stack([lax.slice(zx, (row0 + (i + 1) * seq_len - (kw - 1), inner), (row0 + (i + 1) * seq_len, n_zx))
                          for i in range(n_seq)])[None]

    conv_p, conv_s = last_rows(0, bp, lp), last_rows(tp, bs, ls_)
    ssm_p = hfin_p.reshape(1, bp, n_heads, p, n_state)
    ssm_s = hfin_s.reshape(1, bs, n_heads, p, n_state)

    x, h, s = _ln(x, [m], (modg[0], 2), ln_g[0, 0], ln_b[0, 0], next_mod=(modg[0], 4, 3), h_dtype=F32,
                  router_w=router_w)
    ys, gs = _moe(h, s, router_b, moe_w_gate, moe_w_up, moe_w_down, 0)
    x, h, xb = _ln(x, ys, (modg[0], 5), ln_g[0, 1], ln_b[0, 1], m_gates=gs, next_mod=(modg[1], 1, 0), with_xb=True)

    k_p, k_s = _matmul(xb, sb_w_k, split_rows=tp, name="k_proj")
    v_p, v_s = _matmul(xb, sb_w_v, split_rows=tp, name="v_proj")

    q = _matmul(h, sb_w_q, lead=(0,), out_dtype=BF16, name="q_proj")
    att = _sb_prompt(q, k_p, v_p, n_seq=bp, seq_len=lp, dh=dh)
    att = _sb_sample(att, q, k_s, v_s, cache_k, cache_v, row0=tp, seq_len=ls_)
    m = _matmul(att, sb_w_o, lead=(0,), name="o_proj")

    x, h, s = _ln(x, [m], (modg[1], 2), ln_g[1, 0], ln_b[1, 0], next_mod=(modg[1], 4, 3), h_dtype=F32,
                  router_w=router_w)
    ys, gs = _moe(h, s, router_b, moe_w_gate, moe_w_up, moe_w_down, 1)
    (y_p,) = _ln(x, ys, (modg[1], 5), ln_g[1, 1], ln_b[1, 1], m_gates=gs, rows=(0, tp))
    (y_s,) = _ln(x, ys, (modg[1], 5), ln_g[1, 1], ln_b[1, 1], m_gates=gs, rows=(tp, ts))

    def heads(a, nb_, l):
        return a.reshape(nb_, l, sb_heads, dh)

    return (y_p.reshape(bp, lp, d), y_s.reshape(bs, ls_, d), conv_p, ssm_p,
            heads(k_p, bp, lp), heads(v_p, bp, lp), conv_s, ssm_s, heads(k_s, bs, ls_), heads(v_s, bs, ls_))
```

```python
import functools
import math

import jax
import jax.numpy as jnp
from jax import lax
from jax.experimental import pallas as pl
from jax.experimental.pallas import tpu as pltpu

F32 = jnp.float32
BF16 = jnp.bfloat16
HIGHEST = lax.Precision.HIGHEST

DEPTH = 2
N_EXPERT_GROUPS = 4
TOP_K = 2
DN_ALPHA = (2 * DEPTH) ** 0.25
LN_EPS = 1e-5
RMS_EPS = 1e-5
SSD_CHUNK = 128

ROWS = 32
VMEM_LIMIT = 52 * 1024 * 1024
EXP_ZERO_BELOW = -105.0


def _params(*sem):
    return pltpu.CompilerParams(dimension_semantics=sem, vmem_limit_bytes=VMEM_LIMIT)


def _split3(x):
    hi = x.astype(BF16)
    r1 = x - hi.astype(F32)
    mid = r1.astype(BF16)
    lo = (r1 - mid.astype(F32)).astype(BF16)
    return hi, mid, lo


def _dot_f32_01(x, m01):
    hi, mid, lo = _split3(x)
    d = functools.partial(jnp.dot, preferred_element_type=F32)
    return (d(lo, m01) + d(mid, m01)) + d(hi, m01)


def _dot_01_f32(m01, x):
    hi, mid, lo = _split3(x)
    d = functools.partial(jnp.dot, preferred_element_type=F32)
    return (d(m01, lo) + d(m01, mid)) + d(m01, hi)


def _pick_tile(n, cap, mult=128):
    best = None
    for c in range(mult, min(n, cap) + 1, mult):
        if n % c == 0:
            best = c
    assert best is not None, (n, cap)
    return best


def _ada_kernel(c_ref, w_ref, b_ref, o_ref):
    c = c_ref[...]
    s = (c * jax.nn.sigmoid(c)).astype(BF16)
    o_ref[...] = jnp.dot(s, w_ref[...].astype(BF16), preferred_element_type=F32) + b_ref[...]


def _ada(c_all, w_ada, b_ada):
    depth, d, n = w_ada.shape
    bc = c_all.shape[0]
    tn = 512
    return pl.pallas_call(
        _ada_kernel,
        grid=(depth, n // tn),
        in_specs=[pl.BlockSpec((bc, d), lambda l, j: (0, 0)),
                  pl.BlockSpec((None, d, tn), lambda l, j: (l, 0, j)),
                  pl.BlockSpec((None, 1, tn), lambda l, j: (l, 0, j))],
        out_specs=pl.BlockSpec((None, bc, tn), lambda l, j: (l, 0, j)),
        out_shape=jax.ShapeDtypeStruct((depth, bc, n), F32),
        compiler_params=_params("arbitrary", "arbitrary"),
        name="ada",
    )(c_all, w_ada, b_ada.reshape(depth, 1, n))


def _mod_kernel(x_ref, sc_ref, sh_ref, h_ref):
    gb = sc_ref.shape[0]
    rows, d = x_ref.shape
    x = x_ref[...].reshape(gb, rows // gb, d)
    h = x * (1.0 + sc_ref[...]) + sh_ref[...]
    h_ref[...] = h.reshape(rows, d).astype(h_ref.dtype)


def _modulate(x, modg, scale_blk, shift_blk, gb=8):
    t, d = x.shape
    g = modg.shape[0]
    return pl.pallas_call(
        _mod_kernel,
        grid=(g // gb,),
        in_specs=[pl.BlockSpec((gb * ROWS, d), lambda i: (i, 0)),
                  pl.BlockSpec((gb, 1, d), lambda i: (i, 0, scale_blk)),
                  pl.BlockSpec((gb, 1, d), lambda i: (i, 0, shift_blk))],
        out_specs=pl.BlockSpec((gb * ROWS, d), lambda i: (i, 0)),
        out_shape=jax.ShapeDtypeStruct((t, d), BF16),
        compiler_params=_params("arbitrary"),
        name="modulate",
    )(x, modg, modg)


def _ln_kernel(*refs, n_m, gated_m, with_h, with_router, with_xb):
    it = iter(refs)
    x_ref = next(it)
    m_refs = [next(it) for _ in range(n_m)]
    mg_refs = [next(it) for _ in range(n_m)] if gated_m else []
    gate_ref, g_ref, b_ref = next(it), next(it), next(it)
    sc_ref = sh_ref = rw_ref = None
    if with_h:
        sc_ref, sh_ref = next(it), next(it)
    if with_router:
        rw_ref = next(it)
    xo_ref = next(it)
    h_ref = next(it) if with_h else None
    s_ref = next(it) if with_router else None
    xb_ref = next(it) if with_xb else None

    gb = gate_ref.shape[0]
    rows, d = x_ref.shape
    if gated_m:
        m = m_refs[0][...] * mg_refs[0][...]
        for mr, gr in zip(m_refs[1:], mg_refs[1:]):
            m = m + mr[...] * gr[...]
    else:
        m = m_refs[0][...]
        for mr in m_refs[1:]:
            m = m + mr[...]
    v = DN_ALPHA * x_ref[...] + (gate_ref[...] * m.reshape(gb, rows // gb, d)).reshape(rows, d)
    mu = jnp.mean(v, axis=-1, keepdims=True)
    vc = v - mu
    var = jnp.mean(vc * vc, axis=-1, keepdims=True)
    xn = vc * lax.rsqrt(var + LN_EPS) * g_ref[...] + b_ref[...]
    xo_ref[...] = xn
    if with_xb:
        xb_ref[...] = xn.astype(BF16)
    if with_h:
        h = xn.reshape(gb, rows // gb, d) * (1.0 + sc_ref[...]) + sh_ref[...]
        h = h.reshape(rows, d)
        if h_ref.dtype == jnp.uint32:
            bits = lax.bitcast_convert_type(h.astype(BF16).astype(F32), jnp.uint32)
            h_ref[...] = (bits[:, :d // 2] >> 16) | (bits[:, d // 2:] & jnp.uint32(0xFFFF0000))
        else:
            h_ref[...] = h.astype(h_ref.dtype)
        if with_router:
            logits = jnp.dot(h, rw_ref[...], preferred_element_type=F32, precision=HIGHEST)
            s_ref[...] = jax.nn.sigmoid(logits)


def _ln(x, ms, gate_src, ln_g, ln_b, *, m_gates=None, next_mod=None, h_dtype=BF16, router_w=None, with_xb=False,
        rows=None, gb=8):
    d = x.shape[1]
    modg, gate_blk = gate_src
    tr = gb * ROWS
    r0, t = (0, x.shape[0]) if rows is None else rows
    assert r0 % tr == 0 and t % tr == 0
    off = r0 // tr
    row = pl.BlockSpec((tr, d), lambda i: (i + off, 0))
    out_row = pl.BlockSpec((tr, d), lambda i: (i, 0))
    vec = pl.BlockSpec((1, d), lambda i: (0, 0))

    def modspec(blk):
        return pl.BlockSpec((gb, 1, d), lambda i: (i + off, 0, blk))

    args = [x] + list(ms)
    specs = [row] + [row] * len(ms)
    if m_gates is not None:
        args += list(m_gates)
        specs += [pl.BlockSpec((tr, 1), lambda i: (i + off, 0))] * len(ms)
    args += [modg, ln_g.reshape(1, d), ln_b.reshape(1, d)]
    specs += [modspec(gate_blk), vec, vec]
    outs = [jax.ShapeDtypeStruct((t, d), F32)]
    out_specs = [out_row]
    row = out_row
    if next_mod is not None:
        nm, sc_blk, sh_blk = next_mod
        args += [nm, nm]
        specs += [modspec(sc_blk), modspec(sh_blk)]
        hd = d // 2 if h_dtype == jnp.uint32 else d
        outs.append(jax.ShapeDtypeStruct((t, hd), h_dtype))
        out_specs.append(pl.BlockSpec((tr, hd), lambda i: (i, 0)))
    if router_w is not None:
        e = router_w.shape[1]
        args.append(router_w)
        specs.append(pl.BlockSpec((d, e), lambda i: (0, 0)))
        outs.append(jax.ShapeDtypeStruct((t, e), F32))
        out_specs.append(pl.BlockSpec((tr, e), lambda i: (i, 0)))
    if with_xb:
        outs.append(jax.ShapeDtypeStruct((t, d), BF16))
        out_specs.append(row)
    kern = functools.partial(_ln_kernel, n_m=len(ms), gated_m=m_gates is not None,
                             with_h=next_mod is not None, with_router=router_w is not None, with_xb=with_xb)
    return pl.pallas_call(
        kern, grid=(t // tr,), in_specs=specs, out_specs=out_specs, out_shape=outs,
        compiler_params=_params("arbitrary"), name="ln",
    )(*args)


def _mm_kernel(*refs, with_acc, n_first):
    x_ref, w_ref = refs[:2]
    a_ref = refs[2] if with_acc else None
    o_refs, wb_ref = refs[2 + with_acc:-1], refs[-1]

    @pl.when(pl.program_id(1) == 0)
    def _():
        wb_ref[...] = w_ref[...].astype(BF16)

    r = jnp.dot(x_ref[...], wb_ref[...], preferred_element_type=F32)
    if with_acc:
        r = r + a_ref[...]
    if n_first is None:
        o_refs[0][...] = r.astype(o_refs[0].dtype)
    else:
        @pl.when(pl.program_id(1) < n_first)
        def _():
            o_refs[0][...] = r.astype(o_refs[0].dtype)

        @pl.when(pl.program_id(1) >= n_first)
        def _():
            o_refs[1][...] = r.astype(o_refs[1].dtype)


def _matmul(x, w, *, lead=(), kblk=None, x_koff=0, w_koff=0, n_off=0, n_out=None, tm=1024, tn=512,
            out_dtype=F32, acc=None, split_rows=None, name="matmul"):
    t = x.shape[0]
    kfull, nfull = w.shape[-2:]
    kblk = kfull if kblk is None else kblk
    n_out = nfull if n_out is None else n_out
    tm = _pick_tile(t if split_rows is None else math.gcd(t, split_rows), tm)
    assert n_out % tn == 0
    nl = len(lead)
    in_specs = [pl.BlockSpec((tm, kblk), lambda j, i: (i, x_koff)),
                pl.BlockSpec((None,) * nl + (kblk, tn), lambda j, i: tuple(lead) + (w_koff, j + n_off))]
    args = [x, w]
    if acc is not None:
        in_specs.append(pl.BlockSpec((tm, tn), lambda j, i: (i, j)))
        args.append(acc)
    if split_rows is None:
        n_first = None
        out_specs = pl.BlockSpec((tm, tn), lambda j, i: (i, j))
        out_shape = jax.ShapeDtypeStruct((t, n_out), out_dtype)
    else:
        assert split_rows % tm == 0 and 0 < split_rows < t
        n_first = split_rows // tm
        out_specs = [pl.BlockSpec((tm, tn), lambda j, i: (jnp.minimum(i, n_first - 1), j)),
                     pl.BlockSpec((tm, tn), lambda j, i: (jnp.maximum(i - n_first, 0), j))]
        out_shape = [jax.ShapeDtypeStruct((split_rows, n_out), out_dtype),
                     jax.ShapeDtypeStruct((t - split_rows, n_out), out_dtype)]
    return pl.pallas_call(
        functools.partial(_mm_kernel, with_acc=acc is not None, n_first=n_first),
        grid=(n_out // tn, t // tm),
        in_specs=in_specs,
        out_specs=out_specs,
        out_shape=out_shape,
        scratch_shapes=[pltpu.VMEM((kblk, tn), BF16)],
        compiler_params=_params("arbitrary", "arbitrary"),
        name=name,
    )(*args)


def _conv_kernel(*refs, with_hist, kw):
    if with_hist:
        x_ref, hist_ref, w_ref, b_ref, o_ref, xp_ref = refs
        gs = hist_ref.shape[0]
    else:
        x_ref, w_ref, b_ref, o_ref, xp_ref = refs
        gs = 1
    tr = x_ref.shape[0] // gs
    w = w_ref[...]
    if not with_hist:
        @pl.when(pl.program_id(2) == 0)
        def _():
            xp_ref[0:8, :] = jnp.zeros((8, xp_ref.shape[1]), F32)
    for s in range(gs):
        rows = slice(s * tr, (s + 1) * tr)
        if with_hist:
            xp_ref[0:8, :] = hist_ref[s]
        xp_ref[8:8 + tr, :] = x_ref[rows, :]
        y = b_ref[...] + w[kw - 1:kw, :] * x_ref[rows, :]
        for k in range(1, kw):
            y = y + w[kw - 1 - k:kw - k, :] * xp_ref[8 - k:8 - k + tr, :]
        o_ref[rows, :] = (y * jax.nn.sigmoid(y)).astype(o_ref.dtype)
    if not with_hist:
        xp_ref[0:8, :] = xp_ref[tr:tr + 8, :]


def _conv_silu(zx, col0, conv_w, conv_b, *, row0, n_seq, seq_len, hist8, tc=1024):
    kw, c = conv_w.shape
    tr = min(seq_len, 512)
    nrt = seq_len // tr
    assert seq_len % tr == 0 and c % tc == 0 and col0 % tc == 0 and row0 % tr == 0
    with_hist = hist8 is not None
    assert not with_hist or nrt == 1
    gs = 1
    if with_hist:
        gs = max(g for g in (1, 2, 4, 8) if n_seq % g == 0 and g * tr <= 512)
    br = gs * tr
    assert row0 % br == 0
    rb0, cb0 = row0 // br, col0 // tc
    in_specs = [pl.BlockSpec((br, tc), lambda s, j, r: (rb0 + s * nrt + r, cb0 + j))]
    args = [zx]
    if with_hist:
        in_specs.append(pl.BlockSpec((gs, 8, tc), lambda s, j, r: (s, 0, j)))
        args.append(hist8)
    in_specs += [pl.BlockSpec((kw, tc), lambda s, j, r: (0, j)),
                 pl.BlockSpec((1, tc), lambda s, j, r: (0, j))]
    args += [conv_w, conv_b.reshape(1, c)]
    return pl.pallas_call(
        functools.partial(_conv_kernel, with_hist=with_hist, kw=kw),
        grid=(n_seq // gs, c // tc, nrt),
        in_specs=in_specs,
        out_specs=pl.BlockSpec((br, tc), lambda s, j, r: (s * nrt + r, j)),
        out_shape=jax.ShapeDtypeStruct((n_seq * seq_len, c), BF16),
        scratch_shapes=[pltpu.VMEM((tr + 8, tc), F32)],
        compiler_params=_params("arbitrary", "arbitrary", "arbitrary"),
        name="conv_silu",
    )(*args)


def _ssd_kernel(*refs, with_h0, with_into, hpg, p):
    if with_into:
        refs = refs[1:]
    if with_h0:
        (x_ref, b_ref, c_ref, z_ref, dt_ref, dtb_ref, alog_ref, dsk_ref, ng_ref, h0_ref,
         y_ref, hf_ref, ht_ref, yd_ref) = refs
    else:
        (x_ref, b_ref, c_ref, z_ref, dt_ref, dtb_ref, alog_ref, dsk_ref, ng_ref,
         y_ref, hf_ref, ht_ref, yd_ref) = refs
        h0_ref = None
    g = pl.program_id(1)
    ci = pl.program_id(2)
    lc, gw = x_ref.shape
    nh = dt_ref.shape[1]

    @pl.when(ci == 0)
    def _():
        if with_h0:
            ht_ref[...] = h0_ref[...].T
        else:
            ht_ref[...] = jnp.zeros(ht_ref.shape, F32)

    dt = jax.nn.softplus(dt_ref[...] + dtb_ref[...])
    da = dt * (-jnp.exp(alog_ref[...]))
    r_i = lax.broadcasted_iota(jnp.int32, (lc, lc), 0)
    c_i = lax.broadcasted_iota(jnp.int32, (lc, lc), 1)
    tril = r_i >= c_i
    cs_h = _dot_01_f32(tril.astype(BF16), da)
    head_of_col = g * hpg + lax.broadcasted_iota(jnp.int32, (nh, gw), 1) // p
    expand = (lax.broadcasted_iota(jnp.int32, (nh, gw), 0) == head_of_col).astype(BF16)
    dt_e = _dot_f32_01(dt, expand)
    cs = _dot_f32_01(cs_h, expand)
    cs_last = cs[lc - 1:lc, :]
    eye = r_i == c_i

    xs = x_ref[...].astype(F32)
    xdt = xs * dt_e
    xdt_b = xdt.astype(BF16)
    bm = b_ref[...]
    cm = c_ref[...]
    cb = lax.dot_general(cm, bm, (((1,), (1,)), ((), ())), preferred_element_type=F32)

    for j in range(hpg):
        col = jnp.broadcast_to(cs[:, j * p:j * p + 1], (lc, lc))
        row = jnp.sum(jnp.where(eye, col, 0.0), axis=0, keepdims=True)
        dec = jnp.exp(jnp.where(tril, col - row, -jnp.inf))
        mj = (cb * dec).astype(BF16)
        yd_ref[:, j * p:(j + 1) * p] = jnp.dot(mj, xdt_b[:, j * p:(j + 1) * p], preferred_element_type=F32)

    ht = ht_ref[...]
    y_off = jnp.dot(cm, ht.astype(BF16), preferred_element_type=F32) * jnp.exp(cs)
    xw = (xdt * jnp.exp(cs_last - cs)).astype(BF16)
    st_t = lax.dot_general(bm, xw, (((0,), (0,)), ((), ())), preferred_element_type=F32)
    ht_new = jnp.exp(cs_last) * ht + st_t
    ht_ref[...] = ht_new

    @pl.when(ci == pl.num_programs(2) - 1)
    def _():
        hf_ref[...] = ht_new.T

    y = yd_ref[...] + y_off + dsk_ref[...] * xs
    z = z_ref[...]
    y = y * (z * jax.nn.sigmoid(z))
    ms = jnp.mean(y * y, axis=-1, keepdims=True)
    y_ref[...] = (y * lax.rsqrt(ms + RMS_EPS) * ng_ref[...]).astype(y_ref.dtype)


def _ssd(act, zx, dt_raw, dt_bias, a_log, d_skip, norm_g, h0, *, row0, act_row0, n_seq, seq_len, lc,
         n_heads, p, n_state, n_groups, y_into=None):
    inner = n_heads * p
    hpg = n_heads // n_groups
    gw = hpg * p
    nc = seq_len // lc
    assert seq_len % lc == 0 and row0 % lc == 0 and act_row0 % lc == 0 and gw % 128 == 0 and n_state % 128 == 0
    rb0, ab0 = row0 // lc, act_row0 // lc
    bcol = inner // n_state
    with_h0 = h0 is not None
    d_e = jnp.repeat(d_skip.astype(F32), p).reshape(1, inner)
    in_specs = [
        pl.BlockSpec((lc, gw), lambda s, g, c: (ab0 + s * nc + c, g)),
        pl.BlockSpec((lc, n_state), lambda s, g, c: (ab0 + s * nc + c, bcol + g)),
        pl.BlockSpec((lc, n_state), lambda s, g, c: (ab0 + s * nc + c, bcol + n_groups + g)),
        pl.BlockSpec((lc, gw), lambda s, g, c: (rb0 + s * nc + c, g)),
        pl.BlockSpec((lc, n_heads), lambda s, g, c: (rb0 + s * nc + c, 0)),
        pl.BlockSpec((1, n_heads), lambda s, g, c: (0, 0)),
        pl.BlockSpec((1, n_heads), lambda s, g, c: (0, 0)),
        pl.BlockSpec((1, gw), lambda s, g, c: (0, g)),
        pl.BlockSpec((1, gw), lambda s, g, c: (0, g)),
    ]
    args = [act, act, act, zx, dt_raw, dt_bias.reshape(1, n_heads).astype(F32),
            a_log.reshape(1, n_heads).astype(F32), d_e, norm_g.reshape(1, inner).astype(F32)]
    if with_h0:
        in_specs.append(pl.BlockSpec((None, gw, n_state), lambda s, g, c: (s, g, 0)))
        args.append(h0)
    aliases = {}
    if y_into is not None:
        in_specs.insert(0, pl.BlockSpec(memory_space=pl.ANY))
        args.insert(0, y_into)
        aliases = {0: 0}
    return pl.pallas_call(
        functools.partial(_ssd_kernel, with_h0=with_h0, with_into=y_into is not None, hpg=hpg, p=p),
        grid=(n_seq, n_groups, nc),
        in_specs=in_specs,
        out_specs=[pl.BlockSpec((lc, gw), lambda s, g, c: (rb0 + s * nc + c, g)),
                   pl.BlockSpec((None, gw, n_state), lambda s, g, c: (s, g, 0))],
        out_shape=[jax.ShapeDtypeStruct((zx.shape[0], inner), BF16),
                   jax.ShapeDtypeStruct((n_seq, inner, n_state), F32)],
        scratch_shapes=[pltpu.VMEM((n_state, gw), F32), pltpu.VMEM((lc, gw), F32)],
        input_output_aliases=aliases,
        compiler_params=_params("arbitrary", "arbitrary", "arbitrary"),
        name="ssd",
    )(*args)


def _sb_block(q, k, v, r_run, acc, scale, diag):
    tq, tk = q.shape[0], k.shape[0]
    z = lax.dot_general(q, k, (((1,), (1,)), ((), ())), preferred_element_type=F32) * scale
    ls = jnp.minimum(z, 0.0) - jnp.log(1.0 + jnp.exp(-jnp.abs(z)))
    l1 = ls - z
    if diag:
        mask = lax.broadcasted_iota(jnp.int32, (tq, tk), 1) < lax.broadcasted_iota(jnp.int32, (tq, tk), 0)
        l1 = jnp.where(mask, l1, 0.0)
    later = (lax.broadcasted_iota(jnp.int32, (tk, tk), 0) > lax.broadcasted_iota(jnp.int32, (tk, tk), 1))
    after = _dot_f32_01(l1, later.astype(BF16))
    w = jnp.exp(ls + after + r_run)
    if diag:
        w = jnp.where(mask, w, 0.0)
    acc = acc + jnp.dot(w.astype(BF16), v, preferred_element_type=F32)
    r_run = r_run + jnp.sum(l1, axis=1, keepdims=True)
    return r_run, acc


def _sb_kernel(q_ref, k_ref, v_ref, o_ref, *, hb, dh, scale):
    tq = tk = q_ref.shape[0]

    def head(ref, rows, h):
        return ref[rows, h * dh:(h + 1) * dh]

    qi = pl.program_id(2)
    own = pl.ds(pl.multiple_of(qi * tq, tq), tq)

    state = []
    for h in range(hb):
        q = head(q_ref, slice(None), h)
        r_run, acc = _sb_block(q, head(k_ref, own, h).astype(BF16), head(v_ref, own, h).astype(BF16),
                               jnp.zeros((tq, 1), F32), jnp.zeros((tq, dh), F32), scale, True)
        state += [r_run, acc]
    rmax = functools.reduce(jnp.maximum, [jnp.max(state[2 * h]) for h in range(hb)])

    def cond(carry):
        kb, rmax = carry[0], carry[1]
        return jnp.logical_and(kb >= 0, rmax > EXP_ZERO_BELOW)

    def body(carry):
        kb, st = carry[0], list(carry[2:])
        rows = pl.ds(pl.multiple_of(kb * tk, tk), tk)
        for h in range(hb):
            q = head(q_ref, slice(None), h)
            st[2 * h], st[2 * h + 1] = _sb_block(q, head(k_ref, rows, h).astype(BF16),
                                                 head(v_ref, rows, h).astype(BF16),
                                                 st[2 * h], st[2 * h + 1], scale, False)
        rmax = functools.reduce(jnp.maximum, [jnp.max(st[2 * h]) for h in range(hb)])
        return (kb - 1, rmax, *st)

    out = lax.while_loop(cond, body, (qi - 1, rmax, *state))
    for h in range(hb):
        o_ref[:, h * dh:(h + 1) * dh] = out[3 + 2 * h].astype(o_ref.dtype)


def _sb_past_kernel(into_ref, q_ref, k_ref, v_ref, pk_hbm, pv_hbm, o_ref, rm_ref, r_ref, acc_ref, pk_ref, pv_ref, sem,
                    *, hb, dh, tk, last, scale, single):
    s, hg, pb = pl.program_id(0), pl.program_id(1), pl.program_id(2)
    tq = q_ref.shape[0]
    rows = pl.ds((last - pb) * tk, tk)
    copies = []
    for h in range(hb):
        copies.append(pltpu.make_async_copy(pk_hbm.at[s, rows, hg * hb + h, :], pk_ref.at[h], sem.at[0, h]))
        copies.append(pltpu.make_async_copy(pv_hbm.at[s, rows, hg * hb + h, :], pv_ref.at[h], sem.at[1, h]))
    for cp in copies:
        cp.start()

    def cols(h):
        return slice(h * dh, (h + 1) * dh)

    def own(h):
        return _sb_block(q_ref[:, cols(h)], k_ref[:, cols(h)].astype(BF16), v_ref[:, cols(h)].astype(BF16),
                         jnp.zeros((tq, 1), F32), jnp.zeros((tq, dh), F32), scale, True)

    if single:
        state = [own(h) for h in range(hb)]
        for cp in copies:
            cp.wait()
        rmax = None
        for h in range(hb):
            r_run, acc = _sb_block(q_ref[:, cols(h)], pk_ref[h].astype(BF16), pv_ref[h].astype(BF16),
                                   state[h][0], state[h][1], scale, False)
            o_ref[:, cols(h)] = acc.astype(o_ref.dtype)
            rmax = r_run if rmax is None else jnp.maximum(rmax, r_run)
        rm_ref[...] = jnp.broadcast_to(jnp.max(rmax), rm_ref.shape)
        return

    @pl.when(pb == 0)
    def _():
        for h in range(hb):
            r_run, acc = own(h)
            r_ref[h] = jnp.broadcast_to(r_run, (tq, 128))
            acc_ref[:, cols(h)] = acc

    for cp in copies:
        cp.wait()
    rmax = jnp.max(r_ref[...])

    @pl.when(rmax > EXP_ZERO_BELOW)
    def _():
        for h in range(hb):
            r_run, acc = _sb_block(q_ref[:, cols(h)], pk_ref[h].astype(BF16), pv_ref[h].astype(BF16),
                                   r_ref[h][:, 0:1], acc_ref[:, cols(h)], scale, False)
            r_ref[h] = jnp.broadcast_to(r_run, (tq, 128))
            acc_ref[:, cols(h)] = acc

    @pl.when(pb == pl.num_programs(2) - 1)
    def _():
        o_ref[...] = acc_ref[...].astype(o_ref.dtype)
        rm_ref[...] = jnp.broadcast_to(jnp.max(r_ref[...]), rm_ref.shape)


def _sb_prompt(q, k, v, *, n_seq, seq_len, dh, tq=256, hb=4):
    d = q.shape[1]
    nq = seq_len // tq
    assert seq_len % tq == 0 and d % (hb * dh) == 0
    hw = hb * dh
    kv = pl.BlockSpec((seq_len, hw), lambda b, h, i: (b, h))
    return pl.pallas_call(
        functools.partial(_sb_kernel, hb=hb, dh=dh, scale=dh ** -0.5),
        grid=(n_seq, d // hw, nq),
        in_specs=[pl.BlockSpec((tq, hw), lambda b, h, i: (b * nq + i, h)), kv, kv],
        out_specs=pl.BlockSpec((tq, hw), lambda b, h, i: (b * nq + i, h)),
        out_shape=jax.ShapeDtypeStruct(q.shape, BF16),
        compiler_params=_params("arbitrary", "arbitrary", "arbitrary"),
        name="sb_prompt",
    )(q, k, v)


def _sb_past(att_into, q, k, v, past_k, past_v, *, row0, seq_len, n_blocks, tk, hb=8):
    d = q.shape[1]
    n_seq, past_len, n_h, dh = past_k.shape
    hb = min(hb, n_h)
    assert past_len % tk == 0 and row0 % seq_len == 0 and n_h % hb == 0
    hw = hb * dh
    ng = n_h // hb
    rb0 = row0 // seq_len
    last = past_len // tk - 1
    new = pl.BlockSpec((seq_len, hw), lambda s, h, b: (rb0 + s, h))
    new_kv = pl.BlockSpec((seq_len, hw), lambda s, h, b: (s, h))
    hbm = pl.BlockSpec(memory_space=pl.ANY)
    return pl.pallas_call(
        functools.partial(_sb_past_kernel, hb=hb, dh=dh, tk=tk, last=last, scale=dh ** -0.5, single=n_blocks == 1),
        grid=(n_seq, ng, n_blocks),
        in_specs=[hbm, new, new_kv, new_kv, hbm, hbm],
        out_specs=[new, pl.BlockSpec((None, 8, 128), lambda s, h, b: (s * ng + h, 0, 0))],
        out_shape=[jax.ShapeDtypeStruct(q.shape, BF16), jax.ShapeDtypeStruct((n_seq * ng, 8, 128), F32)],
        scratch_shapes=[pltpu.VMEM((hb, seq_len, 128), F32), pltpu.VMEM((seq_len, hw), F32),
                        pltpu.VMEM((hb, tk, dh), F32), pltpu.VMEM((hb, tk, dh), F32),
                        pltpu.SemaphoreType.DMA((2, hb))],
        input_output_aliases={0: 0},
        compiler_params=_params("arbitrary", "arbitrary", "arbitrary"),
        name="sb_past",
    )(att_into, q, k, v, past_k, past_v)


def _sb_sample(att_into, q, k, v, past_k, past_v, *, row0, seq_len, tk=256):
    past_len = past_k.shape[1]
    tk = min(tk, past_len)
    nblk = past_len // tk
    kw = dict(row0=row0, seq_len=seq_len, tk=tk)
    att, rm = _sb_past(att_into, q, k, v, past_k, past_v, n_blocks=1, **kw)
    if nblk == 1:
        return att
    return lax.cond(jnp.max(rm) > EXP_ZERO_BELOW,
                    lambda a: _sb_past(a, q, k, v, past_k, past_v, n_blocks=nblk, **kw)[0],
                    lambda a: a, att)


def _gather_rows_kernel(tok_ref, nused_ref, src_hbm, o_ref, buf_ref, sem, *, tm):
    b = pl.program_id(0)
    n_used = nused_ref[0]

    def issue(blk, slot):
        for j in range(tm):
            tok = tok_ref[blk * tm + j]
            pltpu.make_async_copy(src_hbm.at[tok // 8, tok % 8], buf_ref.at[slot, j // 8, j % 8],
                                  sem.at[slot]).start(priority=j % 2)

    @pl.when(jnp.logical_and(b == 0, n_used > 0))
    def _():
        issue(0, 0)

    @pl.when(b + 1 < n_used)
    def _():
        issue(b + 1, (b + 1) % 2)

    @pl.when(b < n_used)
    def _():
        slot = b % 2
        pltpu.make_async_copy(src_hbm.at[pl.ds(0, tm // 8)], buf_ref.at[slot], sem.at[slot]).wait()
        half = o_ref.shape[1] // 2
        words = buf_ref[slot].reshape(tm, half)
        o_ref[:, :half] = lax.bitcast_convert_type(words << 16, F32).astype(o_ref.dtype)
        o_ref[:, half:] = lax.bitcast_convert_type(words & jnp.uint32(0xFFFF0000), F32).astype(o_ref.dtype)


def _gather_rows(tok, n_used, xw, *, tm):
    t, dw = xw.shape
    n = tok.shape[0]
    assert t % 8 == 0 and tm % 8 == 0 and n % tm == 0
    return pl.pallas_call(
        functools.partial(_gather_rows_kernel, tm=tm),
        grid_spec=pltpu.PrefetchScalarGridSpec(
            num_scalar_prefetch=2, grid=(n // tm,),
            in_specs=[pl.BlockSpec(memory_space=pl.ANY)],
            out_specs=pl.BlockSpec((tm, 2 * dw), lambda b, tok, nu: (b, 0)),
            scratch_shapes=[pltpu.VMEM((2, tm // 8, 8, dw), jnp.uint32), pltpu.SemaphoreType.DMA((2,))]),
        out_shape=jax.ShapeDtypeStruct((n, 2 * dw), BF16),
        compiler_params=_params("arbitrary"),
        name="gather_rows",
    )(tok, n_used, xw.reshape(t // 8, 8, dw))


def _expert_mm_kernel(start_ref, cnt_ref, x_hbm, *rest, n_w, tm, tn, layer, n_groups, groups_per_block):
    w_hbms, o_hbm = rest[:n_w], rest[n_w]
    wbuf = rest[n_w + 1]
    wb_refs = rest[n_w + 2:2 * n_w + 2]
    xbuf, obuf, pend, sem_w, sem_in, sem_out = rest[2 * n_w + 2:]
    e, c = pl.program_id(0), pl.program_id(1)
    n_c = pl.num_programs(1)
    step = e * n_c + c
    wslot = step % 2
    has_next = step + 1 < pl.num_programs(0) * n_c
    e2 = jnp.minimum((step + 1) // n_c, pl.num_programs(0) - 1)
    c2 = (step + 1) % n_c
    n = cnt_ref[e]
    b0 = start_ref[e]
    kg = wbuf.shape[2] // n_groups

    def in_copy(first_block, r, slot):
        return pltpu.make_async_copy(x_hbm.at[pl.ds((first_block + r) * tm, tm)], xbuf.at[slot], sem_in.at[slot])

    def out_copy(r, slot):
        return pltpu.make_async_copy(obuf.at[slot], o_hbm.at[pl.ds((b0 + r) * tm, tm), pl.ds(c * tn, tn)],
                                     sem_out.at[slot])

    def wait_out(slot):
        @pl.when(pend[slot] == 1)
        def _():
            pltpu.make_async_copy(obuf.at[slot], o_hbm.at[pl.ds(0, tm), pl.ds(0, tn)], sem_out.at[slot]).wait()
            pend[slot] = 0

    def fetch_group(e_, c_, slot_, g):
        for j in range(n_w):
            pltpu.make_async_copy(w_hbms[j].at[layer, e_, pl.ds(g * kg, kg), pl.ds(c_ * tn, tn)],
                                  wbuf.at[slot_, j, pl.ds(g * kg, kg)], sem_w.at[slot_, j]).start()

    @pl.when(step == 0)
    def _():
        pend[0] = 0
        pend[1] = 0
        lax.fori_loop(0, n_groups, lambda g, carry: (fetch_group(e, c, wslot, g), carry)[1], 0)

        @pl.when(n > 0)
        def _():
            in_copy(b0, 0, 0).start()

    for j in range(n_w):
        pltpu.make_async_copy(w_hbms[j].at[layer, e, :, pl.ds(c * tn, tn)], wbuf.at[wslot, j], sem_w.at[wslot, j]).wait()
        wb_refs[j][...] = wbuf[wslot, j].astype(BF16)

    def block(r, carry):
        slot = r % 2

        @pl.when(r + 1 < n)
        def _():
            in_copy(b0, r + 1, 1 - slot).start()

        for k in range(groups_per_block):
            g = r * groups_per_block + k

            @pl.when(jnp.logical_and(has_next, g < n_groups))
            def _():
                fetch_group(e2, c2, 1 - wslot, g)

        in_copy(b0, r, slot).wait()

        wait_out(slot)
        x = xbuf[slot]
        res = jnp.dot(x, wb_refs[0][...], preferred_element_type=F32)
        if n_w == 2:
            res = res * jax.nn.sigmoid(res) * jnp.dot(x, wb_refs[1][...], preferred_element_type=F32)
        obuf[slot] = res.astype(obuf.dtype)
        out_copy(r, slot).start()
        pend[slot] = 1
        return carry

    lax.fori_loop(0, n, block, 0)

    @pl.when(has_next)
    def _():
        first = jnp.minimum(n * groups_per_block, n_groups)
        lax.fori_loop(first, n_groups, lambda g, carry: (fetch_group(e2, c2, 1 - wslot, g), carry)[1], 0)

        @pl.when(cnt_ref[e2] > 0)
        def _():
            in_copy(start_ref[e2], 0, 0).start()

    @pl.when(jnp.logical_not(has_next))
    def _():
        wait_out(0)
        wait_out(1)


def _expert_mm(x, ws, layer, blk_start, blk_cnt, *, tm, tn, out_dtype, groups_per_block, name, n_groups=8):
    pr, k = x.shape
    n_e, n = ws[0].shape[1], ws[0].shape[-1]
    tn = min(tn, n)
    n_w = len(ws)
    assert k % (8 * n_groups) == 0
    hbm = pl.BlockSpec(memory_space=pl.ANY)
    return pl.pallas_call(
        functools.partial(_expert_mm_kernel, n_w=n_w, tm=tm, tn=tn, layer=layer, n_groups=n_groups,
                          groups_per_block=groups_per_block),
        grid_spec=pltpu.PrefetchScalarGridSpec(
            num_scalar_prefetch=2, grid=(n_e, n // tn),
            in_specs=[hbm] * (1 + n_w),
            out_specs=hbm,
            scratch_shapes=[pltpu.VMEM((2, n_w, k, tn), F32)] + [pltpu.VMEM((k, tn), BF16)] * n_w + [
                pltpu.VMEM((2, tm, k), BF16), pltpu.VMEM((2, tm, tn), out_dtype), pltpu.SMEM((2,), jnp.int32),
                pltpu.SemaphoreType.DMA((2, n_w)), pltpu.SemaphoreType.DMA((2,)), pltpu.SemaphoreType.DMA((2,))]),
        out_shape=jax.ShapeDtypeStruct((pr, n), out_dtype),
        compiler_params=_params("arbitrary", "arbitrary"),
        name=name,
    )(blk_start, blk_cnt, x, *ws)


def _route_kernel(s_ref, rb_ref, idx_ref, gate_ref, rank_ref, cnt_ref, carry_ref, *, n_groups):
    @pl.when(pl.program_id(0) == 0)
    def _():
        carry_ref[...] = jnp.zeros(carry_ref.shape, F32)

    s = s_ref[...]
    tr, e = s.shape
    per = e // n_groups
    sel = s + rb_ref[...]
    lane_i = lax.broadcasted_iota(jnp.int32, (tr, e), 1)
    lane = lane_i.astype(F32)
    grp = (lane_i // per).astype(F32)
    neg = -jnp.inf

    def top1(v):
        m = jnp.max(v, axis=1, keepdims=True)
        return m, jnp.min(jnp.where(v == m, lane, float(e)), axis=1, keepdims=True)

    best = gi = None
    for g in range(n_groups):
        vg = jnp.where(grp == float(g), sel, neg)
        m1, i1 = top1(vg)
        m2, _ = top1(jnp.where(lane == i1, neg, vg))
        score = m1 + m2
        if best is None:
            best, gi = score, jnp.zeros_like(score)
        else:
            better = score > best
            gi = jnp.where(better, float(g), gi)
            best = jnp.where(better, score, best)
    vc = jnp.where(grp == gi, sel, neg)
    _, e1 = top1(vc)
    _, e2 = top1(jnp.where(lane == e1, neg, vc))
    oh1, oh2 = lane == e1, lane == e2
    g1 = jnp.sum(jnp.where(oh1, s, 0.0), axis=1, keepdims=True)
    g2 = jnp.sum(jnp.where(oh2, s, 0.0), axis=1, keepdims=True)
    den = g1 + g2

    oh = jnp.logical_or(oh1, oh2)
    earlier = (lax.broadcasted_iota(jnp.int32, (tr, tr), 1) < lax.broadcasted_iota(jnp.int32, (tr, tr), 0))
    before = jnp.dot(earlier.astype(BF16), oh.astype(BF16), preferred_element_type=F32) + carry_ref[...]
    r1 = jnp.sum(jnp.where(oh1, before, 0.0), axis=1, keepdims=True)
    r2 = jnp.sum(jnp.where(oh2, before, 0.0), axis=1, keepdims=True)
    carry = carry_ref[...] + jnp.sum(oh.astype(F32), axis=0, keepdims=True)
    carry_ref[...] = carry
    cnt_ref[...] = carry.astype(jnp.int32)

    first = lax.broadcasted_iota(jnp.int32, (tr, TOP_K), 1) == 0
    idx_ref[...] = jnp.where(first, e1, e2).astype(jnp.int32)
    gate_ref[...] = jnp.where(first, g1, g2) / den
    rank_ref[...] = jnp.where(first, r1, r2).astype(jnp.int32)


def _route(s, router_b, tr=512):
    t, e = s.shape
    tr = _pick_tile(t, tr)
    col = pl.BlockSpec((tr, TOP_K), lambda i: (i, 0))
    return pl.pallas_call(
        functools.partial(_route_kernel, n_groups=N_EXPERT_GROUPS),
        grid=(t // tr,),
        in_specs=[pl.BlockSpec((tr, e), lambda i: (i, 0)), pl.BlockSpec((1, e), lambda i: (0, 0))],
        out_specs=[col, col, col, pl.BlockSpec((1, e), lambda i: (0, 0))],
        out_shape=[jax.ShapeDtypeStruct((t, TOP_K), jnp.int32), jax.ShapeDtypeStruct((t, TOP_K), F32),
                   jax.ShapeDtypeStruct((t, TOP_K), jnp.int32), jax.ShapeDtypeStruct((1, e), jnp.int32)],
        scratch_shapes=[pltpu.VMEM((1, e), F32)],
        compiler_params=_params("arbitrary"),
        name="route",
    )(s, router_b.reshape(1, e).astype(F32))


def _moe(h, s, router_b, w_gate, w_up, w_down, layer, *, tm=256):
    t, e = s.shape
    idx, gates, rank, counts = _route(s, router_b)
    counts = counts.reshape(e)
    padded = (counts + tm - 1) // tm * tm
    pad_ends = jnp.cumsum(padded)
    pad_starts = pad_ends - padded
    slot = pad_starts[idx] + rank
    nb = -(-(t * TOP_K) // tm) + e
    tok = jnp.broadcast_to(jnp.arange(t, dtype=jnp.int32)[:, None], (t, TOP_K))
    tok_of_slot = jnp.zeros((nb * tm,), jnp.int32).at[slot.reshape(-1)].set(tok.reshape(-1))
    xs = _gather_rows(tok_of_slot, (pad_ends[-1:] // tm).astype(jnp.int32), h, tm=tm)
    blk_start, blk_cnt = (pad_starts // tm).astype(jnp.int32), (padded // tm).astype(jnp.int32)
    hid = _expert_mm(xs, [w_gate, w_up], layer, blk_start, blk_cnt, tm=tm, tn=512, out_dtype=BF16,
                     groups_per_block=3, name="moe_up")
    yb = _expert_mm(hid, [w_down], layer, blk_start, blk_cnt, tm=tm, tn=2048, out_dtype=F32,
                    groups_per_block=4, name="moe_down")
    return [yb[slot[:, k]] for k in range(TOP_K)], [gates[:, k:k + 1] for k in range(TOP_K)]


def kernel(x_prompt, x_sample, c_prompt, c_sample, state_conv, state_ssm, cache_k, cache_v, w_ada, b_ada, ln_g, ln_b, ssm_w_in, ssm_conv_w, ssm_conv_b, ssm_dt_bias, ssm_a_log, ssm_d, ssm_norm_g, ssm_w_out, sb_w_k, sb_w_v, sb_w_q, sb_w_o, router_w, router_b, moe_w_gate, moe_w_up, moe_w_down):
    bp, lp, d = x_prompt.shape
    bs, ls_, _ = x_sample.shape
    assert ls_ == ROWS and lp % ROWS == 0 and w_ada.shape[0] == DEPTH == 2
    n_heads, p, n_state = state_ssm.shape[2:]
    inner = n_heads * p
    conv_dim = ssm_conv_w.shape[-1]
    kw = ssm_conv_w.shape[1]
    n_groups = (conv_dim - inner) // (2 * n_state)
    sb_heads, dh = cache_k.shape[2:]
    past_len = cache_k.shape[1]
    tp, ts = bp * lp, bs * ls_
    t = tp + ts

    x = jnp.concatenate([x_prompt.reshape(tp, d), x_sample.reshape(ts, d)], axis=0)
    c_all = jnp.concatenate([c_prompt, c_sample], axis=0)
    gmap = jnp.concatenate([jnp.repeat(jnp.arange(bp), lp // ROWS), bp + jnp.arange(bs)])
    mod = _ada(c_all, w_ada, b_ada)
    modg = [mod[l][gmap][:, None, :] for l in range(DEPTH)]

    h = _modulate(x, modg[0], 1, 0)
    w_in = ssm_w_in[0]
    n_zx = inner + conv_dim
    zx = _matmul(h, w_in, n_out=n_zx, name="in_proj")
    dt_raw = _matmul(h, w_in, n_off=n_zx // 128, n_out=n_heads, tn=128, name="in_proj_dt")
    hist8 = jnp.pad(state_conv[0], ((0, 0), (8 - (kw - 1), 0), (0, 0)))
    act_p = _conv_silu(zx, inner, ssm_conv_w[0], ssm_conv_b[0], row0=0, n_seq=bp, seq_len=lp, hist8=None)
    act_s = _conv_silu(zx, inner, ssm_conv_w[0], ssm_conv_b[0], row0=tp, n_seq=bs, seq_len=ls_, hist8=hist8)
    ssd_kw = dict(n_heads=n_heads, p=p, n_state=n_state, n_groups=n_groups)
    y, hfin_p = _ssd(act_p, zx, dt_raw, ssm_dt_bias[0], ssm_a_log[0], ssm_d[0], ssm_norm_g[0], None,
                     row0=0, act_row0=0, n_seq=bp, seq_len=lp, lc=min(SSD_CHUNK, lp), **ssd_kw)
    y, hfin_s = _ssd(act_s, zx, dt_raw, ssm_dt_bias[0], ssm_a_log[0], ssm_d[0], ssm_norm_g[0],
                     state_ssm[0].reshape(bs, inner, n_state),
                     row0=tp, act_row0=0, n_seq=bs, seq_len=ls_, lc=min(SSD_CHUNK, ls_), y_into=y, **ssd_kw)
    kh = inner // 2
    m = _matmul(y, ssm_w_out[0], kblk=kh, x_koff=0, w_koff=0, name="out_proj0")
    m = _matmul(y, ssm_w_out[0], kblk=kh, x_koff=1, w_koff=1, acc=m, name="out_proj1")

    def last_rows(row0, n_seq, seq_len):
        return jnp.stack([lax.slice(zx, (row0 + (i + 1) * seq_len - (kw - 1), inner), (row0 + (i + 1) * seq_len, n_zx))
                          for i in range(n_seq)])[None]

    conv_p, conv_s = last_rows(0, bp, lp), last_rows(tp, bs, ls_)
    ssm_p = hfin_p.reshape(1, bp, n_heads, p, n_state)
    ssm_s = hfin_s.reshape(1, bs, n_heads, p, n_state)

    x, h, s = _ln(x, [m], (modg[0], 2), ln_g[0, 0], ln_b[0, 0], next_mod=(modg[0], 4, 3), h_dtype=jnp.uint32,
                  router_w=router_w)
    ys, gs = _moe(h, s, router_b, moe_w_gate, moe_w_up, moe_w_down, 0)
    x, h, xb = _ln(x, ys, (modg[0], 5), ln_g[0, 1], ln_b[0, 1], m_gates=gs, next_mod=(modg[1], 1, 0), with_xb=True)

    k_p, k_s = _matmul(xb, sb_w_k, split_rows=tp, name="k_proj")
    v_p, v_s = _matmul(xb, sb_w_v, split_rows=tp, name="v_proj")

    q = _matmul(h, sb_w_q, lead=(0,), out_dtype=BF16, name="q_proj")
    att = _sb_prompt(q, k_p, v_p, n_seq=bp, seq_len=lp, dh=dh)
    att = _sb_sample(att, q, k_s, v_s, cache_k, cache_v, row0=tp, seq_len=ls_)
    m = _matmul(att, sb_w_o, lead=(0,), name="o_proj")

    x, h, s = _ln(x, [m], (modg[1], 2), ln_g[1, 0], ln_b[1, 0], next_mod=(modg[1], 4, 3), h_dtype=jnp.uint32,
                  router_w=router_w)
    ys, gs = _moe(h, s, router_b, moe_w_gate, moe_w_up, moe_w_down, 1)
    (y_p,) = _ln(x, ys, (modg[1], 5), ln_g[1, 1], ln_b[1, 1], m_gates=gs, rows=(0, tp))
    (y_s,) = _ln(x, ys, (modg[1], 5), ln_g[1, 1], ln_b[1, 1], m_gates=gs, rows=(tp, ts))

    def heads(a, nb_, l):
        return a.reshape(nb_, l, sb_heads, dh)

    return (y_p.reshape(bp, lp, d), y_s.reshape(bs, ls_, d), conv_p, ssm_p,
            heads(k_p, bp, lp), heads(v_p, bp, lp), conv_s, ssm_s, heads(k_s, bs, ls_), heads(v_s, bs, ls_))
```

```python
import functools
import math

import jax
import jax.numpy as jnp
from jax import lax
from jax.experimental import pallas as pl
from jax.experimental.pallas import tpu as pltpu

F32 = jnp.float32
BF16 = jnp.bfloat16
HIGHEST = lax.Precision.HIGHEST

DEPTH = 2
N_EXPERT_GROUPS = 4
TOP_K = 2
DN_ALPHA = (2 * DEPTH) ** 0.25
LN_EPS = 1e-5
RMS_EPS = 1e-5
SSD_CHUNK = 128

ROWS = 32
VMEM_LIMIT = 52 * 1024 * 1024
EXP_ZERO_BELOW = -105.0


def _params(*sem):
    return pltpu.CompilerParams(dimension_semantics=sem, vmem_limit_bytes=VMEM_LIMIT)


def _split3(x):
    hi = x.astype(BF16)
    r1 = x - hi.astype(F32)
    mid = r1.astype(BF16)
    lo = (r1 - mid.astype(F32)).astype(BF16)
    return hi, mid, lo


def _dot_f32_01(x, m01):
    hi, mid, lo = _split3(x)
    d = functools.partial(jnp.dot, preferred_element_type=F32)
    return (d(lo, m01) + d(mid, m01)) + d(hi, m01)


def _dot_01_f32(m01, x):
    hi, mid, lo = _split3(x)
    d = functools.partial(jnp.dot, preferred_element_type=F32)
    return (d(m01, lo) + d(m01, mid)) + d(m01, hi)


def _pick_tile(n, cap, mult=128):
    best = None
    for c in range(mult, min(n, cap) + 1, mult):
        if n % c == 0:
            best = c
    assert best is not None, (n, cap)
    return best


def _ada_kernel(c_ref, w_ref, b_ref, o_ref):
    c = c_ref[...]
    s = (c * jax.nn.sigmoid(c)).astype(BF16)
    o_ref[...] = jnp.dot(s, w_ref[...].astype(BF16), preferred_element_type=F32) + b_ref[...]


def _ada(c_all, w_ada, b_ada):
    depth, d, n = w_ada.shape
    bc = c_all.shape[0]
    tn = 512
    return pl.pallas_call(
        _ada_kernel,
        grid=(depth, n // tn),
        in_specs=[pl.BlockSpec((bc, d), lambda l, j: (0, 0)),
                  pl.BlockSpec((None, d, tn), lambda l, j: (l, 0, j)),
                  pl.BlockSpec((None, 1, tn), lambda l, j: (l, 0, j))],
        out_specs=pl.BlockSpec((None, bc, tn), lambda l, j: (l, 0, j)),
        out_shape=jax.ShapeDtypeStruct((depth, bc, n), F32),
        compiler_params=_params("arbitrary", "arbitrary"),
        name="ada",
    )(c_all, w_ada, b_ada.reshape(depth, 1, n))


def _two_part_specs(parts, tr):
    a, b = parts
    assert a.shape[0] % tr == 0 and b.shape[0] % tr == 0
    n_first = a.shape[0] // tr
    d = a.shape[1]
    return n_first, [pl.BlockSpec((tr, d), lambda i: (jnp.minimum(i, n_first - 1), 0)),
                     pl.BlockSpec((tr, d), lambda i: (jnp.maximum(i - n_first, 0), 0))]


def _two_part_load(a_ref, b_ref, n_first):
    return jnp.where(pl.program_id(0) < n_first, a_ref[...], b_ref[...])


def _mod_kernel(xa_ref, xb_ref, sc_ref, sh_ref, h_ref, *, n_first):
    gb = sc_ref.shape[0]
    rows, d = xa_ref.shape
    x = _two_part_load(xa_ref, xb_ref, n_first).reshape(gb, rows // gb, d)
    h = x * (1.0 + sc_ref[...]) + sh_ref[...]
    h_ref[...] = h.reshape(rows, d).astype(h_ref.dtype)


def _modulate(x_parts, modg, scale_blk, shift_blk, gb=8):
    d = x_parts[0].shape[1]
    t = x_parts[0].shape[0] + x_parts[1].shape[0]
    g = modg.shape[0]
    n_first, xspecs = _two_part_specs(x_parts, gb * ROWS)
    return pl.pallas_call(
        functools.partial(_mod_kernel, n_first=n_first),
        grid=(g // gb,),
        in_specs=xspecs + [pl.BlockSpec((gb, 1, d), lambda i: (i, 0, scale_blk)),
                           pl.BlockSpec((gb, 1, d), lambda i: (i, 0, shift_blk))],
        out_specs=pl.BlockSpec((gb * ROWS, d), lambda i: (i, 0)),
        out_shape=jax.ShapeDtypeStruct((t, d), BF16),
        compiler_params=_params("arbitrary"),
        name="modulate",
    )(*x_parts, modg, modg)


def _ln_kernel(*refs, n_m, gated_m, with_h, with_router, with_xb, n_first):
    it = iter(refs)
    x_ref = next(it)
    x2_ref = next(it) if n_first is not None else None
    m_refs = [next(it) for _ in range(n_m)]
    mg_refs = [next(it) for _ in range(n_m)] if gated_m else []
    gate_ref, g_ref, b_ref = next(it), next(it), next(it)
    sc_ref = sh_ref = rw_ref = None
    if with_h:
        sc_ref, sh_ref = next(it), next(it)
    if with_router:
        rw_ref = next(it)
    xo_ref = next(it)
    h_ref = next(it) if with_h else None
    s_ref = next(it) if with_router else None
    xb_ref = next(it) if with_xb else None

    gb = gate_ref.shape[0]
    rows, d = x_ref.shape
    if gated_m:
        m = m_refs[0][...] * mg_refs[0][...]
        for mr, gr in zip(m_refs[1:], mg_refs[1:]):
            m = m + mr[...] * gr[...]
    else:
        m = m_refs[0][...]
        for mr in m_refs[1:]:
            m = m + mr[...]
    x = x_ref[...] if n_first is None else _two_part_load(x_ref, x2_ref, n_first)
    v = DN_ALPHA * x + (gate_ref[...] * m.reshape(gb, rows // gb, d)).reshape(rows, d)
    mu = jnp.mean(v, axis=-1, keepdims=True)
    vc = v - mu
    var = jnp.mean(vc * vc, axis=-1, keepdims=True)
    xn = vc * lax.rsqrt(var + LN_EPS) * g_ref[...] + b_ref[...]
    xo_ref[...] = xn
    if with_xb:
        xb_ref[...] = xn.astype(BF16)
    if with_h:
        h = xn.reshape(gb, rows // gb, d) * (1.0 + sc_ref[...]) + sh_ref[...]
        h = h.reshape(rows, d)
        if h_ref.dtype == jnp.uint32:
            bits = lax.bitcast_convert_type(h.astype(BF16).astype(F32), jnp.uint32)
            h_ref[...] = (bits[:, :d // 2] >> 16) | (bits[:, d // 2:] & jnp.uint32(0xFFFF0000))
        else:
            h_ref[...] = h.astype(h_ref.dtype)
        if with_router:
            logits = jnp.dot(h, rw_ref[...], preferred_element_type=F32, precision=HIGHEST)
            s_ref[...] = jax.nn.sigmoid(logits)


def _ln(x, ms, gate_src, ln_g, ln_b, *, m_gates=None, next_mod=None, h_dtype=BF16, router_w=None, with_xb=False,
        rows=None, gb=8):
    x_parts = x if isinstance(x, (tuple, list)) else None
    assert x_parts is None or rows is None
    d = (x_parts[0] if x_parts else x).shape[1]
    modg, gate_blk = gate_src
    tr = gb * ROWS
    n_rows = x_parts[0].shape[0] + x_parts[1].shape[0] if x_parts else x.shape[0]
    r0, t = (0, n_rows) if rows is None else rows
    assert r0 % tr == 0 and t % tr == 0
    off = r0 // tr
    row = pl.BlockSpec((tr, d), lambda i: (i + off, 0))
    out_row = pl.BlockSpec((tr, d), lambda i: (i, 0))
    vec = pl.BlockSpec((1, d), lambda i: (0, 0))

    def modspec(blk):
        return pl.BlockSpec((gb, 1, d), lambda i: (i + off, 0, blk))

    n_first = None
    if x_parts:
        n_first, xspecs = _two_part_specs(x_parts, tr)
        args = list(x_parts) + list(ms)
        specs = xspecs + [row] * len(ms)
    else:
        args = [x] + list(ms)
        specs = [row] + [row] * len(ms)
    if m_gates is not None:
        args += list(m_gates)
        specs += [pl.BlockSpec((tr, 1), lambda i: (i + off, 0))] * len(ms)
    args += [modg, ln_g.reshape(1, d), ln_b.reshape(1, d)]
    specs += [modspec(gate_blk), vec, vec]
    outs = [jax.ShapeDtypeStruct((t, d), F32)]
    out_specs = [out_row]
    row = out_row
    if next_mod is not None:
        nm, sc_blk, sh_blk = next_mod
        args += [nm, nm]
        specs += [modspec(sc_blk), modspec(sh_blk)]
        hd = d // 2 if h_dtype == jnp.uint32 else d
        outs.append(jax.ShapeDtypeStruct((t, hd), h_dtype))
        out_specs.append(pl.BlockSpec((tr, hd), lambda i: (i, 0)))
    if router_w is not None:
        e = router_w.shape[1]
        args.append(router_w)
        specs.append(pl.BlockSpec((d, e), lambda i: (0, 0)))
        outs.append(jax.ShapeDtypeStruct((t, e), F32))
        out_specs.append(pl.BlockSpec((tr, e), lambda i: (i, 0)))
    if with_xb:
        outs.append(jax.ShapeDtypeStruct((t, d), BF16))
        out_specs.append(row)
    kern = functools.partial(_ln_kernel, n_m=len(ms), gated_m=m_gates is not None,
                             with_h=next_mod is not None, with_router=router_w is not None, with_xb=with_xb,
                             n_first=n_first)
    return pl.pallas_call(
        kern, grid=(t // tr,), in_specs=specs, out_specs=out_specs, out_shape=outs,
        compiler_params=_params("arbitrary"), name="ln",
    )(*args)


def _mm_kernel(*refs, with_acc, n_first):
    x_ref, w_ref = refs[:2]
    a_ref = refs[2] if with_acc else None
    o_refs, wb_ref = refs[2 + with_acc:-1], refs[-1]

    @pl.when(pl.program_id(1) == 0)
    def _():
        wb_ref[...] = w_ref[...].astype(BF16)

    r = jnp.dot(x_ref[...], wb_ref[...], preferred_element_type=F32)
    if with_acc:
        r = r + a_ref[...]
    if n_first is None:
        o_refs[0][...] = r.astype(o_refs[0].dtype)
    else:
        @pl.when(pl.program_id(1) < n_first)
        def _():
            o_refs[0][...] = r.astype(o_refs[0].dtype)

        @pl.when(pl.program_id(1) >= n_first)
        def _():
            o_refs[1][...] = r.astype(o_refs[1].dtype)


def _matmul(x, w, *, lead=(), kblk=None, x_koff=0, w_koff=0, n_off=0, n_out=None, tm=1024, tn=512,
            out_dtype=F32, acc=None, split_rows=None, name="matmul"):
    t = x.shape[0]
    kfull, nfull = w.shape[-2:]
    kblk = kfull if kblk is None else kblk
    n_out = nfull if n_out is None else n_out
    tm = _pick_tile(t if split_rows is None else math.gcd(t, split_rows), tm)
    assert n_out % tn == 0
    nl = len(lead)
    in_specs = [pl.BlockSpec((tm, kblk), lambda j, i: (i, x_koff)),
                pl.BlockSpec((None,) * nl + (kblk, tn), lambda j, i: tuple(lead) + (w_koff, j + n_off))]
    args = [x, w]
    if acc is not None:
        in_specs.append(pl.BlockSpec((tm, tn), lambda j, i: (i, j)))
        args.append(acc)
    if split_rows is None:
        n_first = None
        out_specs = pl.BlockSpec((tm, tn), lambda j, i: (i, j))
        out_shape = jax.ShapeDtypeStruct((t, n_out), out_dtype)
    else:
        assert split_rows % tm == 0 and 0 < split_rows < t
        n_first = split_rows // tm
        out_specs = [pl.BlockSpec((tm, tn), lambda j, i: (jnp.minimum(i, n_first - 1), j)),
                     pl.BlockSpec((tm, tn), lambda j, i: (jnp.maximum(i - n_first, 0), j))]
        out_shape = [jax.ShapeDtypeStruct((split_rows, n_out), out_dtype),
                     jax.ShapeDtypeStruct((t - split_rows, n_out), out_dtype)]
    return pl.pallas_call(
        functools.partial(_mm_kernel, with_acc=acc is not None, n_first=n_first),
        grid=(n_out // tn, t // tm),
        in_specs=in_specs,
        out_specs=out_specs,
        out_shape=out_shape,
        scratch_shapes=[pltpu.VMEM((kblk, tn), BF16)],
        compiler_params=_params("arbitrary", "arbitrary"),
        name=name,
    )(*args)


def _conv_kernel(*refs, with_hist, kw):
    if with_hist:
        x_ref, hist_ref, w_ref, b_ref, o_ref, xp_ref = refs
        gs = hist_ref.shape[0]
    else:
        x_ref, w_ref, b_ref, o_ref, xp_ref = refs
        gs = 1
    tr = x_ref.shape[0] // gs
    w = w_ref[...]
    if not with_hist:
        @pl.when(pl.program_id(2) == 0)
        def _():
            xp_ref[0:8, :] = jnp.zeros((8, xp_ref.shape[1]), F32)
    for s in range(gs):
        rows = slice(s * tr, (s + 1) * tr)
        if with_hist:
            xp_ref[0:8, :] = hist_ref[s]
        xp_ref[8:8 + tr, :] = x_ref[rows, :]
        y = b_ref[...] + w[kw - 1:kw, :] * x_ref[rows, :]
        for k in range(1, kw):
            y = y + w[kw - 1 - k:kw - k, :] * xp_ref[8 - k:8 - k + tr, :]
        o_ref[rows, :] = (y * jax.nn.sigmoid(y)).astype(o_ref.dtype)
    if not with_hist:
        xp_ref[0:8, :] = xp_ref[tr:tr + 8, :]


def _conv_silu(zx, col0, conv_w, conv_b, *, row0, n_seq, seq_len, hist8, tc=1024):
    kw, c = conv_w.shape
    tr = min(seq_len, 512)
    nrt = seq_len // tr
    assert seq_len % tr == 0 and c % tc == 0 and col0 % tc == 0 and row0 % tr == 0
    with_hist = hist8 is not None
    assert not with_hist or nrt == 1
    gs = 1
    if with_hist:
        gs = max(g for g in (1, 2, 4, 8) if n_seq % g == 0 and g * tr <= 512)
    br = gs * tr
    assert row0 % br == 0
    rb0, cb0 = row0 // br, col0 // tc
    in_specs = [pl.BlockSpec((br, tc), lambda s, j, r: (rb0 + s * nrt + r, cb0 + j))]
    args = [zx]
    if with_hist:
        in_specs.append(pl.BlockSpec((gs, 8, tc), lambda s, j, r: (s, 0, j)))
        args.append(hist8)
    in_specs += [pl.BlockSpec((kw, tc), lambda s, j, r: (0, j)),
                 pl.BlockSpec((1, tc), lambda s, j, r: (0, j))]
    args += [conv_w, conv_b.reshape(1, c)]
    return pl.pallas_call(
        functools.partial(_conv_kernel, with_hist=with_hist, kw=kw),
        grid=(n_seq // gs, c // tc, nrt),
        in_specs=in_specs,
        out_specs=pl.BlockSpec((br, tc), lambda s, j, r: (s * nrt + r, j)),
        out_shape=jax.ShapeDtypeStruct((n_seq * seq_len, c), BF16),
        scratch_shapes=[pltpu.VMEM((tr + 8, tc), F32)],
        compiler_params=_params("arbitrary", "arbitrary", "arbitrary"),
        name="conv_silu",
    )(*args)


def _ssd_kernel(*refs, with_h0, with_into, hpg, p):
    if with_into:
        refs = refs[1:]
    if with_h0:
        (x_ref, b_ref, c_ref, z_ref, dt_ref, dtb_ref, alog_ref, dsk_ref, ng_ref, h0_ref,
         y_ref, hf_ref, ht_ref, yd_ref) = refs
    else:
        (x_ref, b_ref, c_ref, z_ref, dt_ref, dtb_ref, alog_ref, dsk_ref, ng_ref,
         y_ref, hf_ref, ht_ref, yd_ref) = refs
        h0_ref = None
    g = pl.program_id(1)
    ci = pl.program_id(2)
    lc, gw = x_ref.shape
    nh = dt_ref.shape[1]

    @pl.when(ci == 0)
    def _():
        if with_h0:
            ht_ref[...] = h0_ref[...].T
        else:
            ht_ref[...] = jnp.zeros(ht_ref.shape, F32)

    dt = jax.nn.softplus(dt_ref[...] + dtb_ref[...])
    da = dt * (-jnp.exp(alog_ref[...]))
    r_i = lax.broadcasted_iota(jnp.int32, (lc, lc), 0)
    c_i = lax.broadcasted_iota(jnp.int32, (lc, lc), 1)
    tril = r_i >= c_i
    cs_h = _dot_01_f32(tril.astype(BF16), da)
    head_of_col = g * hpg + lax.broadcasted_iota(jnp.int32, (nh, gw), 1) // p
    expand = (lax.broadcasted_iota(jnp.int32, (nh, gw), 0) == head_of_col).astype(BF16)
    dt_e = _dot_f32_01(dt, expand)
    cs = _dot_f32_01(cs_h, expand)
    cs_last = cs[lc - 1:lc, :]
    eye = r_i == c_i

    xs = x_ref[...].astype(F32)
    xdt = xs * dt_e
    xdt_b = xdt.astype(BF16)
    bm = b_ref[...]
    cm = c_ref[...]
    cb = lax.dot_general(cm, bm, (((1,), (1,)), ((), ())), preferred_element_type=F32)

    for j in range(hpg):
        col = jnp.broadcast_to(cs[:, j * p:j * p + 1], (lc, lc))
        row = jnp.sum(jnp.where(eye, col, 0.0), axis=0, keepdims=True)
        dec = jnp.exp(jnp.where(tril, col - row, -jnp.inf))
        mj = (cb * dec).astype(BF16)
        yd_ref[:, j * p:(j + 1) * p] = jnp.dot(mj, xdt_b[:, j * p:(j + 1) * p], preferred_element_type=F32)

    ht = ht_ref[...]
    y_off = jnp.dot(cm, ht.astype(BF16), preferred_element_type=F32) * jnp.exp(cs)
    xw = (xdt * jnp.exp(cs_last - cs)).astype(BF16)
    st_t = lax.dot_general(bm, xw, (((0,), (0,)), ((), ())), preferred_element_type=F32)
    ht_new = jnp.exp(cs_last) * ht + st_t
    ht_ref[...] = ht_new

    @pl.when(ci == pl.num_programs(2) - 1)
    def _():
        hf_ref[...] = ht_new.T

    y = yd_ref[...] + y_off + dsk_ref[...] * xs
    z = z_ref[...]
    y = y * (z * jax.nn.sigmoid(z))
    ms = jnp.mean(y * y, axis=-1, keepdims=True)
    y_ref[...] = (y * lax.rsqrt(ms + RMS_EPS) * ng_ref[...]).astype(y_ref.dtype)


def _ssd(act, zx, dt_raw, dt_bias, a_log, d_skip, norm_g, h0, *, row0, act_row0, n_seq, seq_len, lc,
         n_heads, p, n_state, n_groups, y_into=None):
    inner = n_heads * p
    hpg = n_heads // n_groups
    gw = hpg * p
    nc = seq_len // lc
    assert seq_len % lc == 0 and row0 % lc == 0 and act_row0 % lc == 0 and gw % 128 == 0 and n_state % 128 == 0
    rb0, ab0 = row0 // lc, act_row0 // lc
    bcol = inner // n_state
    with_h0 = h0 is not None
    d_e = jnp.repeat(d_skip.astype(F32), p).reshape(1, inner)
    in_specs = [
        pl.BlockSpec((lc, gw), lambda s, g, c: (ab0 + s * nc + c, g)),
        pl.BlockSpec((lc, n_state), lambda s, g, c: (ab0 + s * nc + c, bcol + g)),
        pl.BlockSpec((lc, n_state), lambda s, g, c: (ab0 + s * nc + c, bcol + n_groups + g)),
        pl.BlockSpec((lc, gw), lambda s, g, c: (rb0 + s * nc + c, g)),
        pl.BlockSpec((lc, n_heads), lambda s, g, c: (rb0 + s * nc + c, 0)),
        pl.BlockSpec((1, n_heads), lambda s, g, c: (0, 0)),
        pl.BlockSpec((1, n_heads), lambda s, g, c: (0, 0)),
        pl.BlockSpec((1, gw), lambda s, g, c: (0, g)),
        pl.BlockSpec((1, gw), lambda s, g, c: (0, g)),
    ]
    args = [act, act, act, zx, dt_raw, dt_bias.reshape(1, n_heads).astype(F32),
            a_log.reshape(1, n_heads).astype(F32), d_e, norm_g.reshape(1, inner).astype(F32)]
    if with_h0:
        in_specs.append(pl.BlockSpec((None, gw, n_state), lambda s, g, c: (s, g, 0)))
        args.append(h0)
    aliases = {}
    if y_into is not None:
        in_specs.insert(0, pl.BlockSpec(memory_space=pl.ANY))
        args.insert(0, y_into)
        aliases = {0: 0}
    return pl.pallas_call(
        functools.partial(_ssd_kernel, with_h0=with_h0, with_into=y_into is not None, hpg=hpg, p=p),
        grid=(n_seq, n_groups, nc),
        in_specs=in_specs,
        out_specs=[pl.BlockSpec((lc, gw), lambda s, g, c: (rb0 + s * nc + c, g)),
                   pl.BlockSpec((None, gw, n_state), lambda s, g, c: (s, g, 0))],
        out_shape=[jax.ShapeDtypeStruct((zx.shape[0], inner), BF16),
                   jax.ShapeDtypeStruct((n_seq, inner, n_state), F32)],
        scratch_shapes=[pltpu.VMEM((n_state, gw), F32), pltpu.VMEM((lc, gw), F32)],
        input_output_aliases=aliases,
        compiler_params=_params("arbitrary", "arbitrary", "arbitrary"),
        name="ssd",
    )(*args)


def _sb_block(q, k, v, r_run, acc, scale, diag, n_before=0):
    tq, tk = q.shape[0], k.shape[0]
    z = lax.dot_general(q, k, (((1,), (1,)), ((), ())), preferred_element_type=F32) * scale
    ls = jnp.minimum(z, 0.0) - jnp.log(1.0 + jnp.exp(-jnp.abs(z)))
    l1 = ls - z
    if diag:
        mask = (lax.broadcasted_iota(jnp.int32, (tq, tk), 1) - n_before
                < lax.broadcasted_iota(jnp.int32, (tq, tk), 0))
        l1 = jnp.where(mask, l1, 0.0)
    later = (lax.broadcasted_iota(jnp.int32, (tk, tk), 0) > lax.broadcasted_iota(jnp.int32, (tk, tk), 1))
    after = _dot_f32_01(l1, later.astype(BF16))
    w = jnp.exp(ls + after + r_run)
    if diag:
        w = jnp.where(mask, w, 0.0)
    acc = acc + jnp.dot(w.astype(BF16), v, preferred_element_type=F32)
    r_run = r_run + jnp.sum(l1, axis=1, keepdims=True)
    return r_run, acc


def _sb_kernel(q_ref, k_ref, v_ref, o_ref, *, hb, dh, scale):
    tq = tk = q_ref.shape[0]

    def head(ref, rows, h):
        return ref[rows, h * dh:(h + 1) * dh]

    qi = pl.program_id(2)
    own = pl.ds(pl.multiple_of(qi * tq, tq), tq)

    state = []
    for h in range(hb):
        q = head(q_ref, slice(None), h)
        r_run, acc = _sb_block(q, head(k_ref, own, h).astype(BF16), head(v_ref, own, h).astype(BF16),
                               jnp.zeros((tq, 1), F32), jnp.zeros((tq, dh), F32), scale, True)
        state += [r_run, acc]
    rmax = functools.reduce(jnp.maximum, [jnp.max(state[2 * h]) for h in range(hb)])

    def cond(carry):
        kb, rmax = carry[0], carry[1]
        return jnp.logical_and(kb >= 0, rmax > EXP_ZERO_BELOW)

    def body(carry):
        kb, st = carry[0], list(carry[2:])
        rows = pl.ds(pl.multiple_of(kb * tk, tk), tk)
        for h in range(hb):
            q = head(q_ref, slice(None), h)
            st[2 * h], st[2 * h + 1] = _sb_block(q, head(k_ref, rows, h).astype(BF16),
                                                 head(v_ref, rows, h).astype(BF16),
                                                 st[2 * h], st[2 * h + 1], scale, False)
        rmax = functools.reduce(jnp.maximum, [jnp.max(st[2 * h]) for h in range(hb)])
        return (kb - 1, rmax, *st)

    out = lax.while_loop(cond, body, (qi - 1, rmax, *state))
    for h in range(hb):
        o_ref[:, h * dh:(h + 1) * dh] = out[3 + 2 * h].astype(o_ref.dtype)


def _sb_past_kernel(into_ref, q_ref, k_ref, v_ref, pk_hbm, pv_hbm, o_ref, rm_ref, r_ref, acc_ref, pk_ref, pv_ref, sem,
                    *, hb, dh, tk, last, scale, single):
    s, hg, pb = pl.program_id(0), pl.program_id(1), pl.program_id(2)
    tq = q_ref.shape[0]
    rows = pl.ds((last - pb) * tk, tk)

    def fetch(s_, hg_, slot_):
        out = []
        for h in range(hb):
            out.append(pltpu.make_async_copy(pk_hbm.at[s_, rows, hg_ * hb + h, :], pk_ref.at[slot_, h],
                                             sem.at[slot_, 0, h]))
            out.append(pltpu.make_async_copy(pv_hbm.at[s_, rows, hg_ * hb + h, :], pv_ref.at[slot_, h],
                                             sem.at[slot_, 1, h]))
        return out

    def cols(h):
        return slice(h * dh, (h + 1) * dh)

    if single:
        n_hg = pl.num_programs(1)
        step = s * n_hg + hg
        slot = step % 2

        @pl.when(step == 0)
        def _():
            for cp in fetch(s, hg, slot):
                cp.start()

        @pl.when(step + 1 < pl.num_programs(0) * n_hg)
        def _():
            for cp in fetch((step + 1) // n_hg, (step + 1) % n_hg, 1 - slot):
                cp.start()

        for cp in fetch(s, hg, slot):
            cp.wait()
        pad = [jnp.zeros(((-tq) % 128, dh), BF16)] if tq % 128 else []
        rmax = None
        for h in range(hb):
            kcat = jnp.concatenate([pk_ref[slot, h].astype(BF16), k_ref[:, cols(h)].astype(BF16)] + pad, axis=0)
            vcat = jnp.concatenate([pv_ref[slot, h].astype(BF16), v_ref[:, cols(h)].astype(BF16)] + pad, axis=0)
            r_run, acc = _sb_block(q_ref[:, cols(h)], kcat, vcat, jnp.zeros((tq, 1), F32),
                                   jnp.zeros((tq, dh), F32), scale, True, n_before=tk)
            o_ref[:, cols(h)] = acc.astype(o_ref.dtype)
            rmax = r_run if rmax is None else jnp.maximum(rmax, r_run)
        rm_ref[...] = jnp.broadcast_to(jnp.max(rmax), rm_ref.shape)
        return

    copies = fetch(s, hg, 0)
    for cp in copies:
        cp.start()

    def own(h):
        return _sb_block(q_ref[:, cols(h)], k_ref[:, cols(h)].astype(BF16), v_ref[:, cols(h)].astype(BF16),
                         jnp.zeros((tq, 1), F32), jnp.zeros((tq, dh), F32), scale, True)

    @pl.when(pb == 0)
    def _():
        for h in range(hb):
            r_run, acc = own(h)
            r_ref[h] = jnp.broadcast_to(r_run, (tq, 128))
            acc_ref[:, cols(h)] = acc

    for cp in copies:
        cp.wait()
    rmax = jnp.max(r_ref[...])

    @pl.when(rmax > EXP_ZERO_BELOW)
    def _():
        for h in range(hb):
            r_run, acc = _sb_block(q_ref[:, cols(h)], pk_ref[0, h].astype(BF16), pv_ref[0, h].astype(BF16),
                                   r_ref[h][:, 0:1], acc_ref[:, cols(h)], scale, False)
            r_ref[h] = jnp.broadcast_to(r_run, (tq, 128))
            acc_ref[:, cols(h)] = acc

    @pl.when(pb == pl.num_programs(2) - 1)
    def _():
        o_ref[...] = acc_ref[...].astype(o_ref.dtype)
        rm_ref[...] = jnp.broadcast_to(jnp.max(r_ref[...]), rm_ref.shape)


def _sb_prompt(q, k, v, *, n_seq, seq_len, dh, tq=256, hb=4):
    d = q.shape[1]
    nq = seq_len // tq
    assert seq_len % tq == 0 and d % (hb * dh) == 0
    hw = hb * dh
    kv = pl.BlockSpec((seq_len, hw), lambda b, h, i: (b, h))
    return pl.pallas_call(
        functools.partial(_sb_kernel, hb=hb, dh=dh, scale=dh ** -0.5),
        grid=(n_seq, d // hw, nq),
        in_specs=[pl.BlockSpec((tq, hw), lambda b, h, i: (b * nq + i, h)), kv, kv],
        out_specs=pl.BlockSpec((tq, hw), lambda b, h, i: (b * nq + i, h)),
        out_shape=jax.ShapeDtypeStruct(q.shape, BF16),
        compiler_params=_params("arbitrary", "arbitrary", "arbitrary"),
        name="sb_prompt",
    )(q, k, v)


def _sb_past(att_into, q, k, v, past_k, past_v, *, row0, seq_len, n_blocks, tk, hb=8):
    d = q.shape[1]
    n_seq, past_len, n_h, dh = past_k.shape
    hb = min(hb, n_h)
    assert past_len % tk == 0 and row0 % seq_len == 0 and n_h % hb == 0
    hw = hb * dh
    ng = n_h // hb
    rb0 = row0 // seq_len
    last = past_len // tk - 1
    new = pl.BlockSpec((seq_len, hw), lambda s, h, b: (rb0 + s, h))
    new_kv = pl.BlockSpec((seq_len, hw), lambda s, h, b: (s, h))
    hbm = pl.BlockSpec(memory_space=pl.ANY)
    return pl.pallas_call(
        functools.partial(_sb_past_kernel, hb=hb, dh=dh, tk=tk, last=last, scale=dh ** -0.5, single=n_blocks == 1),
        grid=(n_seq, ng, n_blocks),
        in_specs=[hbm, new, new_kv, new_kv, hbm, hbm],
        out_specs=[new, pl.BlockSpec((None, 8, 128), lambda s, h, b: (s * ng + h, 0, 0))],
        out_shape=[jax.ShapeDtypeStruct(q.shape, BF16), jax.ShapeDtypeStruct((n_seq * ng, 8, 128), F32)],
        scratch_shapes=[pltpu.VMEM((hb, seq_len, 128), F32), pltpu.VMEM((seq_len, hw), F32),
                        pltpu.VMEM((2, hb, tk, dh), F32), pltpu.VMEM((2, hb, tk, dh), F32),
                        pltpu.SemaphoreType.DMA((2, 2, hb))],
        input_output_aliases={0: 0},
        compiler_params=_params("arbitrary", "arbitrary", "arbitrary"),
        name="sb_past",
    )(att_into, q, k, v, past_k, past_v)


def _sb_sample(att_into, q, k, v, past_k, past_v, *, row0, seq_len, tk=256):
    past_len = past_k.shape[1]
    tk = min(tk, past_len)
    nblk = past_len // tk
    kw = dict(row0=row0, seq_len=seq_len, tk=tk)
    att, rm = _sb_past(att_into, q, k, v, past_k, past_v, n_blocks=1, **kw)
    if nblk == 1:
        return att
    return lax.cond(jnp.max(rm) > EXP_ZERO_BELOW,
                    lambda a: _sb_past(a, q, k, v, past_k, past_v, n_blocks=nblk, **kw)[0],
                    lambda a: a, att)


def _gather_rows_kernel(tok_ref, nused_ref, src_hbm, o_ref, buf_ref, sem, *, tm):
    b = pl.program_id(0)
    n_used = nused_ref[0]

    def issue(blk, slot):
        for j in range(tm):
            tok = tok_ref[blk * tm + j]
            pltpu.make_async_copy(src_hbm.at[tok // 8, tok % 8], buf_ref.at[slot, j // 8, j % 8],
                                  sem.at[slot]).start(priority=j % 2)

    @pl.when(jnp.logical_and(b == 0, n_used > 0))
    def _():
        issue(0, 0)

    @pl.when(b + 1 < n_used)
    def _():
        issue(b + 1, (b + 1) % 2)

    @pl.when(b < n_used)
    def _():
        slot = b % 2
        pltpu.make_async_copy(src_hbm.at[pl.ds(0, tm // 8)], buf_ref.at[slot], sem.at[slot]).wait()
        half = o_ref.shape[1] // 2
        words = buf_ref[slot].reshape(tm, half)
        o_ref[:, :half] = lax.bitcast_convert_type(words << 16, F32).astype(o_ref.dtype)
        o_ref[:, half:] = lax.bitcast_convert_type(words & jnp.uint32(0xFFFF0000), F32).astype(o_ref.dtype)


def _gather_rows(tok, n_used, xw, *, tm):
    t, dw = xw.shape
    n = tok.shape[0]
    assert t % 8 == 0 and tm % 8 == 0 and n % tm == 0
    return pl.pallas_call(
        functools.partial(_gather_rows_kernel, tm=tm),
        grid_spec=pltpu.PrefetchScalarGridSpec(
            num_scalar_prefetch=2, grid=(n // tm,),
            in_specs=[pl.BlockSpec(memory_space=pl.ANY)],
            out_specs=pl.BlockSpec((tm, 2 * dw), lambda b, tok, nu: (b, 0)),
            scratch_shapes=[pltpu.VMEM((2, tm // 8, 8, dw), jnp.uint32), pltpu.SemaphoreType.DMA((2,))]),
        out_shape=jax.ShapeDtypeStruct((n, 2 * dw), BF16),
        compiler_params=_params("arbitrary"),
        name="gather_rows",
    )(tok, n_used, xw.reshape(t // 8, 8, dw))


def _expert_mm_kernel(start_ref, cnt_ref, x_hbm, *rest, n_w, tm, tn, layer, n_groups, groups_per_block):
    w_hbms, o_hbm = rest[:n_w], rest[n_w]
    wbuf = rest[n_w + 1]
    wb_refs = rest[n_w + 2:2 * n_w + 2]
    xbuf, obuf, pend, sem_w, sem_in, sem_out = rest[2 * n_w + 2:]
    e, c = pl.program_id(0), pl.program_id(1)
    n_c = pl.num_programs(1)
    step = e * n_c + c
    wslot = step % 2
    has_next = step + 1 < pl.num_programs(0) * n_c
    e2 = jnp.minimum((step + 1) // n_c, pl.num_programs(0) - 1)
    c2 = (step + 1) % n_c
    n = cnt_ref[e]
    b0 = start_ref[e]
    kg = wbuf.shape[2] // n_groups

    def in_copy(first_block, r, slot):
        return pltpu.make_async_copy(x_hbm.at[pl.ds((first_block + r) * tm, tm)], xbuf.at[slot], sem_in.at[slot])

    def out_copy(r, slot):
        return pltpu.make_async_copy(obuf.at[slot], o_hbm.at[pl.ds((b0 + r) * tm, tm), pl.ds(c * tn, tn)],
                                     sem_out.at[slot])

    def wait_out(slot):
        @pl.when(pend[slot] == 1)
        def _():
            pltpu.make_async_copy(obuf.at[slot], o_hbm.at[pl.ds(0, tm), pl.ds(0, tn)], sem_out.at[slot]).wait()
            pend[slot] = 0

    def fetch_group(e_, c_, slot_, g):
        for j in range(n_w):
            pltpu.make_async_copy(w_hbms[j].at[layer, e_, pl.ds(g * kg, kg), pl.ds(c_ * tn, tn)],
                                  wbuf.at[slot_, j, pl.ds(g * kg, kg)], sem_w.at[slot_, j]).start()

    @pl.when(step == 0)
    def _():
        pend[0] = 0
        pend[1] = 0
        lax.fori_loop(0, n_groups, lambda g, carry: (fetch_group(e, c, wslot, g), carry)[1], 0)

        @pl.when(n > 0)
        def _():
            in_copy(b0, 0, 0).start()

    for j in range(n_w):
        pltpu.make_async_copy(w_hbms[j].at[layer, e, :, pl.ds(c * tn, tn)], wbuf.at[wslot, j], sem_w.at[wslot, j]).wait()
        wb_refs[j][...] = wbuf[wslot, j].astype(BF16)

    def block(r, carry):
        slot = r % 2

        @pl.when(r + 1 < n)
        def _():
            in_copy(b0, r + 1, 1 - slot).start()

        for k in range(groups_per_block):
            g = r * groups_per_block + k

            @pl.when(jnp.logical_and(has_next, g < n_groups))
            def _():
                fetch_group(e2, c2, 1 - wslot, g)

        in_copy(b0, r, slot).wait()

        wait_out(slot)
        x = xbuf[slot]
        res = jnp.dot(x, wb_refs[0][...], preferred_element_type=F32)
        if n_w == 2:
            res = res * jax.nn.sigmoid(res) * jnp.dot(x, wb_refs[1][...], preferred_element_type=F32)
        obuf[slot] = res.astype(obuf.dtype)
        out_copy(r, slot).start()
        pend[slot] = 1
        return carry

    lax.fori_loop(0, n, block, 0)

    @pl.when(has_next)
    def _():
        first = jnp.minimum(n * groups_per_block, n_groups)
        lax.fori_loop(first, n_groups, lambda g, carry: (fetch_group(e2, c2, 1 - wslot, g), carry)[1], 0)

        @pl.when(cnt_ref[e2] > 0)
        def _():
            in_copy(start_ref[e2], 0, 0).start()

    @pl.when(jnp.logical_not(has_next))
    def _():
        wait_out(0)
        wait_out(1)


def _expert_mm(x, ws, layer, blk_start, blk_cnt, *, tm, tn, out_dtype, groups_per_block, name, n_groups=8):
    pr, k = x.shape
    n_e, n = ws[0].shape[1], ws[0].shape[-1]
    tn = min(tn, n)
    n_w = len(ws)
    assert k % (8 * n_groups) == 0
    hbm = pl.BlockSpec(memory_space=pl.ANY)
    return pl.pallas_call(
        functools.partial(_expert_mm_kernel, n_w=n_w, tm=tm, tn=tn, layer=layer, n_groups=n_groups,
                          groups_per_block=groups_per_block),
        grid_spec=pltpu.PrefetchScalarGridSpec(
            num_scalar_prefetch=2, grid=(n_e, n // tn),
            in_specs=[hbm] * (1 + n_w),
            out_specs=hbm,
            scratch_shapes=[pltpu.VMEM((2, n_w, k, tn), F32)] + [pltpu.VMEM((k, tn), BF16)] * n_w + [
                pltpu.VMEM((2, tm, k), BF16), pltpu.VMEM((2, tm, tn), out_dtype), pltpu.SMEM((2,), jnp.int32),
                pltpu.SemaphoreType.DMA((2, n_w)), pltpu.SemaphoreType.DMA((2,)), pltpu.SemaphoreType.DMA((2,))]),
        out_shape=jax.ShapeDtypeStruct((pr, n), out_dtype),
        compiler_params=_params("arbitrary", "arbitrary"),
        name=name,
    )(blk_start, blk_cnt, x, *ws)


def _route_kernel(s_ref, rb_ref, idx_ref, gate_ref, rank_ref, cnt_ref, carry_ref, *, n_groups):
    @pl.when(pl.program_id(0) == 0)
    def _():
        carry_ref[...] = jnp.zeros(carry_ref.shape, F32)

    s = s_ref[...]
    tr, e = s.shape
    per = e // n_groups
    sel = s + rb_ref[...]
    lane_i = lax.broadcasted_iota(jnp.int32, (tr, e), 1)
    lane = lane_i.astype(F32)
    grp = (lane_i // per).astype(F32)
    neg = -jnp.inf

    def top1(v):
        m = jnp.max(v, axis=1, keepdims=True)
        return m, jnp.min(jnp.where(v == m, lane, float(e)), axis=1, keepdims=True)

    best = gi = None
    for g in range(n_groups):
        vg = jnp.where(grp == float(g), sel, neg)
        m1, i1 = top1(vg)
        m2, _ = top1(jnp.where(lane == i1, neg, vg))
        score = m1 + m2
        if best is None:
            best, gi = score, jnp.zeros_like(score)
        else:
            better = score > best
            gi = jnp.where(better, float(g), gi)
            best = jnp.where(better, score, best)
    vc = jnp.where(grp == gi, sel, neg)
    _, e1 = top1(vc)
    _, e2 = top1(jnp.where(lane == e1, neg, vc))
    oh1, oh2 = lane == e1, lane == e2
    g1 = jnp.sum(jnp.where(oh1, s, 0.0), axis=1, keepdims=True)
    g2 = jnp.sum(jnp.where(oh2, s, 0.0), axis=1, keepdims=True)
    den = g1 + g2

    oh = jnp.logical_or(oh1, oh2)
    earlier = (lax.broadcasted_iota(jnp.int32, (tr, tr), 1) < lax.broadcasted_iota(jnp.int32, (tr, tr), 0))
    before = jnp.dot(earlier.astype(BF16), oh.astype(BF16), preferred_element_type=F32) + carry_ref[...]
    r1 = jnp.sum(jnp.where(oh1, before, 0.0), axis=1, keepdims=True)
    r2 = jnp.sum(jnp.where(oh2, before, 0.0), axis=1, keepdims=True)
    carry = carry_ref[...] + jnp.sum(oh.astype(F32), axis=0, keepdims=True)
    carry_ref[...] = carry
    cnt_ref[...] = carry.astype(jnp.int32)

    first = lax.broadcasted_iota(jnp.int32, (tr, TOP_K), 1) == 0
    idx_ref[...] = jnp.where(first, e1, e2).astype(jnp.int32)
    gate_ref[...] = jnp.where(first, g1, g2) / den
    rank_ref[...] = jnp.where(first, r1, r2).astype(jnp.int32)


def _route(s, router_b, tr=512):
    t, e = s.shape
    tr = _pick_tile(t, tr)
    col = pl.BlockSpec((tr, TOP_K), lambda i: (i, 0))
    return pl.pallas_call(
        functools.partial(_route_kernel, n_groups=N_EXPERT_GROUPS),
        grid=(t // tr,),
        in_specs=[pl.BlockSpec((tr, e), lambda i: (i, 0)), pl.BlockSpec((1, e), lambda i: (0, 0))],
        out_specs=[col, col, col, pl.BlockSpec((1, e), lambda i: (0, 0))],
        out_shape=[jax.ShapeDtypeStruct((t, TOP_K), jnp.int32), jax.ShapeDtypeStruct((t, TOP_K), F32),
                   jax.ShapeDtypeStruct((t, TOP_K), jnp.int32), jax.ShapeDtypeStruct((1, e), jnp.int32)],
        scratch_shapes=[pltpu.VMEM((1, e), F32)],
        compiler_params=_params("arbitrary"),
        name="route",
    )(s, router_b.reshape(1, e).astype(F32))


def _moe(h, s, router_b, w_gate, w_up, w_down, layer, *, tm=256):
    t, e = s.shape
    idx, gates, rank, counts = _route(s, router_b)
    counts = counts.reshape(e)
    padded = (counts + tm - 1) // tm * tm
    pad_ends = jnp.cumsum(padded)
    pad_starts = pad_ends - padded
    slot = pad_starts[idx] + rank
    nb = -(-(t * TOP_K) // tm) + e
    tok = jnp.broadcast_to(jnp.arange(t, dtype=jnp.int32)[:, None], (t, TOP_K))
    tok_of_slot = jnp.zeros((nb * tm,), jnp.int32).at[slot.reshape(-1)].set(tok.reshape(-1))
    xs = _gather_rows(tok_of_slot, (pad_ends[-1:] // tm).astype(jnp.int32), h, tm=tm)
    blk_start, blk_cnt = (pad_starts // tm).astype(jnp.int32), (padded // tm).astype(jnp.int32)
    hid = _expert_mm(xs, [w_gate, w_up], layer, blk_start, blk_cnt, tm=tm, tn=512, out_dtype=BF16,
                     groups_per_block=3, name="moe_up")
    yb = _expert_mm(hid, [w_down], layer, blk_start, blk_cnt, tm=tm, tn=2048, out_dtype=F32,
                    groups_per_block=4, name="moe_down")
    return [yb[slot[:, k]] for k in range(TOP_K)], [gates[:, k:k + 1] for k in range(TOP_K)]


def kernel(x_prompt, x_sample, c_prompt, c_sample, state_conv, state_ssm, cache_k, cache_v, w_ada, b_ada, ln_g, ln_b, ssm_w_in, ssm_conv_w, ssm_conv_b, ssm_dt_bias, ssm_a_log, ssm_d, ssm_norm_g, ssm_w_out, sb_w_k, sb_w_v, sb_w_q, sb_w_o, router_w, router_b, moe_w_gate, moe_w_up, moe_w_down):
    bp, lp, d = x_prompt.shape
    bs, ls_, _ = x_sample.shape
    assert ls_ == ROWS and lp % ROWS == 0 and w_ada.shape[0] == DEPTH == 2
    n_heads, p, n_state = state_ssm.shape[2:]
    inner = n_heads * p
    conv_dim = ssm_conv_w.shape[-1]
    kw = ssm_conv_w.shape[1]
    n_groups = (conv_dim - inner) // (2 * n_state)
    sb_heads, dh = cache_k.shape[2:]
    past_len = cache_k.shape[1]
    tp, ts = bp * lp, bs * ls_
    t = tp + ts

    x = (x_prompt.reshape(tp, d), x_sample.reshape(ts, d))
    c_all = jnp.concatenate([c_prompt, c_sample], axis=0)
    gmap = jnp.concatenate([jnp.repeat(jnp.arange(bp), lp // ROWS), bp + jnp.arange(bs)])
    mod = _ada(c_all, w_ada, b_ada)
    modg = [mod[l][gmap][:, None, :] for l in range(DEPTH)]

    h = _modulate(x, modg[0], 1, 0)
    w_in = ssm_w_in[0]
    n_zx = inner + conv_dim
    zx = _matmul(h, w_in, n_out=n_zx, name="in_proj")
    dt_raw = _matmul(h, w_in, n_off=n_zx // 128, n_out=n_heads, tn=128, name="in_proj_dt")
    hist8 = jnp.pad(state_conv[0], ((0, 0), (8 - (kw - 1), 0), (0, 0)))
    act_p = _conv_silu(zx, inner, ssm_conv_w[0], ssm_conv_b[0], row0=0, n_seq=bp, seq_len=lp, hist8=None)
    act_s = _conv_silu(zx, inner, ssm_conv_w[0], ssm_conv_b[0], row0=tp, n_seq=bs, seq_len=ls_, hist8=hist8)
    ssd_kw = dict(n_heads=n_heads, p=p, n_state=n_state, n_groups=n_groups)
    y, hfin_p = _ssd(act_p, zx, dt_raw, ssm_dt_bias[0], ssm_a_log[0], ssm_d[0], ssm_norm_g[0], None,
                     row0=0, act_row0=0, n_seq=bp, seq_len=lp, lc=min(SSD_CHUNK, lp), **ssd_kw)
    y, hfin_s = _ssd(act_s, zx, dt_raw, ssm_dt_bias[0], ssm_a_log[0], ssm_d[0], ssm_norm_g[0],
                     state_ssm[0].reshape(bs, inner, n_state),
                     row0=tp, act_row0=0, n_seq=bs, seq_len=ls_, lc=min(SSD_CHUNK, ls_), y_into=y, **ssd_kw)
    kh = inner // 2
    m = _matmul(y, ssm_w_out[0], kblk=kh, x_koff=0, w_koff=0, name="out_proj0")
    m = _matmul(y, ssm_w_out[0], kblk=kh, x_koff=1, w_koff=1, acc=m, name="out_proj1")

    def last_rows(row0, n_seq, seq_len):
        return jnp.stack([lax.slice(zx, (row0 + (i + 1) * seq_len - (kw - 1), inner), (row0 + (i + 1) * seq_len, n_zx))
                          for i in range(n_seq)])[None]

    conv_p, conv_s = last_rows(0, bp, lp), last_rows(tp, bs, ls_)
    ssm_p = hfin_p.reshape(1, bp, n_heads, p, n_state)
    ssm_s = hfin_s.reshape(1, bs, n_heads, p, n_state)

    x, h, s = _ln(x, [m], (modg[0], 2), ln_g[0, 0], ln_b[0, 0], next_mod=(modg[0], 4, 3), h_dtype=jnp.uint32,
                  router_w=router_w)
    ys, gs = _moe(h, s, router_b, moe_w_gate, moe_w_up, moe_w_down, 0)
    x, h, xb = _ln(x, ys, (modg[0], 5), ln_g[0, 1], ln_b[0, 1], m_gates=gs, next_mod=(modg[1], 1, 0), with_xb=True)

    k_p, k_s = _matmul(xb, sb_w_k, split_rows=tp, name="k_proj")
    v_p, v_s = _matmul(xb, sb_w_v, split_rows=tp, name="v_proj")

    q = _matmul(h, sb_w_q, lead=(0,), out_dtype=BF16, name="q_proj")
    att = _sb_prompt(q, k_p, v_p, n_seq=bp, seq_len=lp, dh=dh)
    att = _sb_sample(att, q, k_s, v_s, cache_k, cache_v, row0=tp, seq_len=ls_)
    m = _matmul(att, sb_w_o, lead=(0,), name="o_proj")

    x, h, s = _ln(x, [m], (modg[1], 2), ln_g[1, 0], ln_b[1, 0], next_mod=(modg[1], 4, 3), h_dtype=jnp.uint32,
                  router_w=router_w)
    ys, gs = _moe(h, s, router_b, moe_w_gate, moe_w_up, moe_w_down, 1)
    (y_p,) = _ln(x, ys, (modg[1], 5), ln_g[1, 1], ln_b[1, 1], m_gates=gs, rows=(0, tp))
    (y_s,) = _ln(x, ys, (modg[1], 5), ln_g[1, 1], ln_b[1, 1], m_gates=gs, rows=(tp, ts))

    def heads(a, nb_, l):
        return a.reshape(nb_, l, sb_heads, dh)

    return (y_p.reshape(bp, lp, d), y_s.reshape(bs, ls_, d), conv_p, ssm_p,
            heads(k_p, bp, lp), heads(v_p, bp, lp), conv_s, ssm_s, heads(k_s, bs, ls_), heads(v_s, bs, ls_))
```

```python
import functools
import math

import jax
import jax.numpy as jnp
from jax import lax
from jax.experimental import pallas as pl
from jax.experimental.pallas import tpu as pltpu

F32 = jnp.float32
BF16 = jnp.bfloat16
HIGHEST = lax.Precision.HIGHEST

DEPTH = 2
N_EXPERT_GROUPS = 4
TOP_K = 2
DN_ALPHA = (2 * DEPTH) ** 0.25
LN_EPS = 1e-5
RMS_EPS = 1e-5
SSD_CHUNK = 128

ROWS = 32
VMEM_LIMIT = 52 * 1024 * 1024
EXP_ZERO_BELOW = -105.0


def _params(*sem):
    return pltpu.CompilerParams(dimension_semantics=sem, vmem_limit_bytes=VMEM_LIMIT)


def _split3(x):
    hi = x.astype(BF16)
    r1 = x - hi.astype(F32)
    mid = r1.astype(BF16)
    lo = (r1 - mid.astype(F32)).astype(BF16)
    return hi, mid, lo


def _dot_f32_01(x, m01):
    hi, mid, lo = _split3(x)
    d = functools.partial(jnp.dot, preferred_element_type=F32)
    return (d(lo, m01) + d(mid, m01)) + d(hi, m01)


def _dot_01_f32(m01, x):
    hi, mid, lo = _split3(x)
    d = functools.partial(jnp.dot, preferred_element_type=F32)
    return (d(m01, lo) + d(m01, mid)) + d(m01, hi)


def _pick_tile(n, cap, mult=128):
    best = None
    for c in range(mult, min(n, cap) + 1, mult):
        if n % c == 0:
            best = c
    assert best is not None, (n, cap)
    return best


def _ada_kernel(c_ref, w_ref, b_ref, o_ref):
    c = c_ref[...]
    s = (c * jax.nn.sigmoid(c)).astype(BF16)
    o_ref[...] = jnp.dot(s, w_ref[...].astype(BF16), preferred_element_type=F32) + b_ref[...]


def _ada(c_all, w_ada, b_ada):
    depth, d, n = w_ada.shape
    bc = c_all.shape[0]
    tn = 512
    return pl.pallas_call(
        _ada_kernel,
        grid=(depth, n // tn),
        in_specs=[pl.BlockSpec((bc, d), lambda l, j: (0, 0)),
                  pl.BlockSpec((None, d, tn), lambda l, j: (l, 0, j)),
                  pl.BlockSpec((None, 1, tn), lambda l, j: (l, 0, j))],
        out_specs=pl.BlockSpec((None, bc, tn), lambda l, j: (l, 0, j)),
        out_shape=jax.ShapeDtypeStruct((depth, bc, n), F32),
        compiler_params=_params("arbitrary", "arbitrary"),
        name="ada",
    )(c_all, w_ada, b_ada.reshape(depth, 1, n))


def _two_part_specs(parts, tr):
    a, b = parts
    assert a.shape[0] % tr == 0 and b.shape[0] % tr == 0
    n_first = a.shape[0] // tr
    d = a.shape[1]
    return n_first, [pl.BlockSpec((tr, d), lambda i: (jnp.minimum(i, n_first - 1), 0)),
                     pl.BlockSpec((tr, d), lambda i: (jnp.maximum(i - n_first, 0), 0))]


def _two_part_load(a_ref, b_ref, n_first):
    return jnp.where(pl.program_id(0) < n_first, a_ref[...], b_ref[...])


def _mod_kernel(xa_ref, xb_ref, sc_ref, sh_ref, h_ref, *, n_first):
    gb = sc_ref.shape[0]
    rows, d = xa_ref.shape
    x = _two_part_load(xa_ref, xb_ref, n_first).reshape(gb, rows // gb, d)
    h = x * (1.0 + sc_ref[...]) + sh_ref[...]
    h_ref[...] = h.reshape(rows, d).astype(h_ref.dtype)


def _modulate(x_parts, modg, scale_blk, shift_blk, gb=8):
    d = x_parts[0].shape[1]
    t = x_parts[0].shape[0] + x_parts[1].shape[0]
    g = modg.shape[0]
    n_first, xspecs = _two_part_specs(x_parts, gb * ROWS)
    return pl.pallas_call(
        functools.partial(_mod_kernel, n_first=n_first),
        grid=(g // gb,),
        in_specs=xspecs + [pl.BlockSpec((gb, 1, d), lambda i: (i, 0, scale_blk)),
                           pl.BlockSpec((gb, 1, d), lambda i: (i, 0, shift_blk))],
        out_specs=pl.BlockSpec((gb * ROWS, d), lambda i: (i, 0)),
        out_shape=jax.ShapeDtypeStruct((t, d), BF16),
        compiler_params=_params("arbitrary"),
        name="modulate",
    )(*x_parts, modg, modg)


def _ln_kernel(*refs, n_m, gated_m, with_h, with_router, with_xb, n_first):
    it = iter(refs)
    x_ref = next(it)
    x2_ref = next(it) if n_first is not None else None
    m_refs = [next(it) for _ in range(n_m)]
    mg_refs = [next(it) for _ in range(n_m)] if gated_m else []
    gate_ref, g_ref, b_ref = next(it), next(it), next(it)
    sc_ref = sh_ref = rw_ref = None
    if with_h:
        sc_ref, sh_ref = next(it), next(it)
    if with_router:
        rw_ref = next(it)
    xo_ref = next(it)
    h_ref = next(it) if with_h else None
    s_ref = next(it) if with_router else None
    xb_ref = next(it) if with_xb else None

    gb = gate_ref.shape[0]
    rows, d = x_ref.shape
    if gated_m:
        m = m_refs[0][...] * mg_refs[0][...]
        for mr, gr in zip(m_refs[1:], mg_refs[1:]):
            m = m + mr[...] * gr[...]
    else:
        m = m_refs[0][...]
        for mr in m_refs[1:]:
            m = m + mr[...]
    x = x_ref[...] if n_first is None else _two_part_load(x_ref, x2_ref, n_first)
    v = DN_ALPHA * x + (gate_ref[...] * m.reshape(gb, rows // gb, d)).reshape(rows, d)
    mu = jnp.mean(v, axis=-1, keepdims=True)
    vc = v - mu
    var = jnp.mean(vc * vc, axis=-1, keepdims=True)
    xn = vc * lax.rsqrt(var + LN_EPS) * g_ref[...] + b_ref[...]
    xo_ref[...] = xn
    if with_xb:
        xb_ref[...] = xn.astype(BF16)
    if with_h:
        h = xn.reshape(gb, rows // gb, d) * (1.0 + sc_ref[...]) + sh_ref[...]
        h = h.reshape(rows, d)
        if h_ref.dtype == jnp.uint32:
            bits = lax.bitcast_convert_type(h.astype(BF16).astype(F32), jnp.uint32)
            h_ref[...] = (bits[:, :d // 2] >> 16) | (bits[:, d // 2:] & jnp.uint32(0xFFFF0000))
        else:
            h_ref[...] = h.astype(h_ref.dtype)
        if with_router:
            logits = jnp.dot(h, rw_ref[...], preferred_element_type=F32, precision=HIGHEST)
            s_ref[...] = jax.nn.sigmoid(logits)


def _ln(x, ms, gate_src, ln_g, ln_b, *, m_gates=None, next_mod=None, h_dtype=BF16, router_w=None, with_xb=False,
        rows=None, gb=8):
    x_parts = x if isinstance(x, (tuple, list)) else None
    assert x_parts is None or rows is None
    d = (x_parts[0] if x_parts else x).shape[1]
    modg, gate_blk = gate_src
    tr = gb * ROWS
    n_rows = x_parts[0].shape[0] + x_parts[1].shape[0] if x_parts else x.shape[0]
    r0, t = (0, n_rows) if rows is None else rows
    assert r0 % tr == 0 and t % tr == 0
    off = r0 // tr
    row = pl.BlockSpec((tr, d), lambda i: (i + off, 0))
    out_row = pl.BlockSpec((tr, d), lambda i: (i, 0))
    vec = pl.BlockSpec((1, d), lambda i: (0, 0))

    def modspec(blk):
        return pl.BlockSpec((gb, 1, d), lambda i: (i + off, 0, blk))

    n_first = None
    if x_parts:
        n_first, xspecs = _two_part_specs(x_parts, tr)
        args = list(x_parts) + list(ms)
        specs = xspecs + [row] * len(ms)
    else:
        args = [x] + list(ms)
        specs = [row] + [row] * len(ms)
    if m_gates is not None:
        args += list(m_gates)
        specs += [pl.BlockSpec((tr, 1), lambda i: (i + off, 0))] * len(ms)
    args += [modg, ln_g.reshape(1, d), ln_b.reshape(1, d)]
    specs += [modspec(gate_blk), vec, vec]
    outs = [jax.ShapeDtypeStruct((t, d), F32)]
    out_specs = [out_row]
    row = out_row
    if next_mod is not None:
        nm, sc_blk, sh_blk = next_mod
        args += [nm, nm]
        specs += [modspec(sc_blk), modspec(sh_blk)]
        hd = d // 2 if h_dtype == jnp.uint32 else d
        outs.append(jax.ShapeDtypeStruct((t, hd), h_dtype))
        out_specs.append(pl.BlockSpec((tr, hd), lambda i: (i, 0)))
    if router_w is not None:
        e = router_w.shape[1]
        args.append(router_w)
        specs.append(pl.BlockSpec((d, e), lambda i: (0, 0)))
        outs.append(jax.ShapeDtypeStruct((t, e), F32))
        out_specs.append(pl.BlockSpec((tr, e), lambda i: (i, 0)))
    if with_xb:
        outs.append(jax.ShapeDtypeStruct((t, d), BF16))
        out_specs.append(row)
    kern = functools.partial(_ln_kernel, n_m=len(ms), gated_m=m_gates is not None,
                             with_h=next_mod is not None, with_router=router_w is not None, with_xb=with_xb,
                             n_first=n_first)
    return pl.pallas_call(
        kern, grid=(t // tr,), in_specs=specs, out_specs=out_specs, out_shape=outs,
        compiler_params=_params("arbitrary"), name="ln",
    )(*args)


def _mm_kernel(*refs, with_acc, n_first):
    x_ref, w_ref = refs[:2]
    a_ref = refs[2] if with_acc else None
    o_refs, wb_ref = refs[2 + with_acc:-1], refs[-1]

    @pl.when(pl.program_id(1) == 0)
    def _():
        wb_ref[...] = w_ref[...].astype(BF16)

    r = jnp.dot(x_ref[...], wb_ref[...], preferred_element_type=F32)
    if with_acc:
        r = r + a_ref[...]
    if n_first is None:
        o_refs[0][...] = r.astype(o_refs[0].dtype)
    else:
        @pl.when(pl.program_id(1) < n_first)
        def _():
            o_refs[0][...] = r.astype(o_refs[0].dtype)

        @pl.when(pl.program_id(1) >= n_first)
        def _():
            o_refs[1][...] = r.astype(o_refs[1].dtype)


def _matmul(x, w, *, lead=(), kblk=None, x_koff=0, w_koff=0, n_off=0, n_out=None, tm=1024, tn=512,
            out_dtype=F32, acc=None, split_rows=None, name="matmul"):
    t = x.shape[0]
    kfull, nfull = w.shape[-2:]
    kblk = kfull if kblk is None else kblk
    n_out = nfull if n_out is None else n_out
    tm = _pick_tile(t if split_rows is None else math.gcd(t, split_rows), tm)
    assert n_out % tn == 0
    nl = len(lead)
    in_specs = [pl.BlockSpec((tm, kblk), lambda j, i: (i, x_koff)),
                pl.BlockSpec((None,) * nl + (kblk, tn), lambda j, i: tuple(lead) + (w_koff, j + n_off))]
    args = [x, w]
    if acc is not None:
        in_specs.append(pl.BlockSpec((tm, tn), lambda j, i: (i, j)))
        args.append(acc)
    if split_rows is None:
        n_first = None
        out_specs = pl.BlockSpec((tm, tn), lambda j, i: (i, j))
        out_shape = jax.ShapeDtypeStruct((t, n_out), out_dtype)
    else:
        assert split_rows % tm == 0 and 0 < split_rows < t
        n_first = split_rows // tm
        out_specs = [pl.BlockSpec((tm, tn), lambda j, i: (jnp.minimum(i, n_first - 1), j)),
                     pl.BlockSpec((tm, tn), lambda j, i: (jnp.maximum(i - n_first, 0), j))]
        out_shape = [jax.ShapeDtypeStruct((split_rows, n_out), out_dtype),
                     jax.ShapeDtypeStruct((t - split_rows, n_out), out_dtype)]
    return pl.pallas_call(
        functools.partial(_mm_kernel, with_acc=acc is not None, n_first=n_first),
        grid=(n_out // tn, t // tm),
        in_specs=in_specs,
        out_specs=out_specs,
        out_shape=out_shape,
        scratch_shapes=[pltpu.VMEM((kblk, tn), BF16)],
        compiler_params=_params("arbitrary", "arbitrary"),
        name=name,
    )(*args)


def _conv_kernel(*refs, with_hist, kw):
    if with_hist:
        x_ref, hist_ref, w_ref, b_ref, o_ref, xp_ref = refs
        gs = hist_ref.shape[0]
    else:
        x_ref, w_ref, b_ref, o_ref, xp_ref = refs
        gs = 1
    tr = x_ref.shape[0] // gs
    w = w_ref[...]
    if not with_hist:
        @pl.when(pl.program_id(2) == 0)
        def _():
            xp_ref[0:8, :] = jnp.zeros((8, xp_ref.shape[1]), F32)
    for s in range(gs):
        rows = slice(s * tr, (s + 1) * tr)
        if with_hist:
            xp_ref[0:8, :] = hist_ref[s]
        xp_ref[8:8 + tr, :] = x_ref[rows, :]
        y = b_ref[...] + w[kw - 1:kw, :] * x_ref[rows, :]
        for k in range(1, kw):
            y = y + w[kw - 1 - k:kw - k, :] * xp_ref[8 - k:8 - k + tr, :]
        o_ref[rows, :] = (y * jax.nn.sigmoid(y)).astype(o_ref.dtype)
    if not with_hist:
        xp_ref[0:8, :] = xp_ref[tr:tr + 8, :]


def _conv_silu(zx, col0, conv_w, conv_b, *, row0, n_seq, seq_len, hist8, tc=1024):
    kw, c = conv_w.shape
    tr = min(seq_len, 512)
    nrt = seq_len // tr
    assert seq_len % tr == 0 and c % tc == 0 and col0 % tc == 0 and row0 % tr == 0
    with_hist = hist8 is not None
    assert not with_hist or nrt == 1
    gs = 1
    if with_hist:
        gs = max(g for g in (1, 2, 4, 8) if n_seq % g == 0 and g * tr <= 512)
    br = gs * tr
    assert row0 % br == 0
    rb0, cb0 = row0 // br, col0 // tc
    in_specs = [pl.BlockSpec((br, tc), lambda s, j, r: (rb0 + s * nrt + r, cb0 + j))]
    args = [zx]
    if with_hist:
        in_specs.append(pl.BlockSpec((gs, 8, tc), lambda s, j, r: (s, 0, j)))
        args.append(hist8)
    in_specs += [pl.BlockSpec((kw, tc), lambda s, j, r: (0, j)),
                 pl.BlockSpec((1, tc), lambda s, j, r: (0, j))]
    args += [conv_w, conv_b.reshape(1, c)]
    return pl.pallas_call(
        functools.partial(_conv_kernel, with_hist=with_hist, kw=kw),
        grid=(n_seq // gs, c // tc, nrt),
        in_specs=in_specs,
        out_specs=pl.BlockSpec((br, tc), lambda s, j, r: (s * nrt + r, j)),
        out_shape=jax.ShapeDtypeStruct((n_seq * seq_len, c), BF16),
        scratch_shapes=[pltpu.VMEM((tr + 8, tc), F32)],
        compiler_params=_params("arbitrary", "arbitrary", "arbitrary"),
        name="conv_silu",
    )(*args)


def _ssd_kernel(*refs, with_h0, with_into, hpg, p):
    if with_into:
        refs = refs[1:]
    if with_h0:
        (x_ref, b_ref, c_ref, z_ref, dt_ref, dtb_ref, alog_ref, dsk_ref, ng_ref, h0_ref,
         y_ref, hf_ref, ht_ref, yd_ref) = refs
    else:
        (x_ref, b_ref, c_ref, z_ref, dt_ref, dtb_ref, alog_ref, dsk_ref, ng_ref,
         y_ref, hf_ref, ht_ref, yd_ref) = refs
        h0_ref = None
    g = pl.program_id(1)
    ci = pl.program_id(2)
    lc, gw = x_ref.shape
    nh = dt_ref.shape[1]

    @pl.when(ci == 0)
    def _():
        if with_h0:
            ht_ref[...] = h0_ref[...].T
        else:
            ht_ref[...] = jnp.zeros(ht_ref.shape, F32)

    dt = jax.nn.softplus(dt_ref[...] + dtb_ref[...])
    da = dt * (-jnp.exp(alog_ref[...]))
    r_i = lax.broadcasted_iota(jnp.int32, (lc, lc), 0)
    c_i = lax.broadcasted_iota(jnp.int32, (lc, lc), 1)
    tril = r_i >= c_i
    cs_h = _dot_01_f32(tril.astype(BF16), da)
    head_of_col = g * hpg + lax.broadcasted_iota(jnp.int32, (nh, gw), 1) // p
    expand = (lax.broadcasted_iota(jnp.int32, (nh, gw), 0) == head_of_col).astype(BF16)
    dt_e = _dot_f32_01(dt, expand)
    cs = _dot_f32_01(cs_h, expand)
    cs_last = cs[lc - 1:lc, :]
    eye = r_i == c_i

    xs = x_ref[...].astype(F32)
    xdt = xs * dt_e
    xdt_b = xdt.astype(BF16)
    bm = b_ref[...]
    cm = c_ref[...]
    cb = lax.dot_general(cm, bm, (((1,), (1,)), ((), ())), preferred_element_type=F32)

    for j in range(hpg):
        col = jnp.broadcast_to(cs[:, j * p:j * p + 1], (lc, lc))
        row = jnp.sum(jnp.where(eye, col, 0.0), axis=0, keepdims=True)
        dec = jnp.exp(jnp.where(tril, col - row, -jnp.inf))
        mj = (cb * dec).astype(BF16)
        yd_ref[:, j * p:(j + 1) * p] = jnp.dot(mj, xdt_b[:, j * p:(j + 1) * p], preferred_element_type=F32)

    ht = ht_ref[...]
    y_off = jnp.dot(cm, ht.astype(BF16), preferred_element_type=F32) * jnp.exp(cs)
    xw = (xdt * jnp.exp(cs_last - cs)).astype(BF16)
    st_t = lax.dot_general(bm, xw, (((0,), (0,)), ((), ())), preferred_element_type=F32)
    ht_new = jnp.exp(cs_last) * ht + st_t
    ht_ref[...] = ht_new

    @pl.when(ci == pl.num_programs(2) - 1)
    def _():
        hf_ref[...] = ht_new.T

    y = yd_ref[...] + y_off + dsk_ref[...] * xs
    z = z_ref[...]
    y = y * (z * jax.nn.sigmoid(z))
    ms = jnp.mean(y * y, axis=-1, keepdims=True)
    y_ref[...] = (y * lax.rsqrt(ms + RMS_EPS) * ng_ref[...]).astype(y_ref.dtype)


def _ssd(act, zx, dt_raw, dt_bias, a_log, d_skip, norm_g, h0, *, row0, act_row0, n_seq, seq_len, lc,
         n_heads, p, n_state, n_groups, y_into=None):
    inner = n_heads * p
    hpg = n_heads // n_groups
    gw = hpg * p
    nc = seq_len // lc
    assert seq_len % lc == 0 and row0 % lc == 0 and act_row0 % lc == 0 and gw % 128 == 0 and n_state % 128 == 0
    rb0, ab0 = row0 // lc, act_row0 // lc
    bcol = inner // n_state
    with_h0 = h0 is not None
    d_e = jnp.repeat(d_skip.astype(F32), p).reshape(1, inner)
    in_specs = [
        pl.BlockSpec((lc, gw), lambda s, g, c: (ab0 + s * nc + c, g)),
        pl.BlockSpec((lc, n_state), lambda s, g, c: (ab0 + s * nc + c, bcol + g)),
        pl.BlockSpec((lc, n_state), lambda s, g, c: (ab0 + s * nc + c, bcol + n_groups + g)),
        pl.BlockSpec((lc, gw), lambda s, g, c: (rb0 + s * nc + c, g)),
        pl.BlockSpec((lc, n_heads), lambda s, g, c: (rb0 + s * nc + c, 0)),
        pl.BlockSpec((1, n_heads), lambda s, g, c: (0, 0)),
        pl.BlockSpec((1, n_heads), lambda s, g, c: (0, 0)),
        pl.BlockSpec((1, gw), lambda s, g, c: (0, g)),
        pl.BlockSpec((1, gw), lambda s, g, c: (0, g)),
    ]
    args = [act, act, act, zx, dt_raw, dt_bias.reshape(1, n_heads).astype(F32),
            a_log.reshape(1, n_heads).astype(F32), d_e, norm_g.reshape(1, inner).astype(F32)]
    if with_h0:
        in_specs.append(pl.BlockSpec((None, gw, n_state), lambda s, g, c: (s, g, 0)))
        args.append(h0)
    aliases = {}
    if y_into is not None:
        in_specs.insert(0, pl.BlockSpec(memory_space=pl.ANY))
        args.insert(0, y_into)
        aliases = {0: 0}
    return pl.pallas_call(
        functools.partial(_ssd_kernel, with_h0=with_h0, with_into=y_into is not None, hpg=hpg, p=p),
        grid=(n_seq, n_groups, nc),
        in_specs=in_specs,
        out_specs=[pl.BlockSpec((lc, gw), lambda s, g, c: (rb0 + s * nc + c, g)),
                   pl.BlockSpec((None, gw, n_state), lambda s, g, c: (s, g, 0))],
        out_shape=[jax.ShapeDtypeStruct((zx.shape[0], inner), BF16),
                   jax.ShapeDtypeStruct((n_seq, inner, n_state), F32)],
        scratch_shapes=[pltpu.VMEM((n_state, gw), F32), pltpu.VMEM((lc, gw), F32)],
        input_output_aliases=aliases,
        compiler_params=_params("arbitrary", "arbitrary", "arbitrary"),
        name="ssd",
    )(*args)


def _sb_block(q, k, v, r_run, acc, scale, diag, n_before=0, visible=None):
    tq, tk = q.shape[0], k.shape[0]
    z = lax.dot_general(q, k, (((1,), (1,)), ((), ())), preferred_element_type=F32) * scale
    ls = jnp.minimum(z, 0.0) - jnp.log(1.0 + jnp.exp(-jnp.abs(z)))
    l1 = ls - z
    if diag:
        mask = (lax.broadcasted_iota(jnp.int32, (tq, tk), 1) - n_before
                < lax.broadcasted_iota(jnp.int32, (tq, tk), 0))
        l1 = jnp.where(mask, l1, 0.0)
    if visible is not None:
        l1 = jnp.where(visible, l1, 0.0)
    later = (lax.broadcasted_iota(jnp.int32, (tk, tk), 0) > lax.broadcasted_iota(jnp.int32, (tk, tk), 1))
    after = _dot_f32_01(l1, later.astype(BF16))
    w = jnp.exp(ls + after + r_run)
    if diag:
        w = jnp.where(mask, w, 0.0)
    if visible is not None:
        w = jnp.where(visible, w, 0.0)
    acc = acc + jnp.dot(w.astype(BF16), v, preferred_element_type=F32)
    r_run = r_run + jnp.sum(l1, axis=1, keepdims=True)
    return r_run, acc


def _sb_kernel(q_ref, k_ref, v_ref, o_ref, *, hb, dh, scale):
    tq = tk = q_ref.shape[0]

    def head(ref, rows, h):
        return ref[rows, h * dh:(h + 1) * dh]

    qi = pl.program_id(2)
    own = pl.ds(pl.multiple_of(qi * tq, tq), tq)

    state = []
    for h in range(hb):
        q = head(q_ref, slice(None), h)
        r_run, acc = _sb_block(q, head(k_ref, own, h).astype(BF16), head(v_ref, own, h).astype(BF16),
                               jnp.zeros((tq, 1), F32), jnp.zeros((tq, dh), F32), scale, True)
        state += [r_run, acc]
    has_prev = qi > 0
    prev = pl.ds(pl.multiple_of(jnp.maximum(qi - 1, 0) * tk, tk), tk)
    for h in range(hb):
        q = head(q_ref, slice(None), h)
        state[2 * h], state[2 * h + 1] = _sb_block(q, head(k_ref, prev, h).astype(BF16),
                                                   head(v_ref, prev, h).astype(BF16),
                                                   state[2 * h], state[2 * h + 1], scale, False, visible=has_prev)
    rmax = functools.reduce(jnp.maximum, [jnp.max(state[2 * h]) for h in range(hb)])

    def cond(carry):
        kb, rmax = carry[0], carry[1]
        return jnp.logical_and(kb >= 0, rmax > EXP_ZERO_BELOW)

    def body(carry):
        kb, st = carry[0], list(carry[2:])
        rows = pl.ds(pl.multiple_of(kb * tk, tk), tk)
        for h in range(hb):
            q = head(q_ref, slice(None), h)
            st[2 * h], st[2 * h + 1] = _sb_block(q, head(k_ref, rows, h).astype(BF16),
                                                 head(v_ref, rows, h).astype(BF16),
                                                 st[2 * h], st[2 * h + 1], scale, False)
        rmax = functools.reduce(jnp.maximum, [jnp.max(st[2 * h]) for h in range(hb)])
        return (kb - 1, rmax, *st)

    out = lax.while_loop(cond, body, (qi - 2, rmax, *state))
    for h in range(hb):
        o_ref[:, h * dh:(h + 1) * dh] = out[3 + 2 * h].astype(o_ref.dtype)


def _sb_past_kernel(into_ref, q_ref, k_ref, v_ref, pk_hbm, pv_hbm, o_ref, rm_ref, r_ref, acc_ref, pk_ref, pv_ref, sem,
                    *, hb, dh, tk, last, scale, single):
    s, hg, pb = pl.program_id(0), pl.program_id(1), pl.program_id(2)
    tq = q_ref.shape[0]
    rows = pl.ds((last - pb) * tk, tk)

    def fetch(s_, hg_, slot_):
        out = []
        for h in range(hb):
            out.append(pltpu.make_async_copy(pk_hbm.at[s_, rows, hg_ * hb + h, :], pk_ref.at[slot_, h],
                                             sem.at[slot_, 0, h]))
            out.append(pltpu.make_async_copy(pv_hbm.at[s_, rows, hg_ * hb + h, :], pv_ref.at[slot_, h],
                                             sem.at[slot_, 1, h]))
        return out

    def cols(h):
        return slice(h * dh, (h + 1) * dh)

    if single:
        n_hg = pl.num_programs(1)
        step = s * n_hg + hg
        slot = step % 2

        @pl.when(step == 0)
        def _():
            for cp in fetch(s, hg, slot):
                cp.start()

        @pl.when(step + 1 < pl.num_programs(0) * n_hg)
        def _():
            for cp in fetch((step + 1) // n_hg, (step + 1) % n_hg, 1 - slot):
                cp.start()

        for cp in fetch(s, hg, slot):
            cp.wait()
        pad = [jnp.zeros(((-tq) % 128, dh), BF16)] if tq % 128 else []
        rmax = None
        for h in range(hb):
            kcat = jnp.concatenate([pk_ref[slot, h].astype(BF16), k_ref[:, cols(h)].astype(BF16)] + pad, axis=0)
            vcat = jnp.concatenate([pv_ref[slot, h].astype(BF16), v_ref[:, cols(h)].astype(BF16)] + pad, axis=0)
            r_run, acc = _sb_block(q_ref[:, cols(h)], kcat, vcat, jnp.zeros((tq, 1), F32),
                                   jnp.zeros((tq, dh), F32), scale, True, n_before=tk)
            o_ref[:, cols(h)] = acc.astype(o_ref.dtype)
            rmax = r_run if rmax is None else jnp.maximum(rmax, r_run)
        rm_ref[...] = jnp.broadcast_to(jnp.max(rmax), rm_ref.shape)
        return

    copies = fetch(s, hg, 0)
    for cp in copies:
        cp.start()

    def own(h):
        return _sb_block(q_ref[:, cols(h)], k_ref[:, cols(h)].astype(BF16), v_ref[:, cols(h)].astype(BF16),
                         jnp.zeros((tq, 1), F32), jnp.zeros((tq, dh), F32), scale, True)

    @pl.when(pb == 0)
    def _():
        for h in range(hb):
            r_run, acc = own(h)
            r_ref[h] = jnp.broadcast_to(r_run, (tq, 128))
            acc_ref[:, cols(h)] = acc

    for cp in copies:
        cp.wait()
    rmax = jnp.max(r_ref[...])

    @pl.when(rmax > EXP_ZERO_BELOW)
    def _():
        for h in range(hb):
            r_run, acc = _sb_block(q_ref[:, cols(h)], pk_ref[0, h].astype(BF16), pv_ref[0, h].astype(BF16),
                                   r_ref[h][:, 0:1], acc_ref[:, cols(h)], scale, False)
            r_ref[h] = jnp.broadcast_to(r_run, (tq, 128))
            acc_ref[:, cols(h)] = acc

    @pl.when(pb == pl.num_programs(2) - 1)
    def _():
        o_ref[...] = acc_ref[...].astype(o_ref.dtype)
        rm_ref[...] = jnp.broadcast_to(jnp.max(r_ref[...]), rm_ref.shape)


def _sb_prompt(q, k, v, *, n_seq, seq_len, dh, tq=256, hb=4):
    d = q.shape[1]
    nq = seq_len // tq
    assert seq_len % tq == 0 and d % (hb * dh) == 0
    hw = hb * dh
    kv = pl.BlockSpec((seq_len, hw), lambda b, h, i: (b, h))
    return pl.pallas_call(
        functools.partial(_sb_kernel, hb=hb, dh=dh, scale=dh ** -0.5),
        grid=(n_seq, d // hw, nq),
        in_specs=[pl.BlockSpec((tq, hw), lambda b, h, i: (b * nq + i, h)), kv, kv],
        out_specs=pl.BlockSpec((tq, hw), lambda b, h, i: (b * nq + i, h)),
        out_shape=jax.ShapeDtypeStruct(q.shape, BF16),
        compiler_params=_params("arbitrary", "arbitrary", "arbitrary"),
        name="sb_prompt",
    )(q, k, v)


def _sb_past(att_into, q, k, v, past_k, past_v, *, row0, seq_len, n_blocks, tk, hb=8):
    d = q.shape[1]
    n_seq, past_len, n_h, dh = past_k.shape
    hb = min(hb, n_h)
    assert past_len % tk == 0 and row0 % seq_len == 0 and n_h % hb == 0
    hw = hb * dh
    ng = n_h // hb
    rb0 = row0 // seq_len
    last = past_len // tk - 1
    new = pl.BlockSpec((seq_len, hw), lambda s, h, b: (rb0 + s, h))
    new_kv = pl.BlockSpec((seq_len, hw), lambda s, h, b: (s, h))
    hbm = pl.BlockSpec(memory_space=pl.ANY)
    return pl.pallas_call(
        functools.partial(_sb_past_kernel, hb=hb, dh=dh, tk=tk, last=last, scale=dh ** -0.5, single=n_blocks == 1),
        grid=(n_seq, ng, n_blocks),
        in_specs=[hbm, new, new_kv, new_kv, hbm, hbm],
        out_specs=[new, pl.BlockSpec((None, 8, 128), lambda s, h, b: (s * ng + h, 0, 0))],
        out_shape=[jax.ShapeDtypeStruct(q.shape, BF16), jax.ShapeDtypeStruct((n_seq * ng, 8, 128), F32)],
        scratch_shapes=[pltpu.VMEM((hb, seq_len, 128), F32), pltpu.VMEM((seq_len, hw), F32),
                        pltpu.VMEM((2, hb, tk, dh), F32), pltpu.VMEM((2, hb, tk, dh), F32),
                        pltpu.SemaphoreType.DMA((2, 2, hb))],
        input_output_aliases={0: 0},
        compiler_params=_params("arbitrary", "arbitrary", "arbitrary"),
        name="sb_past",
    )(att_into, q, k, v, past_k, past_v)


def _sb_sample(att_into, q, k, v, past_k, past_v, *, row0, seq_len, tk=256):
    past_len = past_k.shape[1]
    tk = min(tk, past_len)
    nblk = past_len // tk
    kw = dict(row0=row0, seq_len=seq_len, tk=tk)
    att, rm = _sb_past(att_into, q, k, v, past_k, past_v, n_blocks=1, **kw)
    if nblk == 1:
        return att
    return lax.cond(jnp.max(rm) > EXP_ZERO_BELOW,
                    lambda a: _sb_past(a, q, k, v, past_k, past_v, n_blocks=nblk, **kw)[0],
                    lambda a: a, att)


def _gather_rows_kernel(tok_ref, nused_ref, src_hbm, o_ref, buf_ref, sem, *, tm):
    b = pl.program_id(0)
    n_used = nused_ref[0]

    def issue(blk, slot):
        for j in range(tm):
            tok = tok_ref[blk * tm + j]
            pltpu.make_async_copy(src_hbm.at[tok // 8, tok % 8], buf_ref.at[slot, j // 8, j % 8],
                                  sem.at[slot]).start(priority=j % 2)

    @pl.when(jnp.logical_and(b == 0, n_used > 0))
    def _():
        issue(0, 0)

    @pl.when(b + 1 < n_used)
    def _():
        issue(b + 1, (b + 1) % 2)

    @pl.when(b < n_used)
    def _():
        slot = b % 2
        pltpu.make_async_copy(src_hbm.at[pl.ds(0, tm // 8)], buf_ref.at[slot], sem.at[slot]).wait()
        half = o_ref.shape[1] // 2
        words = buf_ref[slot].reshape(tm, half)
        o_ref[:, :half] = lax.bitcast_convert_type(words << 16, F32).astype(o_ref.dtype)
        o_ref[:, half:] = lax.bitcast_convert_type(words & jnp.uint32(0xFFFF0000), F32).astype(o_ref.dtype)


def _gather_rows(tok, n_used, xw, *, tm):
    t, dw = xw.shape
    n = tok.shape[0]
    assert t % 8 == 0 and tm % 8 == 0 and n % tm == 0
    return pl.pallas_call(
        functools.partial(_gather_rows_kernel, tm=tm),
        grid_spec=pltpu.PrefetchScalarGridSpec(
            num_scalar_prefetch=2, grid=(n // tm,),
            in_specs=[pl.BlockSpec(memory_space=pl.ANY)],
            out_specs=pl.BlockSpec((tm, 2 * dw), lambda b, tok, nu: (b, 0)),
            scratch_shapes=[pltpu.VMEM((2, tm // 8, 8, dw), jnp.uint32), pltpu.SemaphoreType.DMA((2,))]),
        out_shape=jax.ShapeDtypeStruct((n, 2 * dw), BF16),
        compiler_params=_params("arbitrary"),
        name="gather_rows",
    )(tok, n_used, xw.reshape(t // 8, 8, dw))


def _expert_mm_kernel(start_ref, cnt_ref, x_hbm, *rest, n_w, tm, tn, layer, n_groups, groups_per_block):
    w_hbms, o_hbm = rest[:n_w], rest[n_w]
    wbuf = rest[n_w + 1]
    wb_refs = rest[n_w + 2:2 * n_w + 2]
    xbuf, obuf, pend, sem_w, sem_in, sem_out = rest[2 * n_w + 2:]
    e, c = pl.program_id(0), pl.program_id(1)
    n_c = pl.num_programs(1)
    step = e * n_c + c
    wslot = step % 2
    has_next = step + 1 < pl.num_programs(0) * n_c
    e2 = jnp.minimum((step + 1) // n_c, pl.num_programs(0) - 1)
    c2 = (step + 1) % n_c
    n = cnt_ref[e]
    b0 = start_ref[e]
    kg = wbuf.shape[2] // n_groups

    def in_copy(first_block, r, slot):
        return pltpu.make_async_copy(x_hbm.at[pl.ds((first_block + r) * tm, tm)], xbuf.at[slot], sem_in.at[slot])

    def out_copy(r, slot):
        return pltpu.make_async_copy(obuf.at[slot], o_hbm.at[pl.ds((b0 + r) * tm, tm), pl.ds(c * tn, tn)],
                                     sem_out.at[slot])

    def wait_out(slot):
        @pl.when(pend[slot] == 1)
        def _():
            pltpu.make_async_copy(obuf.at[slot], o_hbm.at[pl.ds(0, tm), pl.ds(0, tn)], sem_out.at[slot]).wait()
            pend[slot] = 0

    def fetch_group(e_, c_, slot_, g):
        for j in range(n_w):
            pltpu.make_async_copy(w_hbms[j].at[layer, e_, pl.ds(g * kg, kg), pl.ds(c_ * tn, tn)],
                                  wbuf.at[slot_, j, pl.ds(g * kg, kg)], sem_w.at[slot_, j]).start()

    @pl.when(step == 0)
    def _():
        pend[0] = 0
        pend[1] = 0
        lax.fori_loop(0, n_groups, lambda g, carry: (fetch_group(e, c, wslot, g), carry)[1], 0)

        @pl.when(n > 0)
        def _():
            in_copy(b0, 0, 0).start()

    for j in range(n_w):
        pltpu.make_async_copy(w_hbms[j].at[layer, e, :, pl.ds(c * tn, tn)], wbuf.at[wslot, j], sem_w.at[wslot, j]).wait()
        wb_refs[j][...] = wbuf[wslot, j].astype(BF16)

    def block(r, carry):
        slot = r % 2

        @pl.when(r + 1 < n)
        def _():
            in_copy(b0, r + 1, 1 - slot).start()

        for k in range(groups_per_block):
            g = r * groups_per_block + k

            @pl.when(jnp.logical_and(has_next, g < n_groups))
            def _():
                fetch_group(e2, c2, 1 - wslot, g)

        in_copy(b0, r, slot).wait()

        wait_out(slot)
        x = xbuf[slot]
        res = jnp.dot(x, wb_refs[0][...], preferred_element_type=F32)
        if n_w == 2:
            res = res * jax.nn.sigmoid(res) * jnp.dot(x, wb_refs[1][...], preferred_element_type=F32)
        obuf[slot] = res.astype(obuf.dtype)
        out_copy(r, slot).start()
        pend[slot] = 1
        return carry

    lax.fori_loop(0, n, block, 0)

    @pl.when(has_next)
    def _():
        first = jnp.minimum(n * groups_per_block, n_groups)
        lax.fori_loop(first, n_groups, lambda g, carry: (fetch_group(e2, c2, 1 - wslot, g), carry)[1], 0)

        @pl.when(cnt_ref[e2] > 0)
        def _():
            in_copy(start_ref[e2], 0, 0).start()

    @pl.when(jnp.logical_not(has_next))
    def _():
        wait_out(0)
        wait_out(1)


def _expert_mm(x, ws, layer, blk_start, blk_cnt, *, tm, tn, out_dtype, groups_per_block, name, n_groups=8):
    pr, k = x.shape
    n_e, n = ws[0].shape[1], ws[0].shape[-1]
    tn = min(tn, n)
    n_w = len(ws)
    assert k % (8 * n_groups) == 0
    hbm = pl.BlockSpec(memory_space=pl.ANY)
    return pl.pallas_call(
        functools.partial(_expert_mm_kernel, n_w=n_w, tm=tm, tn=tn, layer=layer, n_groups=n_groups,
                          groups_per_block=groups_per_block),
        grid_spec=pltpu.PrefetchScalarGridSpec(
            num_scalar_prefetch=2, grid=(n_e, n // tn),
            in_specs=[hbm] * (1 + n_w),
            out_specs=hbm,
            scratch_shapes=[pltpu.VMEM((2, n_w, k, tn), F32)] + [pltpu.VMEM((k, tn), BF16)] * n_w + [
                pltpu.VMEM((2, tm, k), BF16), pltpu.VMEM((2, tm, tn), out_dtype), pltpu.SMEM((2,), jnp.int32),
                pltpu.SemaphoreType.DMA((2, n_w)), pltpu.SemaphoreType.DMA((2,)), pltpu.SemaphoreType.DMA((2,))]),
        out_shape=jax.ShapeDtypeStruct((pr, n), out_dtype),
        compiler_params=_params("arbitrary", "arbitrary"),
        name=name,
    )(blk_start, blk_cnt, x, *ws)


def _route_kernel(s_ref, rb_ref, idx_ref, gate_ref, rank_ref, cnt_ref, carry_ref, *, n_groups):
    @pl.when(pl.program_id(0) == 0)
    def _():
        carry_ref[...] = jnp.zeros(carry_ref.shape, F32)

    s = s_ref[...]
    tr, e = s.shape
    per = e // n_groups
    sel = s + rb_ref[...]
    lane_i = lax.broadcasted_iota(jnp.int32, (tr, e), 1)
    lane = lane_i.astype(F32)
    grp = (lane_i // per).astype(F32)
    neg = -jnp.inf

    def top1(v):
        m = jnp.max(v, axis=1, keepdims=True)
        return m, jnp.min(jnp.where(v == m, lane, float(e)), axis=1, keepdims=True)

    best = gi = None
    for g in range(n_groups):
        vg = jnp.where(grp == float(g), sel, neg)
        m1, i1 = top1(vg)
        m2, _ = top1(jnp.where(lane == i1, neg, vg))
        score = m1 + m2
        if best is None:
            best, gi = score, jnp.zeros_like(score)
        else:
            better = score > best
            gi = jnp.where(better, float(g), gi)
            best = jnp.where(better, score, best)
    vc = jnp.where(grp == gi, sel, neg)
    _, e1 = top1(vc)
    _, e2 = top1(jnp.where(lane == e1, neg, vc))
    oh1, oh2 = lane == e1, lane == e2
    g1 = jnp.sum(jnp.where(oh1, s, 0.0), axis=1, keepdims=True)
    g2 = jnp.sum(jnp.where(oh2, s, 0.0), axis=1, keepdims=True)
    den = g1 + g2

    oh = jnp.logical_or(oh1, oh2)
    earlier = (lax.broadcasted_iota(jnp.int32, (tr, tr), 1) < lax.broadcasted_iota(jnp.int32, (tr, tr), 0))
    before = jnp.dot(earlier.astype(BF16), oh.astype(BF16), preferred_element_type=F32) + carry_ref[...]
    r1 = jnp.sum(jnp.where(oh1, before, 0.0), axis=1, keepdims=True)
    r2 = jnp.sum(jnp.where(oh2, before, 0.0), axis=1, keepdims=True)
    carry = carry_ref[...] + jnp.sum(oh.astype(F32), axis=0, keepdims=True)
    carry_ref[...] = carry
    cnt_ref[...] = carry.astype(jnp.int32)

    first = lax.broadcasted_iota(jnp.int32, (tr, TOP_K), 1) == 0
    idx_ref[...] = jnp.where(first, e1, e2).astype(jnp.int32)
    gate_ref[...] = jnp.where(first, g1, g2) / den
    rank_ref[...] = jnp.where(first, r1, r2).astype(jnp.int32)


def _route(s, router_b, tr=512):
    t, e = s.shape
    tr = _pick_tile(t, tr)
    col = pl.BlockSpec((tr, TOP_K), lambda i: (i, 0))
    return pl.pallas_call(
        functools.partial(_route_kernel, n_groups=N_EXPERT_GROUPS),
        grid=(t // tr,),
        in_specs=[pl.BlockSpec((tr, e), lambda i: (i, 0)), pl.BlockSpec((1, e), lambda i: (0, 0))],
        out_specs=[col, col, col, pl.BlockSpec((1, e), lambda i: (0, 0))],
        out_shape=[jax.ShapeDtypeStruct((t, TOP_K), jnp.int32), jax.ShapeDtypeStruct((t, TOP_K), F32),
                   jax.ShapeDtypeStruct((t, TOP_K), jnp.int32), jax.ShapeDtypeStruct((1, e), jnp.int32)],
        scratch_shapes=[pltpu.VMEM((1, e), F32)],
        compiler_params=_params("arbitrary"),
        name="route",
    )(s, router_b.reshape(1, e).astype(F32))


def _moe(h, s, router_b, w_gate, w_up, w_down, layer, *, tm=256):
    t, e = s.shape
    idx, gates, rank, counts = _route(s, router_b)
    counts = counts.reshape(e)
    padded = (counts + tm - 1) // tm * tm
    pad_ends = jnp.cumsum(padded)
    pad_starts = pad_ends - padded
    slot = pad_starts[idx] + rank
    nb = -(-(t * TOP_K) // tm) + e
    tok = jnp.broadcast_to(jnp.arange(t, dtype=jnp.int32)[:, None], (t, TOP_K))
    tok_of_slot = jnp.zeros((nb * tm,), jnp.int32).at[slot.reshape(-1)].set(tok.reshape(-1))
    xs = _gather_rows(tok_of_slot, (pad_ends[-1:] // tm).astype(jnp.int32), h, tm=tm)
    blk_start, blk_cnt = (pad_starts // tm).astype(jnp.int32), (padded // tm).astype(jnp.int32)
    hid = _expert_mm(xs, [w_gate, w_up], layer, blk_start, blk_cnt, tm=tm, tn=512, out_dtype=BF16,
                     groups_per_block=3, name="moe_up")
    yb = _expert_mm(hid, [w_down], layer, blk_start, blk_cnt, tm=tm, tn=2048, out_dtype=F32,
                    groups_per_block=4, name="moe_down")
    return [yb[slot[:, k]] for k in range(TOP_K)], [gates[:, k:k + 1] for k in range(TOP_K)]


def kernel(x_prompt, x_sample, c_prompt, c_sample, state_conv, state_ssm, cache_k, cache_v, w_ada, b_ada, ln_g, ln_b, ssm_w_in, ssm_conv_w, ssm_conv_b, ssm_dt_bias, ssm_a_log, ssm_d, ssm_norm_g, ssm_w_out, sb_w_k, sb_w_v, sb_w_q, sb_w_o, router_w, router_b, moe_w_gate, moe_w_up, moe_w_down):
    bp, lp, d = x_prompt.shape
    bs, ls_, _ = x_sample.shape
    assert ls_ == ROWS and lp % ROWS == 0 and w_ada.shape[0] == DEPTH == 2
    n_heads, p, n_state = state_ssm.shape[2:]
    inner = n_heads * p
    conv_dim = ssm_conv_w.shape[-1]
    kw = ssm_conv_w.shape[1]
    n_groups = (conv_dim - inner) // (2 * n_state)
    sb_heads, dh = cache_k.shape[2:]
    past_len = cache_k.shape[1]
    tp, ts = bp * lp, bs * ls_
    t = tp + ts

    x = (x_prompt.reshape(tp, d), x_sample.reshape(ts, d))
    c_all = jnp.concatenate([c_prompt, c_sample], axis=0)
    gmap = jnp.concatenate([jnp.repeat(jnp.arange(bp), lp // ROWS), bp + jnp.arange(bs)])
    mod = _ada(c_all, w_ada, b_ada)
    modg = [mod[l][gmap][:, None, :] for l in range(DEPTH)]

    h = _modulate(x, modg[0], 1, 0)
    w_in = ssm_w_in[0]
    n_zx = inner + conv_dim
    zx = _matmul(h, w_in, n_out=n_zx, name="in_proj")
    dt_raw = _matmul(h, w_in, n_off=n_zx // 128, n_out=n_heads, tn=128, name="in_proj_dt")
    hist8 = jnp.pad(state_conv[0], ((0, 0), (8 - (kw - 1), 0), (0, 0)))
    act_p = _conv_silu(zx, inner, ssm_conv_w[0], ssm_conv_b[0], row0=0, n_seq=bp, seq_len=lp, hist8=None)
    act_s = _conv_silu(zx, inner, ssm_conv_w[0], ssm_conv_b[0], row0=tp, n_seq=bs, seq_len=ls_, hist8=hist8)
    ssd_kw = dict(n_heads=n_heads, p=p, n_state=n_state, n_groups=n_groups)
    y, hfin_p = _ssd(act_p, zx, dt_raw, ssm_dt_bias[0], ssm_a_log[0], ssm_d[0], ssm_norm_g[0], None,
                     row0=0, act_row0=0, n_seq=bp, seq_len=lp, lc=min(SSD_CHUNK, lp), **ssd_kw)
    y, hfin_s = _ssd(act_s, zx, dt_raw, ssm_dt_bias[0], ssm_a_log[0], ssm_d[0], ssm_norm_g[0],
                     state_ssm[0].reshape(bs, inner, n_state),
                     row0=tp, act_row0=0, n_seq=bs, seq_len=ls_, lc=min(SSD_CHUNK, ls_), y_into=y, **ssd_kw)
    kh = inner // 2
    m = _matmul(y, ssm_w_out[0], kblk=kh, x_koff=0, w_koff=0, name="out_proj0")
    m = _matmul(y, ssm_w_out[0], kblk=kh, x_koff=1, w_koff=1, acc=m, name="out_proj1")

    def last_rows(row0, n_seq, seq_len):
        return jnp.stack([lax.slice(zx, (row0 + (i + 1) * seq_len - (kw - 1), inner), (row0 + (i + 1) * seq_len, n_zx))
                          for i in range(n_seq)])[None]

    conv_p, conv_s = last_rows(0, bp, lp), last_rows(tp, bs, ls_)
    ssm_p = hfin_p.reshape(1, bp, n_heads, p, n_state)
    ssm_s = hfin_s.reshape(1, bs, n_heads, p, n_state)

    x, h, s = _ln(x, [m], (modg[0], 2), ln_g[0, 0], ln_b[0, 0], next_mod=(modg[0], 4, 3), h_dtype=jnp.uint32,
                  router_w=router_w)
    ys, gs = _moe(h, s, router_b, moe_w_gate, moe_w_up, moe_w_down, 0)
    x, h, xb = _ln(x, ys, (modg[0], 5), ln_g[0, 1], ln_b[0, 1], m_gates=gs, next_mod=(modg[1], 1, 0), with_xb=True)

    k_p, k_s = _matmul(xb, sb_w_k, split_rows=tp, name="k_proj")
    v_p, v_s = _matmul(xb, sb_w_v, split_rows=tp, name="v_proj")

    q = _matmul(h, sb_w_q, lead=(0,), out_dtype=BF16, name="q_proj")
    att = _sb_prompt(q, k_p, v_p, n_seq=bp, seq_len=lp, dh=dh)
    att = _sb_sample(att, q, k_s, v_s, cache_k, cache_v, row0=tp, seq_len=ls_)
    m = _matmul(att, sb_w_o, lead=(0,), name="o_proj")

    x, h, s = _ln(x, [m], (modg[1], 2), ln_g[1, 0], ln_b[1, 0], next_mod=(modg[1], 4, 3), h_dtype=jnp.uint32,
                  router_w=router_w)
    ys, gs = _moe(h, s, router_b, moe_w_gate, moe_w_up, moe_w_down, 1)
    (y_p,) = _ln(x, ys, (modg[1], 5), ln_g[1, 1], ln_b[1, 1], m_gates=gs, rows=(0, tp))
    (y_s,) = _ln(x, ys, (modg[1], 5), ln_g[1, 1], ln_b[1, 1], m_gates=gs, rows=(tp, ts))

    def heads(a, nb_, l):
        return a.reshape(nb_, l, sb_heads, dh)

    return (y_p.reshape(bp, lp, d), y_s.reshape(bs, ls_, d), conv_p, ssm_p,
            heads(k_p, bp, lp), heads(v_p, bp, lp), conv_s, ssm_s, heads(k_s, bs, ls_), heads(v_s, bs, ls_))
```

```python
import functools
import math

import jax
import jax.numpy as jnp
from jax import lax
from jax.experimental import pallas as pl
from jax.experimental.pallas import tpu as pltpu

F32 = jnp.float32
BF16 = jnp.bfloat16
HIGHEST = lax.Precision.HIGHEST

DEPTH = 2
N_EXPERT_GROUPS = 4
TOP_K = 2
DN_ALPHA = (2 * DEPTH) ** 0.25
LN_EPS = 1e-5
RMS_EPS = 1e-5
SSD_CHUNK = 128

ROWS = 32
VMEM_LIMIT = 52 * 1024 * 1024
EXP_ZERO_BELOW = -105.0


def _params(*sem):
    return pltpu.CompilerParams(dimension_semantics=sem, vmem_limit_bytes=VMEM_LIMIT)


def _split3(x):
    hi = x.astype(BF16)
    r1 = x - hi.astype(F32)
    mid = r1.astype(BF16)
    lo = (r1 - mid.astype(F32)).astype(BF16)
    return hi, mid, lo


def _dot_f32_01(x, m01):
    hi, mid, lo = _split3(x)
    d = functools.partial(jnp.dot, preferred_element_type=F32)
    return (d(lo, m01) + d(mid, m01)) + d(hi, m01)


def _dot_01_f32(m01, x):
    hi, mid, lo = _split3(x)
    d = functools.partial(jnp.dot, preferred_element_type=F32)
    return (d(m01, lo) + d(m01, mid)) + d(m01, hi)


def _pick_tile(n, cap, mult=128):
    best = None
    for c in range(mult, min(n, cap) + 1, mult):
        if n % c == 0:
            best = c
    assert best is not None, (n, cap)
    return best


def _ada_kernel(c_ref, w_ref, b_ref, o_ref):
    c = c_ref[...]
    s = (c * jax.nn.sigmoid(c)).astype(BF16)
    o_ref[...] = jnp.dot(s, w_ref[...].astype(BF16), preferred_element_type=F32) + b_ref[...]


def _ada(c_all, w_ada, b_ada):
    depth, d, n = w_ada.shape
    bc = c_all.shape[0]
    tn = 512
    return pl.pallas_call(
        _ada_kernel,
        grid=(depth, n // tn),
        in_specs=[pl.BlockSpec((bc, d), lambda l, j: (0, 0)),
                  pl.BlockSpec((None, d, tn), lambda l, j: (l, 0, j)),
                  pl.BlockSpec((None, 1, tn), lambda l, j: (l, 0, j))],
        out_specs=pl.BlockSpec((None, bc, tn), lambda l, j: (l, 0, j)),
        out_shape=jax.ShapeDtypeStruct((depth, bc, n), F32),
        compiler_params=_params("arbitrary", "arbitrary"),
        name="ada",
    )(c_all, w_ada, b_ada.reshape(depth, 1, n))


def _two_part_specs(parts, tr):
    a, b = parts
    assert a.shape[0] % tr == 0 and b.shape[0] % tr == 0
    n_first = a.shape[0] // tr
    d = a.shape[1]
    return n_first, [pl.BlockSpec((tr, d), lambda i: (jnp.minimum(i, n_first - 1), 0)),
                     pl.BlockSpec((tr, d), lambda i: (jnp.maximum(i - n_first, 0), 0))]


def _two_part_load(a_ref, b_ref, n_first):
    return jnp.where(pl.program_id(0) < n_first, a_ref[...], b_ref[...])


def _mod_kernel(xa_ref, xb_ref, sc_ref, sh_ref, h_ref, *, n_first):
    gb = sc_ref.shape[0]
    rows, d = xa_ref.shape
    x = _two_part_load(xa_ref, xb_ref, n_first).reshape(gb, rows // gb, d)
    h = x * (1.0 + sc_ref[...]) + sh_ref[...]
    h_ref[...] = h.reshape(rows, d).astype(h_ref.dtype)


def _modulate(x_parts, modg, scale_blk, shift_blk, gb=8):
    d = x_parts[0].shape[1]
    t = x_parts[0].shape[0] + x_parts[1].shape[0]
    g = modg.shape[0]
    n_first, xspecs = _two_part_specs(x_parts, gb * ROWS)
    return pl.pallas_call(
        functools.partial(_mod_kernel, n_first=n_first),
        grid=(g // gb,),
        in_specs=xspecs + [pl.BlockSpec((gb, 1, d), lambda i: (i, 0, scale_blk)),
                           pl.BlockSpec((gb, 1, d), lambda i: (i, 0, shift_blk))],
        out_specs=pl.BlockSpec((gb * ROWS, d), lambda i: (i, 0)),
        out_shape=jax.ShapeDtypeStruct((t, d), BF16),
        compiler_params=_params("arbitrary"),
        name="modulate",
    )(*x_parts, modg, modg)


def _ln_kernel(*refs, n_m, gated_m, with_h, with_router, with_xb, n_first):
    it = iter(refs)
    x_ref = next(it)
    x2_ref = next(it) if n_first is not None else None
    m_refs = [next(it) for _ in range(n_m)]
    mg_refs = [next(it) for _ in range(n_m)] if gated_m else []
    gate_ref, g_ref, b_ref = next(it), next(it), next(it)
    sc_ref = sh_ref = rw_ref = None
    if with_h:
        sc_ref, sh_ref = next(it), next(it)
    if with_router:
        rw_ref = next(it)
    xo_ref = next(it)
    h_ref = next(it) if with_h else None
    s_ref = next(it) if with_router else None
    xb_ref = next(it) if with_xb else None

    gb = gate_ref.shape[0]
    rows, d = x_ref.shape
    if gated_m:
        m = m_refs[0][...] * mg_refs[0][...]
        for mr, gr in zip(m_refs[1:], mg_refs[1:]):
            m = m + mr[...] * gr[...]
    else:
        m = m_refs[0][...]
        for mr in m_refs[1:]:
            m = m + mr[...]
    x = x_ref[...] if n_first is None else _two_part_load(x_ref, x2_ref, n_first)
    v = DN_ALPHA * x + (gate_ref[...] * m.reshape(gb, rows // gb, d)).reshape(rows, d)
    mu = jnp.mean(v, axis=-1, keepdims=True)
    vc = v - mu
    var = jnp.mean(vc * vc, axis=-1, keepdims=True)
    xn = vc * lax.rsqrt(var + LN_EPS) * g_ref[...] + b_ref[...]
    xo_ref[...] = xn
    if with_xb:
        xb_ref[...] = xn.astype(BF16)
    if with_h:
        h = xn.reshape(gb, rows // gb, d) * (1.0 + sc_ref[...]) + sh_ref[...]
        h = h.reshape(rows, d)
        if h_ref.dtype == jnp.uint32:
            bits = lax.bitcast_convert_type(h.astype(BF16).astype(F32), jnp.uint32)
            h_ref[...] = (bits[:, :d // 2] >> 16) | (bits[:, d // 2:] & jnp.uint32(0xFFFF0000))
        else:
            h_ref[...] = h.astype(h_ref.dtype)
        if with_router:
            logits = jnp.dot(h, rw_ref[...], preferred_element_type=F32, precision=HIGHEST)
            s_ref[...] = jax.nn.sigmoid(logits)


def _ln(x, ms, gate_src, ln_g, ln_b, *, m_gates=None, next_mod=None, h_dtype=BF16, router_w=None, with_xb=False,
        rows=None, gb=8):
    x_parts = x if isinstance(x, (tuple, list)) else None
    assert x_parts is None or rows is None
    d = (x_parts[0] if x_parts else x).shape[1]
    modg, gate_blk = gate_src
    tr = gb * ROWS
    n_rows = x_parts[0].shape[0] + x_parts[1].shape[0] if x_parts else x.shape[0]
    r0, t = (0, n_rows) if rows is None else rows
    assert r0 % tr == 0 and t % tr == 0
    off = r0 // tr
    row = pl.BlockSpec((tr, d), lambda i: (i + off, 0))
    out_row = pl.BlockSpec((tr, d), lambda i: (i, 0))
    vec = pl.BlockSpec((1, d), lambda i: (0, 0))

    def modspec(blk):
        return pl.BlockSpec((gb, 1, d), lambda i: (i + off, 0, blk))

    n_first = None
    if x_parts:
        n_first, xspecs = _two_part_specs(x_parts, tr)
        args = list(x_parts) + list(ms)
        specs = xspecs + [row] * len(ms)
    else:
        args = [x] + list(ms)
        specs = [row] + [row] * len(ms)
    if m_gates is not None:
        args += list(m_gates)
        specs += [pl.BlockSpec((tr, 1), lambda i: (i + off, 0))] * len(ms)
    args += [modg, ln_g.reshape(1, d), ln_b.reshape(1, d)]
    specs += [modspec(gate_blk), vec, vec]
    outs = [jax.ShapeDtypeStruct((t, d), F32)]
    out_specs = [out_row]
    row = out_row
    if next_mod is not None:
        nm, sc_blk, sh_blk = next_mod
        args += [nm, nm]
        specs += [modspec(sc_blk), modspec(sh_blk)]
        hd = d // 2 if h_dtype == jnp.uint32 else d
        outs.append(jax.ShapeDtypeStruct((t, hd), h_dtype))
        out_specs.append(pl.BlockSpec((tr, hd), lambda i: (i, 0)))
    if router_w is not None:
        e = router_w.shape[1]
        args.append(router_w)
        specs.append(pl.BlockSpec((d, e), lambda i: (0, 0)))
        outs.append(jax.ShapeDtypeStruct((t, e), F32))
        out_specs.append(pl.BlockSpec((tr, e), lambda i: (i, 0)))
    if with_xb:
        outs.append(jax.ShapeDtypeStruct((t, d), BF16))
        out_specs.append(row)
    kern = functools.partial(_ln_kernel, n_m=len(ms), gated_m=m_gates is not None,
                             with_h=next_mod is not None, with_router=router_w is not None, with_xb=with_xb,
                             n_first=n_first)
    return pl.pallas_call(
        kern, grid=(t // tr,), in_specs=specs, out_specs=out_specs, out_shape=outs,
        compiler_params=_params("arbitrary"), name="ln",
    )(*args)


def _mm_kernel(*refs, with_acc, n_first):
    x_ref, w_ref = refs[:2]
    a_ref = refs[2] if with_acc else None
    o_refs, wb_ref = refs[2 + with_acc:-1], refs[-1]

    @pl.when(pl.program_id(1) == 0)
    def _():
        wb_ref[...] = w_ref[...].astype(BF16)

    r = jnp.dot(x_ref[...], wb_ref[...], preferred_element_type=F32)
    if with_acc:
        r = r + a_ref[...]
    if n_first is None:
        o_refs[0][...] = r.astype(o_refs[0].dtype)
    else:
        @pl.when(pl.program_id(1) < n_first)
        def _():
            o_refs[0][...] = r.astype(o_refs[0].dtype)

        @pl.when(pl.program_id(1) >= n_first)
        def _():
            o_refs[1][...] = r.astype(o_refs[1].dtype)


def _matmul(x, w, *, lead=(), kblk=None, x_koff=0, w_koff=0, n_off=0, n_out=None, tm=1024, tn=512,
            out_dtype=F32, acc=None, split_rows=None, name="matmul"):
    t = x.shape[0]
    kfull, nfull = w.shape[-2:]
    kblk = kfull if kblk is None else kblk
    n_out = nfull if n_out is None else n_out
    tm = _pick_tile(t if split_rows is None else math.gcd(t, split_rows), tm)
    assert n_out % tn == 0
    nl = len(lead)
    in_specs = [pl.BlockSpec((tm, kblk), lambda j, i: (i, x_koff)),
                pl.BlockSpec((None,) * nl + (kblk, tn), lambda j, i: tuple(lead) + (w_koff, j + n_off))]
    args = [x, w]
    if acc is not None:
        in_specs.append(pl.BlockSpec((tm, tn), lambda j, i: (i, j)))
        args.append(acc)
    if split_rows is None:
        n_first = None
        out_specs = pl.BlockSpec((tm, tn), lambda j, i: (i, j))
        out_shape = jax.ShapeDtypeStruct((t, n_out), out_dtype)
    else:
        assert split_rows % tm == 0 and 0 < split_rows < t
        n_first = split_rows // tm
        out_specs = [pl.BlockSpec((tm, tn), lambda j, i: (jnp.minimum(i, n_first - 1), j)),
                     pl.BlockSpec((tm, tn), lambda j, i: (jnp.maximum(i - n_first, 0), j))]
        out_shape = [jax.ShapeDtypeStruct((split_rows, n_out), out_dtype),
                     jax.ShapeDtypeStruct((t - split_rows, n_out), out_dtype)]
    return pl.pallas_call(
        functools.partial(_mm_kernel, with_acc=acc is not None, n_first=n_first),
        grid=(n_out // tn, t // tm),
        in_specs=in_specs,
        out_specs=out_specs,
        out_shape=out_shape,
        scratch_shapes=[pltpu.VMEM((kblk, tn), BF16)],
        compiler_params=_params("arbitrary", "arbitrary"),
        name=name,
    )(*args)


def _conv_kernel(*refs, with_hist, kw):
    if with_hist:
        x_ref, hist_ref, w_ref, b_ref, o_ref, xp_ref = refs
        gs = hist_ref.shape[0]
    else:
        x_ref, w_ref, b_ref, o_ref, xp_ref = refs
        gs = 1
    tr = x_ref.shape[0] // gs
    w = w_ref[...]
    if not with_hist:
        @pl.when(pl.program_id(2) == 0)
        def _():
            xp_ref[0:8, :] = jnp.zeros((8, xp_ref.shape[1]), F32)
    for s in range(gs):
        rows = slice(s * tr, (s + 1) * tr)
        if with_hist:
            xp_ref[0:8, :] = hist_ref[s]
        xp_ref[8:8 + tr, :] = x_ref[rows, :]
        y = b_ref[...] + w[kw - 1:kw, :] * x_ref[rows, :]
        for k in range(1, kw):
            y = y + w[kw - 1 - k:kw - k, :] * xp_ref[8 - k:8 - k + tr, :]
        o_ref[rows, :] = (y * jax.nn.sigmoid(y)).astype(o_ref.dtype)
    if not with_hist:
        xp_ref[0:8, :] = xp_ref[tr:tr + 8, :]


def _conv_silu(zx, col0, conv_w, conv_b, *, row0, n_seq, seq_len, hist8, tc=1024):
    kw, c = conv_w.shape
    tr = min(seq_len, 512)
    nrt = seq_len // tr
    assert seq_len % tr == 0 and c % tc == 0 and col0 % tc == 0 and row0 % tr == 0
    with_hist = hist8 is not None
    assert not with_hist or nrt == 1
    gs = 1
    if with_hist:
        gs = max(g for g in (1, 2, 4, 8) if n_seq % g == 0 and g * tr <= 512)
    br = gs * tr
    assert row0 % br == 0
    rb0, cb0 = row0 // br, col0 // tc
    in_specs = [pl.BlockSpec((br, tc), lambda s, j, r: (rb0 + s * nrt + r, cb0 + j))]
    args = [zx]
    if with_hist:
        in_specs.append(pl.BlockSpec((gs, 8, tc), lambda s, j, r: (s, 0, j)))
        args.append(hist8)
    in_specs += [pl.BlockSpec((kw, tc), lambda s, j, r: (0, j)),
                 pl.BlockSpec((1, tc), lambda s, j, r: (0, j))]
    args += [conv_w, conv_b.reshape(1, c)]
    return pl.pallas_call(
        functools.partial(_conv_kernel, with_hist=with_hist, kw=kw),
        grid=(n_seq // gs, c // tc, nrt),
        in_specs=in_specs,
        out_specs=pl.BlockSpec((br, tc), lambda s, j, r: (s * nrt + r, j)),
        out_shape=jax.ShapeDtypeStruct((n_seq * seq_len, c), BF16),
        scratch_shapes=[pltpu.VMEM((tr + 8, tc), F32)],
        compiler_params=_params("arbitrary", "arbitrary", "arbitrary"),
        name="conv_silu",
    )(*args)


def _ssd_kernel(*refs, with_h0, with_into, hpg, p):
    if with_into:
        refs = refs[1:]
    if with_h0:
        (x_ref, b_ref, c_ref, z_ref, dt_ref, dtb_ref, alog_ref, dsk_ref, ng_ref, h0_ref,
         y_ref, hf_ref, ht_ref, yd_ref) = refs
    else:
        (x_ref, b_ref, c_ref, z_ref, dt_ref, dtb_ref, alog_ref, dsk_ref, ng_ref,
         y_ref, hf_ref, ht_ref, yd_ref) = refs
        h0_ref = None
    g = pl.program_id(1)
    ci = pl.program_id(2)
    lc, gw = x_ref.shape
    nh = dt_ref.shape[1]

    @pl.when(ci == 0)
    def _():
        if with_h0:
            ht_ref[...] = h0_ref[...].T
        else:
            ht_ref[...] = jnp.zeros(ht_ref.shape, F32)

    dt = jax.nn.softplus(dt_ref[...] + dtb_ref[...])
    da = dt * (-jnp.exp(alog_ref[...]))
    r_i = lax.broadcasted_iota(jnp.int32, (lc, lc), 0)
    c_i = lax.broadcasted_iota(jnp.int32, (lc, lc), 1)
    tril = r_i >= c_i
    cs_h = _dot_01_f32(tril.astype(BF16), da)
    head_of_col = g * hpg + lax.broadcasted_iota(jnp.int32, (nh, gw), 1) // p
    expand = (lax.broadcasted_iota(jnp.int32, (nh, gw), 0) == head_of_col).astype(BF16)
    dt_e = _dot_f32_01(dt, expand)
    cs = _dot_f32_01(cs_h, expand)
    cs_last = cs[lc - 1:lc, :]
    eye = r_i == c_i

    xs = x_ref[...].astype(F32)
    xdt = xs * dt_e
    xdt_b = xdt.astype(BF16)
    bm = b_ref[...]
    cm = c_ref[...]
    cb = lax.dot_general(cm, bm, (((1,), (1,)), ((), ())), preferred_element_type=F32)

    for j in range(hpg):
        col = jnp.broadcast_to(cs[:, j * p:j * p + 1], (lc, lc))
        row = jnp.sum(jnp.where(eye, col, 0.0), axis=0, keepdims=True)
        dec = jnp.exp(jnp.where(tril, col - row, -jnp.inf))
        mj = (cb * dec).astype(BF16)
        yd_ref[:, j * p:(j + 1) * p] = jnp.dot(mj, xdt_b[:, j * p:(j + 1) * p], preferred_element_type=F32)

    ht = ht_ref[...]
    y_off = jnp.dot(cm, ht.astype(BF16), preferred_element_type=F32) * jnp.exp(cs)
    xw = (xdt * jnp.exp(cs_last - cs)).astype(BF16)
    st_t = lax.dot_general(bm, xw, (((0,), (0,)), ((), ())), preferred_element_type=F32)
    ht_new = jnp.exp(cs_last) * ht + st_t
    ht_ref[...] = ht_new

    @pl.when(ci == pl.num_programs(2) - 1)
    def _():
        hf_ref[...] = ht_new.T

    y = yd_ref[...] + y_off + dsk_ref[...] * xs
    z = z_ref[...]
    y = y * (z * jax.nn.sigmoid(z))
    ms = jnp.mean(y * y, axis=-1, keepdims=True)
    y_ref[...] = (y * lax.rsqrt(ms + RMS_EPS) * ng_ref[...]).astype(y_ref.dtype)


def _ssd(act, zx, dt_raw, dt_bias, a_log, d_skip, norm_g, h0, *, row0, act_row0, n_seq, seq_len, lc,
         n_heads, p, n_state, n_groups, y_into=None):
    inner = n_heads * p
    hpg = n_heads // n_groups
    gw = hpg * p
    nc = seq_len // lc
    assert seq_len % lc == 0 and row0 % lc == 0 and act_row0 % lc == 0 and gw % 128 == 0 and n_state % 128 == 0
    rb0, ab0 = row0 // lc, act_row0 // lc
    bcol = inner // n_state
    with_h0 = h0 is not None
    d_e = jnp.repeat(d_skip.astype(F32), p).reshape(1, inner)
    in_specs = [
        pl.BlockSpec((lc, gw), lambda s, g, c: (ab0 + s * nc + c, g)),
        pl.BlockSpec((lc, n_state), lambda s, g, c: (ab0 + s * nc + c, bcol + g)),
        pl.BlockSpec((lc, n_state), lambda s, g, c: (ab0 + s * nc + c, bcol + n_groups + g)),
        pl.BlockSpec((lc, gw), lambda s, g, c: (rb0 + s * nc + c, g)),
        pl.BlockSpec((lc, n_heads), lambda s, g, c: (rb0 + s * nc + c, 0)),
        pl.BlockSpec((1, n_heads), lambda s, g, c: (0, 0)),
        pl.BlockSpec((1, n_heads), lambda s, g, c: (0, 0)),
        pl.BlockSpec((1, gw), lambda s, g, c: (0, g)),
        pl.BlockSpec((1, gw), lambda s, g, c: (0, g)),
    ]
    args = [act, act, act, zx, dt_raw, dt_bias.reshape(1, n_heads).astype(F32),
            a_log.reshape(1, n_heads).astype(F32), d_e, norm_g.reshape(1, inner).astype(F32)]
    if with_h0:
        in_specs.append(pl.BlockSpec((None, gw, n_state), lambda s, g, c: (s, g, 0)))
        args.append(h0)
    aliases = {}
    if y_into is not None:
        in_specs.insert(0, pl.BlockSpec(memory_space=pl.ANY))
        args.insert(0, y_into)
        aliases = {0: 0}
    return pl.pallas_call(
        functools.partial(_ssd_kernel, with_h0=with_h0, with_into=y_into is not None, hpg=hpg, p=p),
        grid=(n_seq, n_groups, nc),
        in_specs=in_specs,
        out_specs=[pl.BlockSpec((lc, gw), lambda s, g, c: (rb0 + s * nc + c, g)),
                   pl.BlockSpec((None, gw, n_state), lambda s, g, c: (s, g, 0))],
        out_shape=[jax.ShapeDtypeStruct((zx.shape[0], inner), BF16),
                   jax.ShapeDtypeStruct((n_seq, inner, n_state), F32)],
        scratch_shapes=[pltpu.VMEM((n_state, gw), F32), pltpu.VMEM((lc, gw), F32)],
        input_output_aliases=aliases,
        compiler_params=_params("arbitrary", "arbitrary", "arbitrary"),
        name="ssd",
    )(*args)


def _sb_block(q, k, v, r_run, acc, scale, diag, n_before=0, visible=None):
    tq, tk = q.shape[0], k.shape[0]
    z = lax.dot_general(q, k, (((1,), (1,)), ((), ())), preferred_element_type=F32) * scale
    ls = jnp.minimum(z, 0.0) - jnp.log(1.0 + jnp.exp(-jnp.abs(z)))
    l1 = ls - z
    if diag:
        mask = (lax.broadcasted_iota(jnp.int32, (tq, tk), 1) - n_before
                < lax.broadcasted_iota(jnp.int32, (tq, tk), 0))
        l1 = jnp.where(mask, l1, 0.0)
    if visible is not None:
        l1 = jnp.where(visible, l1, 0.0)
    later = (lax.broadcasted_iota(jnp.int32, (tk, tk), 0) > lax.broadcasted_iota(jnp.int32, (tk, tk), 1))
    after = _dot_f32_01(l1, later.astype(BF16))
    w = jnp.exp(ls + after + r_run)
    if diag:
        w = jnp.where(mask, w, 0.0)
    if visible is not None:
        w = jnp.where(visible, w, 0.0)
    acc = acc + jnp.dot(w.astype(BF16), v, preferred_element_type=F32)
    r_run = r_run + jnp.sum(l1, axis=1, keepdims=True)
    return r_run, acc


def _sb_kernel(q_ref, k_ref, v_ref, o_ref, *, hb, dh, scale):
    tq = tk = q_ref.shape[0]

    def head(ref, rows, h):
        return ref[rows, h * dh:(h + 1) * dh]

    qi = pl.program_id(2)
    own = pl.ds(pl.multiple_of(qi * tq, tq), tq)

    state = []
    for h in range(hb):
        q = head(q_ref, slice(None), h)
        r_run, acc = _sb_block(q, head(k_ref, own, h).astype(BF16), head(v_ref, own, h).astype(BF16),
                               jnp.zeros((tq, 1), F32), jnp.zeros((tq, dh), F32), scale, True)
        state += [r_run, acc]
    has_prev = qi > 0
    prev = pl.ds(pl.multiple_of(jnp.maximum(qi - 1, 0) * tk, tk), tk)
    for h in range(hb):
        q = head(q_ref, slice(None), h)
        state[2 * h], state[2 * h + 1] = _sb_block(q, head(k_ref, prev, h).astype(BF16),
                                                   head(v_ref, prev, h).astype(BF16),
                                                   state[2 * h], state[2 * h + 1], scale, False, visible=has_prev)
    rmax = functools.reduce(jnp.maximum, [jnp.max(state[2 * h]) for h in range(hb)])

    def cond(carry):
        kb, rmax = carry[0], carry[1]
        return jnp.logical_and(kb >= 0, rmax > EXP_ZERO_BELOW)

    def body(carry):
        kb, st = carry[0], list(carry[2:])
        rows = pl.ds(pl.multiple_of(kb * tk, tk), tk)
        for h in range(hb):
            q = head(q_ref, slice(None), h)
            st[2 * h], st[2 * h + 1] = _sb_block(q, head(k_ref, rows, h).astype(BF16),
                                                 head(v_ref, rows, h).astype(BF16),
                                                 st[2 * h], st[2 * h + 1], scale, False)
        rmax = functools.reduce(jnp.maximum, [jnp.max(st[2 * h]) for h in range(hb)])
        return (kb - 1, rmax, *st)

    out = lax.while_loop(cond, body, (qi - 2, rmax, *state))
    for h in range(hb):
        o_ref[:, h * dh:(h + 1) * dh] = out[3 + 2 * h].astype(o_ref.dtype)


def _sb_past_kernel(into_ref, q_ref, k_ref, v_ref, pk_hbm, pv_hbm, o_ref, rm_ref, r_ref, acc_ref, pk_ref, pv_ref, sem,
                    *, hb, dh, tk, last, scale, single):
    s, hg, pb = pl.program_id(0), pl.program_id(1), pl.program_id(2)
    tq = q_ref.shape[0]
    rows = pl.ds((last - pb) * tk, tk)

    def fetch(s_, hg_, slot_):
        out = []
        for h in range(hb):
            out.append(pltpu.make_async_copy(pk_hbm.at[s_, rows, hg_ * hb + h, :], pk_ref.at[slot_, h],
                                             sem.at[slot_, 0, h]))
            out.append(pltpu.make_async_copy(pv_hbm.at[s_, rows, hg_ * hb + h, :], pv_ref.at[slot_, h],
                                             sem.at[slot_, 1, h]))
        return out

    def cols(h):
        return slice(h * dh, (h + 1) * dh)

    if single:
        n_hg = pl.num_programs(1)
        step = s * n_hg + hg
        slot = step % 2

        @pl.when(step == 0)
        def _():
            for cp in fetch(s, hg, slot):
                cp.start()

        @pl.when(step + 1 < pl.num_programs(0) * n_hg)
        def _():
            for cp in fetch((step + 1) // n_hg, (step + 1) % n_hg, 1 - slot):
                cp.start()

        for cp in fetch(s, hg, slot):
            cp.wait()
        pad = [jnp.zeros(((-tq) % 128, dh), BF16)] if tq % 128 else []
        rmax = None
        for h in range(hb):
            kcat = jnp.concatenate([pk_ref[slot, h].astype(BF16), k_ref[:, cols(h)].astype(BF16)] + pad, axis=0)
            vcat = jnp.concatenate([pv_ref[slot, h].astype(BF16), v_ref[:, cols(h)].astype(BF16)] + pad, axis=0)
            r_run, acc = _sb_block(q_ref[:, cols(h)], kcat, vcat, jnp.zeros((tq, 1), F32),
                                   jnp.zeros((tq, dh), F32), scale, True, n_before=tk)
            o_ref[:, cols(h)] = acc.astype(o_ref.dtype)
            rmax = r_run if rmax is None else jnp.maximum(rmax, r_run)
        rm_ref[...] = jnp.broadcast_to(jnp.max(rmax), rm_ref.shape)
        return

    copies = fetch(s, hg, 0)
    for cp in copies:
        cp.start()

    def own(h):
        return _sb_block(q_ref[:, cols(h)], k_ref[:, cols(h)].astype(BF16), v_ref[:, cols(h)].astype(BF16),
                         jnp.zeros((tq, 1), F32), jnp.zeros((tq, dh), F32), scale, True)

    @pl.when(pb == 0)
    def _():
        for h in range(hb):
            r_run, acc = own(h)
            r_ref[h] = jnp.broadcast_to(r_run, (tq, 128))
            acc_ref[:, cols(h)] = acc

    for cp in copies:
        cp.wait()
    rmax = jnp.max(r_ref[...])

    @pl.when(rmax > EXP_ZERO_BELOW)
    def _():
        for h in range(hb):
            r_run, acc = _sb_block(q_ref[:, cols(h)], pk_ref[0, h].astype(BF16), pv_ref[0, h].astype(BF16),
                                   r_ref[h][:, 0:1], acc_ref[:, cols(h)], scale, False)
            r_ref[h] = jnp.broadcast_to(r_run, (tq, 128))
            acc_ref[:, cols(h)] = acc

    @pl.when(pb == pl.num_programs(2) - 1)
    def _():
        o_ref[...] = acc_ref[...].astype(o_ref.dtype)
        rm_ref[...] = jnp.broadcast_to(jnp.max(r_ref[...]), rm_ref.shape)


def _sb_prompt(q, k, v, *, n_seq, seq_len, dh, tq=256, hb=4):
    d = q.shape[1]
    nq = seq_len // tq
    assert seq_len % tq == 0 and d % (hb * dh) == 0
    hw = hb * dh
    kv = pl.BlockSpec((seq_len, hw), lambda b, h, i: (b, h))
    return pl.pallas_call(
        functools.partial(_sb_kernel, hb=hb, dh=dh, scale=dh ** -0.5),
        grid=(n_seq, d // hw, nq),
        in_specs=[pl.BlockSpec((tq, hw), lambda b, h, i: (b * nq + i, h)), kv, kv],
        out_specs=pl.BlockSpec((tq, hw), lambda b, h, i: (b * nq + i, h)),
        out_shape=jax.ShapeDtypeStruct(q.shape, BF16),
        compiler_params=_params("arbitrary", "arbitrary", "arbitrary"),
        name="sb_prompt",
    )(q, k, v)


def _sb_past(att_into, q, k, v, past_k, past_v, *, row0, seq_len, n_blocks, tk, hb=16):
    d = q.shape[1]
    n_seq, past_len, n_h, dh = past_k.shape
    hb = min(hb, n_h)
    assert past_len % tk == 0 and row0 % seq_len == 0 and n_h % hb == 0
    hw = hb * dh
    ng = n_h // hb
    rb0 = row0 // seq_len
    last = past_len // tk - 1
    new = pl.BlockSpec((seq_len, hw), lambda s, h, b: (rb0 + s, h))
    new_kv = pl.BlockSpec((seq_len, hw), lambda s, h, b: (s, h))
    hbm = pl.BlockSpec(memory_space=pl.ANY)
    return pl.pallas_call(
        functools.partial(_sb_past_kernel, hb=hb, dh=dh, tk=tk, last=last, scale=dh ** -0.5, single=n_blocks == 1),
        grid=(n_seq, ng, n_blocks),
        in_specs=[hbm, new, new_kv, new_kv, hbm, hbm],
        out_specs=[new, pl.BlockSpec((None, 8, 128), lambda s, h, b: (s * ng + h, 0, 0))],
        out_shape=[jax.ShapeDtypeStruct(q.shape, BF16), jax.ShapeDtypeStruct((n_seq * ng, 8, 128), F32)],
        scratch_shapes=[pltpu.VMEM((hb, seq_len, 128), F32), pltpu.VMEM((seq_len, hw), F32),
                        pltpu.VMEM((2, hb, tk, dh), F32), pltpu.VMEM((2, hb, tk, dh), F32),
                        pltpu.SemaphoreType.DMA((2, 2, hb))],
        input_output_aliases={0: 0},
        compiler_params=_params("arbitrary", "arbitrary", "arbitrary"),
        name="sb_past",
    )(att_into, q, k, v, past_k, past_v)


def _sb_sample(att_into, q, k, v, past_k, past_v, *, row0, seq_len, tk=256):
    past_len = past_k.shape[1]
    tk = min(tk, past_len)
    nblk = past_len // tk
    kw = dict(row0=row0, seq_len=seq_len, tk=tk)
    att, rm = _sb_past(att_into, q, k, v, past_k, past_v, n_blocks=1, **kw)
    if nblk == 1:
        return att
    return lax.cond(jnp.max(rm) > EXP_ZERO_BELOW,
                    lambda a: _sb_past(a, q, k, v, past_k, past_v, n_blocks=nblk, **kw)[0],
                    lambda a: a, att)


def _gather_rows_kernel(tok_ref, nused_ref, src_hbm, o_ref, buf_ref, sem, *, tm):
    b = pl.program_id(0)
    n_used = nused_ref[0]

    def issue(blk, slot):
        for j in range(tm):
            tok = tok_ref[blk * tm + j]
            pltpu.make_async_copy(src_hbm.at[tok // 8, tok % 8], buf_ref.at[slot, j // 8, j % 8],
                                  sem.at[slot]).start(priority=j % 2)

    @pl.when(jnp.logical_and(b == 0, n_used > 0))
    def _():
        issue(0, 0)

    @pl.when(b + 1 < n_used)
    def _():
        issue(b + 1, (b + 1) % 2)

    @pl.when(b < n_used)
    def _():
        slot = b % 2
        pltpu.make_async_copy(src_hbm.at[pl.ds(0, tm // 8)], buf_ref.at[slot], sem.at[slot]).wait()
        half = o_ref.shape[1] // 2
        words = buf_ref[slot].reshape(tm, half)
        o_ref[:, :half] = lax.bitcast_convert_type(words << 16, F32).astype(o_ref.dtype)
        o_ref[:, half:] = lax.bitcast_convert_type(words & jnp.uint32(0xFFFF0000), F32).astype(o_ref.dtype)


def _gather_rows(tok, n_used, xw, *, tm):
    t, dw = xw.shape
    n = tok.shape[0]
    assert t % 8 == 0 and tm % 8 == 0 and n % tm == 0
    return pl.pallas_call(
        functools.partial(_gather_rows_kernel, tm=tm),
        grid_spec=pltpu.PrefetchScalarGridSpec(
            num_scalar_prefetch=2, grid=(n // tm,),
            in_specs=[pl.BlockSpec(memory_space=pl.ANY)],
            out_specs=pl.BlockSpec((tm, 2 * dw), lambda b, tok, nu: (b, 0)),
            scratch_shapes=[pltpu.VMEM((2, tm // 8, 8, dw), jnp.uint32), pltpu.SemaphoreType.DMA((2,))]),
        out_shape=jax.ShapeDtypeStruct((n, 2 * dw), BF16),
        compiler_params=_params("arbitrary"),
        name="gather_rows",
    )(tok, n_used, xw.reshape(t // 8, 8, dw))


def _expert_mm_kernel(start_ref, cnt_ref, x_hbm, *rest, n_w, tm, tn, layer, n_groups, groups_per_block):
    w_hbms, o_hbm = rest[:n_w], rest[n_w]
    wbuf = rest[n_w + 1]
    wb_refs = rest[n_w + 2:2 * n_w + 2]
    xbuf, obuf, pend, sem_w, sem_in, sem_out = rest[2 * n_w + 2:]
    e, c = pl.program_id(0), pl.program_id(1)
    n_c = pl.num_programs(1)
    step = e * n_c + c
    wslot = step % 2
    has_next = step + 1 < pl.num_programs(0) * n_c
    e2 = jnp.minimum((step + 1) // n_c, pl.num_programs(0) - 1)
    c2 = (step + 1) % n_c
    n = cnt_ref[e]
    b0 = start_ref[e]
    kg = wbuf.shape[2] // n_groups

    def in_copy(first_block, r, slot):
        return pltpu.make_async_copy(x_hbm.at[pl.ds((first_block + r) * tm, tm)], xbuf.at[slot], sem_in.at[slot])

    def out_copy(r, slot):
        return pltpu.make_async_copy(obuf.at[slot], o_hbm.at[pl.ds((b0 + r) * tm, tm), pl.ds(c * tn, tn)],
                                     sem_out.at[slot])

    def wait_out(slot):
        @pl.when(pend[slot] == 1)
        def _():
            pltpu.make_async_copy(obuf.at[slot], o_hbm.at[pl.ds(0, tm), pl.ds(0, tn)], sem_out.at[slot]).wait()
            pend[slot] = 0

    def fetch_group(e_, c_, slot_, g):
        for j in range(n_w):
            pltpu.make_async_copy(w_hbms[j].at[layer, e_, pl.ds(g * kg, kg), pl.ds(c_ * tn, tn)],
                                  wbuf.at[slot_, j, pl.ds(g * kg, kg)], sem_w.at[slot_, j]).start()

    @pl.when(step == 0)
    def _():
        pend[0] = 0
        pend[1] = 0
        lax.fori_loop(0, n_groups, lambda g, carry: (fetch_group(e, c, wslot, g), carry)[1], 0)

        @pl.when(n > 0)
        def _():
            in_copy(b0, 0, 0).start()

    for j in range(n_w):
        pltpu.make_async_copy(w_hbms[j].at[layer, e, :, pl.ds(c * tn, tn)], wbuf.at[wslot, j], sem_w.at[wslot, j]).wait()
        wb_refs[j][...] = wbuf[wslot, j].astype(BF16)

    def block(r, carry):
        slot = r % 2

        @pl.when(r + 1 < n)
        def _():
            in_copy(b0, r + 1, 1 - slot).start()

        for k in range(groups_per_block):
            g = r * groups_per_block + k

            @pl.when(jnp.logical_and(has_next, g < n_groups))
            def _():
                fetch_group(e2, c2, 1 - wslot, g)

        in_copy(b0, r, slot).wait()

        wait_out(slot)
        x = xbuf[slot]
        res = jnp.dot(x, wb_refs[0][...], preferred_element_type=F32)
        if n_w == 2:
            res = res * jax.nn.sigmoid(res) * jnp.dot(x, wb_refs[1][...], preferred_element_type=F32)
        obuf[slot] = res.astype(obuf.dtype)
        out_copy(r, slot).start()
        pend[slot] = 1
        return carry

    lax.fori_loop(0, n, block, 0)

    @pl.when(has_next)
    def _():
        first = jnp.minimum(n * groups_per_block, n_groups)
        lax.fori_loop(first, n_groups, lambda g, carry: (fetch_group(e2, c2, 1 - wslot, g), carry)[1], 0)

        @pl.when(cnt_ref[e2] > 0)
        def _():
            in_copy(start_ref[e2], 0, 0).start()

    @pl.when(jnp.logical_not(has_next))
    def _():
        wait_out(0)
        wait_out(1)


def _expert_mm(x, ws, layer, blk_start, blk_cnt, *, tm, tn, out_dtype, groups_per_block, name, n_groups=8):
    pr, k = x.shape
    n_e, n = ws[0].shape[1], ws[0].shape[-1]
    tn = min(tn, n)
    n_w = len(ws)
    assert k % (8 * n_groups) == 0
    hbm = pl.BlockSpec(memory_space=pl.ANY)
    return pl.pallas_call(
        functools.partial(_expert_mm_kernel, n_w=n_w, tm=tm, tn=tn, layer=layer, n_groups=n_groups,
                          groups_per_block=groups_per_block),
        grid_spec=pltpu.PrefetchScalarGridSpec(
            num_scalar_prefetch=2, grid=(n_e, n // tn),
            in_specs=[hbm] * (1 + n_w),
            out_specs=hbm,
            scratch_shapes=[pltpu.VMEM((2, n_w, k, tn), F32)] + [pltpu.VMEM((k, tn), BF16)] * n_w + [
                pltpu.VMEM((2, tm, k), BF16), pltpu.VMEM((2, tm, tn), out_dtype), pltpu.SMEM((2,), jnp.int32),
                pltpu.SemaphoreType.DMA((2, n_w)), pltpu.SemaphoreType.DMA((2,)), pltpu.SemaphoreType.DMA((2,))]),
        out_shape=jax.ShapeDtypeStruct((pr, n), out_dtype),
        compiler_params=_params("arbitrary", "arbitrary"),
        name=name,
    )(blk_start, blk_cnt, x, *ws)


def _route_kernel(s_ref, rb_ref, idx_ref, gate_ref, rank_ref, cnt_ref, carry_ref, *, n_groups):
    @pl.when(pl.program_id(0) == 0)
    def _():
        carry_ref[...] = jnp.zeros(carry_ref.shape, F32)

    s = s_ref[...]
    tr, e = s.shape
    per = e // n_groups
    sel = s + rb_ref[...]
    lane_i = lax.broadcasted_iota(jnp.int32, (tr, e), 1)
    lane = lane_i.astype(F32)
    grp = (lane_i // per).astype(F32)
    neg = -jnp.inf

    def top1(v):
        m = jnp.max(v, axis=1, keepdims=True)
        return m, jnp.min(jnp.where(v == m, lane, float(e)), axis=1, keepdims=True)

    best = gi = None
    for g in range(n_groups):
        vg = jnp.where(grp == float(g), sel, neg)
        m1, i1 = top1(vg)
        m2, _ = top1(jnp.where(lane == i1, neg, vg))
        score = m1 + m2
        if best is None:
            best, gi = score, jnp.zeros_like(score)
        else:
            better = score > best
            gi = jnp.where(better, float(g), gi)
            best = jnp.where(better, score, best)
    vc = jnp.where(grp == gi, sel, neg)
    _, e1 = top1(vc)
    _, e2 = top1(jnp.where(lane == e1, neg, vc))
    oh1, oh2 = lane == e1, lane == e2
    g1 = jnp.sum(jnp.where(oh1, s, 0.0), axis=1, keepdims=True)
    g2 = jnp.sum(jnp.where(oh2, s, 0.0), axis=1, keepdims=True)
    den = g1 + g2

    oh = jnp.logical_or(oh1, oh2)
    earlier = (lax.broadcasted_iota(jnp.int32, (tr, tr), 1) < lax.broadcasted_iota(jnp.int32, (tr, tr), 0))
    before = jnp.dot(earlier.astype(BF16), oh.astype(BF16), preferred_element_type=F32) + carry_ref[...]
    r1 = jnp.sum(jnp.where(oh1, before, 0.0), axis=1, keepdims=True)
    r2 = jnp.sum(jnp.where(oh2, before, 0.0), axis=1, keepdims=True)
    carry = carry_ref[...] + jnp.sum(oh.astype(F32), axis=0, keepdims=True)
    carry_ref[...] = carry
    cnt_ref[...] = carry.astype(jnp.int32)

    first = lax.broadcasted_iota(jnp.int32, (tr, TOP_K), 1) == 0
    idx_ref[...] = jnp.where(first, e1, e2).astype(jnp.int32)
    gate_ref[...] = jnp.where(first, g1, g2) / den
    rank_ref[...] = jnp.where(first, r1, r2).astype(jnp.int32)


def _route(s, router_b, tr=512):
    t, e = s.shape
    tr = _pick_tile(t, tr)
    col = pl.BlockSpec((tr, TOP_K), lambda i: (i, 0))
    return pl.pallas_call(
        functools.partial(_route_kernel, n_groups=N_EXPERT_GROUPS),
        grid=(t // tr,),
        in_specs=[pl.BlockSpec((tr, e), lambda i: (i, 0)), pl.BlockSpec((1, e), lambda i: (0, 0))],
        out_specs=[col, col, col, pl.BlockSpec((1, e), lambda i: (0, 0))],
        out_shape=[jax.ShapeDtypeStruct((t, TOP_K), jnp.int32), jax.ShapeDtypeStruct((t, TOP_K), F32),
                   jax.ShapeDtypeStruct((t, TOP_K), jnp.int32), jax.ShapeDtypeStruct((1, e), jnp.int32)],
        scratch_shapes=[pltpu.VMEM((1, e), F32)],
        compiler_params=_params("arbitrary"),
        name="route",
    )(s, router_b.reshape(1, e).astype(F32))


def _moe(h, s, router_b, w_gate, w_up, w_down, layer, *, tm=256):
    t, e = s.shape
    idx, gates, rank, counts = _route(s, router_b)
    counts = counts.reshape(e)
    padded = (counts + tm - 1) // tm * tm
    pad_ends = jnp.cumsum(padded)
    pad_starts = pad_ends - padded
    slot = pad_starts[idx] + rank
    nb = -(-(t * TOP_K) // tm) + e
    tok = jnp.broadcast_to(jnp.arange(t, dtype=jnp.int32)[:, None], (t, TOP_K))
    tok_of_slot = jnp.zeros((nb * tm,), jnp.int32).at[slot.reshape(-1)].set(tok.reshape(-1))
    xs = _gather_rows(tok_of_slot, (pad_ends[-1:] // tm).astype(jnp.int32), h, tm=tm)
    blk_start, blk_cnt = (pad_starts // tm).astype(jnp.int32), (padded // tm).astype(jnp.int32)
    hid = _expert_mm(xs, [w_gate, w_up], layer, blk_start, blk_cnt, tm=tm, tn=512, out_dtype=BF16,
                     groups_per_block=3, name="moe_up")
    yb = _expert_mm(hid, [w_down], layer, blk_start, blk_cnt, tm=tm, tn=2048, out_dtype=F32,
                    groups_per_block=4, name="moe_down")
    return [yb[slot[:, k]] for k in range(TOP_K)], [gates[:, k:k + 1] for k in range(TOP_K)]


def kernel(x_prompt, x_sample, c_prompt, c_sample, state_conv, state_ssm, cache_k, cache_v, w_ada, b_ada, ln_g, ln_b, ssm_w_in, ssm_conv_w, ssm_conv_b, ssm_dt_bias, ssm_a_log, ssm_d, ssm_norm_g, ssm_w_out, sb_w_k, sb_w_v, sb_w_q, sb_w_o, router_w, router_b, moe_w_gate, moe_w_up, moe_w_down):
    bp, lp, d = x_prompt.shape
    bs, ls_, _ = x_sample.shape
    assert ls_ == ROWS and lp % ROWS == 0 and w_ada.shape[0] == DEPTH == 2
    n_heads, p, n_state = state_ssm.shape[2:]
    inner = n_heads * p
    conv_dim = ssm_conv_w.shape[-1]
    kw = ssm_conv_w.shape[1]
    n_groups = (conv_dim - inner) // (2 * n_state)
    sb_heads, dh = cache_k.shape[2:]
    past_len = cache_k.shape[1]
    tp, ts = bp * lp, bs * ls_
    t = tp + ts

    x = (x_prompt.reshape(tp, d), x_sample.reshape(ts, d))
    c_all = jnp.concatenate([c_prompt, c_sample], axis=0)
    gmap = jnp.concatenate([jnp.repeat(jnp.arange(bp), lp // ROWS), bp + jnp.arange(bs)])
    mod = _ada(c_all, w_ada, b_ada)
    modg = [mod[l][gmap][:, None, :] for l in range(DEPTH)]

    h = _modulate(x, modg[0], 1, 0)
    w_in = ssm_w_in[0]
    n_zx = inner + conv_dim
    zx = _matmul(h, w_in, n_out=n_zx, name="in_proj")
    dt_raw = _matmul(h, w_in, n_off=n_zx // 128, n_out=n_heads, tn=128, name="in_proj_dt")
    hist8 = jnp.pad(state_conv[0], ((0, 0), (8 - (kw - 1), 0), (0, 0)))
    act_p = _conv_silu(zx, inner, ssm_conv_w[0], ssm_conv_b[0], row0=0, n_seq=bp, seq_len=lp, hist8=None)
    act_s = _conv_silu(zx, inner, ssm_conv_w[0], ssm_conv_b[0], row0=tp, n_seq=bs, seq_len=ls_, hist8=hist8)
    ssd_kw = dict(n_heads=n_heads, p=p, n_state=n_state, n_groups=n_groups)
    y, hfin_p = _ssd(act_p, zx, dt_raw, ssm_dt_bias[0], ssm_a_log[0], ssm_d[0], ssm_norm_g[0], None,
                     row0=0, act_row0=0, n_seq=bp, seq_len=lp, lc=min(SSD_CHUNK, lp), **ssd_kw)
    y, hfin_s = _ssd(act_s, zx, dt_raw, ssm_dt_bias[0], ssm_a_log[0], ssm_d[0], ssm_norm_g[0],
                     state_ssm[0].reshape(bs, inner, n_state),
                     row0=tp, act_row0=0, n_seq=bs, seq_len=ls_, lc=min(SSD_CHUNK, ls_), y_into=y, **ssd_kw)
    kh = inner // 2
    m = _matmul(y, ssm_w_out[0], kblk=kh, x_koff=0, w_koff=0, name="out_proj0")
    m = _matmul(y, ssm_w_out[0], kblk=kh, x_koff=1, w_koff=1, acc=m, name="out_proj1")

    def last_rows(row0, n_seq, seq_len):
        return jnp.stack([lax.slice(zx, (row0 + (i + 1) * seq_len - (kw - 1), inner), (row0 + (i + 1) * seq_len, n_zx))
                          for i in range(n_seq)])[None]

    conv_p, conv_s = last_rows(0, bp, lp), last_rows(tp, bs, ls_)
    ssm_p = hfin_p.reshape(1, bp, n_heads, p, n_state)
    ssm_s = hfin_s.reshape(1, bs, n_heads, p, n_state)

    x, h, s = _ln(x, [m], (modg[0], 2), ln_g[0, 0], ln_b[0, 0], next_mod=(modg[0], 4, 3), h_dtype=jnp.uint32,
                  router_w=router_w)
    ys, gs = _moe(h, s, router_b, moe_w_gate, moe_w_up, moe_w_down, 0)
    x, h, xb = _ln(x, ys, (modg[0], 5), ln_g[0, 1], ln_b[0, 1], m_gates=gs, next_mod=(modg[1], 1, 0), with_xb=True)

    k_p, k_s = _matmul(xb, sb_w_k, split_rows=tp, name="k_proj")
    v_p, v_s = _matmul(xb, sb_w_v, split_rows=tp, name="v_proj")

    q = _matmul(h, sb_w_q, lead=(0,), out_dtype=BF16, name="q_proj")
    att = _sb_prompt(q, k_p, v_p, n_seq=bp, seq_len=lp, dh=dh)
    att = _sb_sample(att, q, k_s, v_s, cache_k, cache_v, row0=tp, seq_len=ls_)
    m = _matmul(att, sb_w_o, lead=(0,), name="o_proj")

    x, h, s = _ln(x, [m], (modg[1], 2), ln_g[1, 0], ln_b[1, 0], next_mod=(modg[1], 4, 3), h_dtype=jnp.uint32,
                  router_w=router_w)
    ys, gs = _moe(h, s, router_b, moe_w_gate, moe_w_up, moe_w_down, 1)
    (y_p,) = _ln(x, ys, (modg[1], 5), ln_g[1, 1], ln_b[1, 1], m_gates=gs, rows=(0, tp))
    (y_s,) = _ln(x, ys, (modg[1], 5), ln_g[1, 1], ln_b[1, 1], m_gates=gs, rows=(tp, ts))

    def heads(a, nb_, l):
        return a.reshape(nb_, l, sb_heads, dh)

    return (y_p.reshape(bp, lp, d), y_s.reshape(bs, ls_, d), conv_p, ssm_p,
            heads(k_p, bp, lp), heads(v_p, bp, lp), conv_s, ssm_s, heads(k_s, bs, ls_), heads(v_s, bs, ls_))
```
